```python
import math
import jax
import jax.numpy as jnp
from jax import lax
import numpy as np

D_MODEL = 2048
BATCH = 16
SEQ = 2048
DEPTH = 2

MIX_WIDTH = D_MODEL
RWKV_WIDTH = MIX_WIDTH // 2
RWKV_HEAD = 64
RWKV_HEADS = RWKV_WIDTH // RWKV_HEAD
DECAY_RANK = 96
ICLR_RANK = 96
GATE_RANK = 256
RWKV_GN_EPS = 64e-5
S5_WIDTH = MIX_WIDTH - RWKV_WIDTH
S5_GROUP = 16
S5_GROUPS = S5_WIDTH // S5_GROUP
S5_STATE = 64
LRU_WIDTH = D_MODEL
LRU_HEADS = 8
LRU_BLOCK = LRU_WIDTH // LRU_HEADS
LRU_C = 8.0
CONV_WIDTH = 4
D_FF = 4 * D_MODEL
NORM_EPS = 1e-6
N_HYB = (DEPTH + 1) // 2
N_REC = DEPTH // 2

kernel_name = 'hybrid_rwkv7_s5_rglru_block'


def rms_norm(x, g):
    xf = x.astype(jnp.float32)
    y = xf * lax.rsqrt(jnp.mean(xf * xf, axis=-1, keepdims=True) + NORM_EPS)
    return (y * g.astype(jnp.float32)).astype(x.dtype)


def token_shift(x):
    return jnp.pad(x[:, :-1], ((0, 0), (1, 0), (0, 0)))


def linear_scan_combine(e1, e2):
    a1, b1 = e1
    a2, b2 = e2
    return a1 * a2, a2 * b1 + b2


def complex_scan_combine(e1, e2):
    a1r, a1i, b1r, b1i = e1
    a2r, a2i, b2r, b2i = e2
    return (a1r * a2r - a1i * a2i,
            a1r * a2i + a1i * a2r,
            a2r * b1r - a2i * b1i + b2r,
            a2r * b1i + a2i * b1r + b2i)


def rwkv7_time_mix(h, p_r, p_k, p_v, mu_rkv, mu_wag, w0, w1, w2, a0, a1, a2, g1, g2,
                   k_k, k_a, r_k, ln_w, ln_b):
    f32 = jnp.float32
    bsz, seq, _ = h.shape
    dh = token_shift(h) - h
    xw = h + dh * mu_wag[0]
    xa = h + dh * mu_wag[1]
    xg = h + dh * mu_wag[2]
    r = (p_r + (token_shift(p_r) - p_r) * mu_rkv[0]).astype(f32)
    k = (p_k + (token_shift(p_k) - p_k) * mu_rkv[1]).astype(f32)
    v = (p_v + (token_shift(p_v) - p_v) * mu_rkv[2]).astype(f32)
    w_log = -jax.nn.softplus(-(w0 + jnp.tanh(xw @ w1) @ w2).astype(f32)) - 0.5
    decay = jnp.exp(-jnp.exp(w_log))
    a = jax.nn.sigmoid((a0 + (xa @ a1) @ a2).astype(f32))
    g = (jax.nn.sigmoid(xg @ g1) @ g2).astype(f32)

    def heads(t):
        return t.reshape(bsz, seq, RWKV_HEADS, RWKV_HEAD)

    kk = heads(k * k_k.astype(f32))
    kk = kk / jnp.maximum(jnp.sqrt(jnp.sum(kk * kk, axis=-1, keepdims=True)), 1e-12)
    k = k * (1.0 + (a - 1.0) * k_a.astype(f32))
    rh, kh, vh, wh, ah = heads(r), heads(k), heads(v), heads(decay), heads(a)

    def step(state, inp):
        r_t, w_t, k_t, v_t, kk_t, a_t = inp
        removed = jnp.einsum('bhvk,bhk->bhv', state, kk_t)
        state = (state * w_t[:, :, None, :]
                 - removed[..., None] * (kk_t * a_t)[:, :, None, :]
                 + v_t[..., None] * k_t[:, :, None, :])
        return state, jnp.einsum('bhvk,bhk->bhv', state, r_t)

    def seq_first(t):
        return jnp.moveaxis(t, 1, 0)

    s0 = jnp.zeros((bsz, RWKV_HEADS, RWKV_HEAD, RWKV_HEAD), f32)
    xs = (seq_first(rh), seq_first(wh), seq_first(kh), seq_first(vh), seq_first(kk), seq_first(ah))
    _, y = lax.scan(step, s0, xs)
    y = jnp.moveaxis(y, 0, 1)
    mean = jnp.mean(y, axis=-1, keepdims=True)
    var = jnp.mean(jnp.square(y - mean), axis=-1, keepdims=True)
    y = ((y - mean) * lax.rsqrt(var + RWKV_GN_EPS)).reshape(bsz, seq, RWKV_WIDTH)
    y = y * ln_w.astype(f32) + ln_b.astype(f32)
    bonus = jnp.sum(rh * kh * r_k.astype(f32), axis=-1, keepdims=True) * vh
    y = y + bonus.reshape(bsz, seq, RWKV_WIDTH)
    return (y * g).astype(h.dtype)


def s5_ssm(u, lam_re, lam_im, log_dt, b_re, b_im, c_re, c_im, d_skip, glu_w, glu_b):
    f32 = jnp.float32
    bsz, seq, _ = u.shape
    uf = u.astype(f32)
    ug = uf.reshape(bsz, seq, S5_GROUPS, S5_GROUP)
    lr, li = lam_re.astype(f32), lam_im.astype(f32)
    dt = jnp.exp(log_dt.astype(f32))[:, None]
    mag = jnp.exp(lr * dt)
    lb_re, lb_im = mag * jnp.cos(li * dt), mag * jnp.sin(li * dt)
    den = lr * lr + li * li
    z_re = ((lb_re - 1.0) * lr + lb_im * li) / den
    z_im = (lb_im * lr - (lb_re - 1.0) * li) / den
    br, bi = b_re.astype(f32), b_im.astype(f32)
    bb_re = z_re[..., None] * br - z_im[..., None] * bi
    bb_im = z_re[..., None] * bi + z_im[..., None] * br
    bu_re = jnp.einsum('blgh,gph->lbgp', ug, bb_re)
    bu_im = jnp.einsum('blgh,gph->lbgp', ug, bb_im)
    a_re = jnp.broadcast_to(lb_re, (seq, 1, S5_GROUPS, S5_STATE))
    a_im = jnp.broadcast_to(lb_im, (seq, 1, S5_GROUPS, S5_STATE))
    _, _, s_re, s_im = lax.associative_scan(complex_scan_combine, (a_re, a_im, bu_re, bu_im), axis=0)
    y = (jnp.einsum('lbgp,ghp->blgh', s_re, c_re.astype(f32))
         - jnp.einsum('lbgp,ghp->blgh', s_im, c_im.astype(f32))).reshape(bsz, seq, S5_WIDTH)
    y = jax.nn.gelu(y + d_skip.astype(f32) * uf)
    y = y * jax.nn.sigmoid(y @ glu_w.astype(f32) + glu_b.astype(f32))
    return y.astype(u.dtype)


def causal_depthwise_conv(x, w, b):
    out = lax.conv_general_dilated(
        x, w[:, None, :].astype(x.dtype), window_strides=(1,),
        padding=[(CONV_WIDTH - 1, 0)], dimension_numbers=('NWC', 'WIO', 'NWC'),
        feature_group_count=x.shape[-1])
    return out + b


def rg_lru(x, w_a, b_a, w_x, b_x, lam):
    f32 = jnp.float32
    bsz, seq, width = x.shape
    xh = x.reshape(bsz, seq, LRU_HEADS, LRU_BLOCK)
    gate_r = jax.nn.sigmoid((jnp.einsum('blhi,hij->blhj', xh, w_a).reshape(bsz, seq, width) + b_a).astype(f32))
    gate_i = jax.nn.sigmoid((jnp.einsum('blhi,hij->blhj', xh, w_x).reshape(bsz, seq, width) + b_x).astype(f32))
    log_a = LRU_C * gate_r * jax.nn.log_sigmoid(lam.astype(f32))
    mult = jnp.sqrt(-jnp.expm1(2.0 * log_a))
    mult = mult.at[:, 0].set(1.0)
    b = x.astype(f32) * gate_i * mult
    _, hs = lax.associative_scan(
        linear_scan_combine, (jnp.moveaxis(jnp.exp(log_a), 1, 0), jnp.moveaxis(b, 1, 0)), axis=0)
    return jnp.moveaxis(hs, 0, 1).astype(x.dtype)


def setup_inputs(seed: int = 0) -> dict:
    keys = iter(jax.random.split(jax.random.key(seed), 48))
    f32 = jnp.float32

    def normal(shape, std):
        return jax.random.normal(next(keys), shape, f32) * std

    def uniform(shape, lo, hi):
        return jax.random.uniform(next(keys), shape, f32, lo, hi)

    ch = jnp.arange(RWKV_WIDTH, dtype=f32) / (RWKV_WIDTH - 1)
    decay_speed = -7.0 + 5.0 * ch ** 0.85
    rad = uniform((N_REC, LRU_WIDTH), 0.9, 0.999)
    s = rad ** (1.0 / LRU_C)
    lam_im = jnp.broadcast_to(math.pi * jnp.arange(S5_STATE, dtype=f32), (N_HYB, S5_GROUPS, S5_STATE))
    return {
        'x': normal((BATCH, SEQ, D_MODEL), 1.0),
        'norm_mix_g': 1.0 + normal((DEPTH, D_MODEL), 0.02),
        'norm_mlp_g': 1.0 + normal((DEPTH, D_MODEL), 0.02),
        'norm_f_g': 1.0 + normal((D_MODEL,), 0.02),
        'mlp_w1': normal((DEPTH, D_MODEL, D_FF), D_MODEL ** -0.5),
        'mlp_w2': normal((DEPTH, D_FF, D_MODEL), D_FF ** -0.5),
        'hy_w_in': normal((N_HYB, D_MODEL, 3 * RWKV_WIDTH + S5_WIDTH), D_MODEL ** -0.5),
        'hy_w_out': normal((N_HYB, RWKV_WIDTH + S5_WIDTH, D_MODEL), MIX_WIDTH ** -0.5),
        'rw_mu_rkv': uniform((N_HYB, 3, RWKV_WIDTH), 0.0, 1.0),
        'rw_mu_wag': uniform((N_HYB, 3, D_MODEL), 0.0, 1.0),
        'rw_w0': decay_speed + 0.5 + normal((N_HYB, RWKV_WIDTH), 0.1),
        'rw_w1': normal((N_HYB, D_MODEL, DECAY_RANK), D_MODEL ** -0.5),
        'rw_w2': normal((N_HYB, DECAY_RANK, RWKV_WIDTH), 0.1 * DECAY_RANK ** -0.5),
        'rw_a0': normal((N_HYB, RWKV_WIDTH), 0.1),
        'rw_a1': normal((N_HYB, D_MODEL, ICLR_RANK), D_MODEL ** -0.5),
        'rw_a2': normal((N_HYB, ICLR_RANK, RWKV_WIDTH), ICLR_RANK ** -0.5),
        'rw_g1': normal((N_HYB, D_MODEL, GATE_RANK), D_MODEL ** -0.5),
        'rw_g2': normal((N_HYB, GATE_RANK, RWKV_WIDTH), GATE_RANK ** -0.5),
        'rw_k_k': 0.85 + normal((N_HYB, RWKV_WIDTH), 0.02),
        'rw_k_a': 1.0 + normal((N_HYB, RWKV_WIDTH), 0.02),
        'rw_r_k': normal((N_HYB, RWKV_HEADS, RWKV_HEAD), 0.1),
        'rw_ln_w': 1.0 + normal((N_HYB, RWKV_WIDTH), 0.02),
        'rw_ln_b': normal((N_HYB, RWKV_WIDTH), 0.01),
        's5_lam_re': -0.5 * jnp.exp(normal((N_HYB, S5_GROUPS, S5_STATE), 0.05)),
        's5_lam_im': lam_im,
        's5_log_dt': uniform((N_HYB, S5_GROUPS), math.log(1e-3), math.log(1e-1)),
        's5_b_re': normal((N_HYB, S5_GROUPS, S5_STATE, S5_GROUP), (2 * S5_GROUP) ** -0.5),
        's5_b_im': normal((N_HYB, S5_GROUPS, S5_STATE, S5_GROUP), (2 * S5_GROUP) ** -0.5),
        's5_c_re': normal((N_HYB, S5_GROUPS, S5_GROUP, S5_STATE), S5_STATE ** -0.5),
        's5_c_im': normal((N_HYB, S5_GROUPS, S5_GROUP, S5_STATE), S5_STATE ** -0.5),
        's5_d': normal((N_HYB, S5_WIDTH), 1.0),
        's5_glu_w': normal((N_HYB, S5_WIDTH, S5_WIDTH), S5_WIDTH ** -0.5),
        's5_glu_b': normal((N_HYB, S5_WIDTH), 0.01),
        'rg_w_in': normal((N_REC, D_MODEL, 2 * LRU_WIDTH), D_MODEL ** -0.5),
        'rg_w_out': normal((N_REC, LRU_WIDTH, D_MODEL), LRU_WIDTH ** -0.5),
        'rg_conv_w': normal((N_REC, CONV_WIDTH, LRU_WIDTH), CONV_WIDTH ** -0.5),
        'rg_conv_b': normal((N_REC, LRU_WIDTH), 0.01),
        'rg_w_a': normal((N_REC, LRU_HEADS, LRU_BLOCK, LRU_BLOCK), LRU_BLOCK ** -0.5),
        'rg_b_a': normal((N_REC, LRU_WIDTH), 0.01),
        'rg_w_x': normal((N_REC, LRU_HEADS, LRU_BLOCK, LRU_BLOCK), LRU_BLOCK ** -0.5),
        'rg_b_x': normal((N_REC, LRU_WIDTH), 0.01),
        'rg_lam': jnp.log(s) - jnp.log1p(-s),
    }


def reference(x, norm_mix_g, norm_mlp_g, norm_f_g, mlp_w1, mlp_w2,
              hy_w_in, hy_w_out, rw_mu_rkv, rw_mu_wag, rw_w0, rw_w1, rw_w2, rw_a0, rw_a1, rw_a2,
              rw_g1, rw_g2, rw_k_k, rw_k_a, rw_r_k, rw_ln_w, rw_ln_b,
              s5_lam_re, s5_lam_im, s5_log_dt, s5_b_re, s5_b_im, s5_c_re, s5_c_im, s5_d,
              s5_glu_w, s5_glu_b,
              rg_w_in, rg_w_out, rg_conv_w, rg_conv_b, rg_w_a, rg_b_a, rg_w_x, rg_b_x, rg_lam):
    rw = RWKV_WIDTH
    for layer in range(DEPTH):
        j = layer // 2
        h = rms_norm(x, norm_mix_g[layer])
        if layer % 2 == 0:
            proj = h @ hy_w_in[j]
            p_r, p_k, p_v = proj[..., :rw], proj[..., rw:2 * rw], proj[..., 2 * rw:3 * rw]
            u = proj[..., 3 * rw:]
            y_a = rwkv7_time_mix(h, p_r, p_k, p_v, rw_mu_rkv[j], rw_mu_wag[j], rw_w0[j], rw_w1[j],
                                 rw_w2[j], rw_a0[j], rw_a1[j], rw_a2[j], rw_g1[j], rw_g2[j],
                                 rw_k_k[j], rw_k_a[j], rw_r_k[j], rw_ln_w[j], rw_ln_b[j])
            y_b = s5_ssm(u, s5_lam_re[j], s5_lam_im[j], s5_log_dt[j], s5_b_re[j], s5_b_im[j],
                         s5_c_re[j], s5_c_im[j], s5_d[j], s5_glu_w[j], s5_glu_b[j])
            mix = jnp.concatenate([y_a, y_b], axis=-1) @ hy_w_out[j]
        else:
            proj = h @ rg_w_in[j]
            gate, xr = proj[..., :LRU_WIDTH], proj[..., LRU_WIDTH:]
            xr = causal_depthwise_conv(xr, rg_conv_w[j], rg_conv_b[j])
            y = rg_lru(xr, rg_w_a[j], rg_b_a[j], rg_w_x[j], rg_b_x[j], rg_lam[j])
            mix = (jax.nn.gelu(gate) * y) @ rg_w_out[j]
        x = x + mix.astype(x.dtype)
        h = rms_norm(x, norm_mlp_g[layer])
        x = x + jnp.square(jax.nn.relu(h @ mlp_w1[layer])) @ mlp_w2[layer]
    return rms_norm(x, norm_f_g)
```

```python
import functools
import math

import jax
import jax.numpy as jnp
from jax import lax
from jax.experimental import pallas as pl
from jax.experimental.pallas import tpu as pltpu

F32 = jnp.float32
BF16 = jnp.bfloat16

NORM_EPS = 1e-6
RWKV_HEAD = 64
RWKV_GN_EPS = 64e-5
RWKV_CHUNK = 64
HEAD_PAIR = 2 * RWKV_HEAD
S5_GROUP = 16
S5_STATE = 64
S5_CHUNK = 16
LRU_C = 8.0
CONV_WIDTH = 4
SUBLANES = 8
V7X_VMEM_BYTES = 64 * 1024 * 1024


def _compiler_params(semantics, vmem_mib):
    assert vmem_mib * 1024 * 1024 < V7X_VMEM_BYTES
    return pltpu.CompilerParams(dimension_semantics=semantics,
                                vmem_limit_bytes=vmem_mib * 1024 * 1024)


def _rms(x, g):
    return x * lax.rsqrt(jnp.mean(x * x, axis=-1, keepdims=True) + NORM_EPS) * g


def _gelu_tanh(x):
    c = math.sqrt(2.0 / math.pi)
    return 0.5 * x * (1.0 + jnp.tanh(c * (x + 0.044715 * (x * x * x))))


def _sigmoid(x):
    return 1.0 / (1.0 + jnp.exp(-x))


def _softplus(x):
    return jnp.maximum(x, 0.0) + jnp.log1p(jnp.exp(-jnp.abs(x)))


def _dot(a, b):
    return jnp.dot(a, b, preferred_element_type=F32)


def _dot_nt(a, b):
    return lax.dot_general(a, b, (((1,), (1,)), ((), ())), preferred_element_type=F32)


def _dot_tn(a, b):
    return lax.dot_general(a, b, (((0,), (0,)), ((), ())), preferred_element_type=F32)


def _shift_rows(x, prev_row):
    row = lax.broadcasted_iota(jnp.int32, x.shape, 0)
    return jnp.where(row == 0, prev_row, pltpu.roll(x, 1, axis=0))


def _norm_proj_kernel(x_ref, g_ref, w_ref, o_ref, h_ref):
    @pl.when(pl.program_id(1) == 0)
    def _():
        h_ref[...] = _rms(x_ref[...], g_ref[...]).astype(BF16)

    o_ref[...] = _dot(h_ref[...], w_ref[...])


def _norm_proj(x, g, w, *, tm, tn):
    m, d = x.shape
    n = w.shape[1]
    return pl.pallas_call(
        _norm_proj_kernel,
        grid=(m // tm, n // tn),
        in_specs=[pl.BlockSpec((tm, d), lambda i, j: (i, 0)),
                  pl.BlockSpec((1, d), lambda i, j: (0, 0)),
                  pl.BlockSpec((d, tn), lambda i, j: (0, j))],
        out_specs=pl.BlockSpec((tm, tn), lambda i, j: (i, j)),
        out_shape=jax.ShapeDtypeStruct((m, n), F32),
        scratch_shapes=[pltpu.VMEM((tm, d), BF16)],
        compiler_params=_compiler_params(("parallel", "arbitrary"), 40),
        name="norm_proj",
    )(x, g, w)


def _lora_kernel(x_ref, xp_ref, g_ref, mu_ref, w1_ref, a1_ref, g1_ref, w2_ref, a2_ref, g2_ref,
                 w0_ref, a0_ref, lw_ref, a_ref, gate_ref, *, blocks_per_seq):
    g = g_ref[...]
    h = _rms(x_ref[...], g)
    hp = _rms(xp_ref[...], g)
    first = (pl.program_id(0) % blocks_per_seq) == 0
    prev_row = jnp.where(first, 0.0, hp[SUBLANES - 1:SUBLANES, :])
    dh = _shift_rows(h, prev_row) - h
    xw = (h + dh * mu_ref[0:1, :]).astype(BF16)
    xa = (h + dh * mu_ref[1:2, :]).astype(BF16)
    xg = (h + dh * mu_ref[2:3, :]).astype(BF16)
    tw = jnp.tanh(_dot(xw, w1_ref[...])).astype(BF16)
    w_pre = w0_ref[...] + _dot(tw, w2_ref[...])
    w_log = -_softplus(-w_pre) - 0.5
    lw_ref[...] = -jnp.exp(w_log)
    ta = _dot(xa, a1_ref[...]).astype(BF16)
    a_ref[...] = _sigmoid(a0_ref[...] + _dot(ta, a2_ref[...]))
    tg = _sigmoid(_dot(xg, g1_ref[...])).astype(BF16)
    gate_ref[...] = _dot(tg, g2_ref[...])


def _rwkv_lora(x, g, mu_wag, w1, a1, g1, w2, a2, g2, w0, a0, *, seq, tm):
    m, d = x.shape
    width = w2.shape[1]
    blocks_per_seq = seq // tm
    full = lambda arr: pl.BlockSpec(arr.shape, lambda i: (0, 0))
    out = jax.ShapeDtypeStruct((m, width), F32)
    return pl.pallas_call(
        functools.partial(_lora_kernel, blocks_per_seq=blocks_per_seq),
        grid=(m // tm,),
        in_specs=[pl.BlockSpec((tm, d), lambda i: (i, 0)),
                  pl.BlockSpec((SUBLANES, d), lambda i: (jnp.maximum(i * (tm // SUBLANES) - 1, 0), 0)),
                  full(g), full(mu_wag), full(w1), full(a1), full(g1), full(w2), full(a2), full(g2),
                  full(w0), full(a0)],
        out_specs=[pl.BlockSpec((tm, width), lambda i: (i, 0))] * 3,
        out_shape=[out, out, out],
        compiler_params=_compiler_params(("parallel",), 40),
        name="rwkv_lora",
    )(x, x, g, mu_wag, w1, a1, g1, w2, a2, g2, w0, a0)


def _split3(x):
    hi = x.astype(BF16)
    r1 = x - hi.astype(F32)
    mid = r1.astype(BF16)
    lo = (r1 - mid.astype(F32)).astype(BF16)
    return hi, mid, lo


def _split2(x):
    hi = x.astype(BF16)
    lo = (x - hi.astype(F32)).astype(BF16)
    return hi, lo


def _rwkv_kernel(pr_ref, pk_ref, pv_ref, lw_ref, a_ref, gate_ref, mu_ref, kk_ref, ka_ref, rk_ref,
                 lnw_ref, lnb_ref, o_ref, prev_ref, state_ref):
    t = RWKV_CHUNK
    n_pairs = o_ref.shape[1] // HEAD_PAIR

    @pl.when(pl.program_id(1) == 0)
    def _():
        prev_ref[...] = jnp.zeros_like(prev_ref)
        state_ref[...] = jnp.zeros_like(state_ref)

    def shift_mix(p_ref, idx):
        p = p_ref[...]
        ps = _shift_rows(p, prev_ref[idx, SUBLANES - 1:SUBLANES, :])
        prev_ref[idx] = p[t - SUBLANES:t, :]
        return p + (ps - p) * mu_ref[idx:idx + 1, :]

    r_all = shift_mix(pr_ref, 0)
    k_all = shift_mix(pk_ref, 1)
    v_all = shift_mix(pv_ref, 2)
    a_all = a_ref[...]
    lw_all = lw_ref[...]

    ri = lax.broadcasted_iota(jnp.int32, (t, t), 0)
    ci = lax.broadcasted_iota(jnp.int32, (t, t), 1)
    tril = jnp.where(ci <= ri, 1.0, 0.0).astype(BF16)
    hi, mid, lo = _split3(lw_all)
    cum_all = _dot(tril, hi) + _dot(tril, mid) + _dot(tril, lo)

    si = lax.broadcasted_iota(jnp.int32, (HEAD_PAIR, HEAD_PAIR), 0) // RWKV_HEAD
    sj = lax.broadcasted_iota(jnp.int32, (HEAD_PAIR, HEAD_PAIR), 1) // RWKV_HEAD
    seg = jnp.where(si == sj, 1.0, 0.0).astype(BF16)

    def head_sum(x):
        xh, xl = _split2(x)
        return _dot(xh, seg) + _dot(xl, seg)

    lane = lax.broadcasted_iota(jnp.int32, (t, HEAD_PAIR), 1)
    head0 = lane < RWKV_HEAD

    def expand(x):
        xb = x.astype(BF16)
        zero = jnp.zeros_like(xb)
        return jnp.concatenate([jnp.where(head0, xb, zero), jnp.where(head0, zero, xb)], axis=0)

    bi = lax.broadcasted_iota(jnp.int32, (2 * t, 2 * t), 0) % t
    bj = lax.broadcasted_iota(jnp.int32, (2 * t, 2 * t), 1) % t
    strict = bj < bi
    incl = bj <= bi
    eye = jnp.where(lax.broadcasted_iota(jnp.int32, (2 * t, 2 * t), 0)
                    == lax.broadcasted_iota(jnp.int32, (2 * t, 2 * t), 1), 1.0, 0.0)

    for p in range(n_pairs):
        sl = slice(p * HEAD_PAIR, (p + 1) * HEAD_PAIR)
        r, k, v, a = r_all[:, sl], k_all[:, sl], v_all[:, sl], a_all[:, sl]
        lw, cum = lw_all[:, sl], cum_all[:, sl]
        kk = k * kk_ref[:, sl]
        kk = kk / jnp.maximum(jnp.sqrt(head_sum(kk * kk)), 1e-12)
        k = k * (1.0 + (a - 1.0) * ka_ref[:, sl])
        b = kk * a
        cum_t = cum[t - 1:t, :]
        p_inv = jnp.exp(-cum)
        d_rest = jnp.exp(cum_t - cum)
        a_e = expand(kk * jnp.exp(cum - lw))
        r_e = expand(r * jnp.exp(cum))
        k_e = expand(k * p_inv)
        b_e = expand(b * p_inv)
        kt_e = expand(k * d_rest)
        bt_e = expand(-(b * d_rest))
        v_e = expand(v)

        ar = jnp.concatenate([a_e, r_e], axis=0)
        scores = _dot_nt(ar, jnp.concatenate([k_e, b_e], axis=0))
        a_ak = jnp.where(strict, scores[:2 * t, :2 * t], 0.0)
        n_ab = jnp.where(strict, scores[:2 * t, 2 * t:], 0.0)
        a_rk = jnp.where(incl, scores[2 * t:, :2 * t], 0.0)
        a_rb = jnp.where(incl, scores[2 * t:, 2 * t:], 0.0)

        state = state_ref[p]
        ar_h = _dot_nt(ar, state.astype(BF16))

        inv = eye - n_ab
        n_pow = n_ab.astype(BF16)
        for _ in range(int(math.log2(t)) - 1):
            n_sq = _dot(n_pow, n_pow)
            n_pow = n_sq.astype(BF16)
            inv = inv + _dot(inv.astype(BF16), n_pow)

        rhs = ar_h[:2 * t] + _dot(a_ak.astype(BF16), v_e)
        u_e = _dot(inv.astype(BF16), rhs.astype(BF16))
        vu = jnp.concatenate([v_e, u_e.astype(BF16)], axis=0)
        o_e = ar_h[2 * t:] + _dot(jnp.concatenate([a_rk, -a_rb], axis=1).astype(BF16), vu)
        o = o_e[:t] + o_e[t:]

        state_ref[p] = state * jnp.exp(cum_t) + _dot_tn(vu, jnp.concatenate([kt_e, bt_e], axis=0))

        inv_n = 1.0 / RWKV_HEAD
        mean = head_sum(o) * inv_n
        d = o - mean
        var = head_sum(d * d) * inv_n
        y = d * lax.rsqrt(var + RWKV_GN_EPS) * lnw_ref[:, sl] + lnb_ref[:, sl]
        y = y + head_sum(r * k * rk_ref[:, sl]) * v
        o_ref[:, sl] = y * gate_ref[:, sl]


def _rwkv_core(proj, lw, a, gate, mu_rkv, k_k, k_a, r_k, ln_w, ln_b, *, batch, seq):
    width = lw.shape[1]
    t = RWKV_CHUNK
    nc = seq // t
    row = lambda col: pl.BlockSpec((t, width), lambda b, c: (b * nc + c, col))
    par = lambda arr: pl.BlockSpec(arr.shape, lambda b, c: (0, 0))
    return pl.pallas_call(
        _rwkv_kernel,
        grid=(batch, nc),
        in_specs=[row(0), row(1), row(2), row(0), row(0), row(0),
                  par(mu_rkv), par(k_k), par(k_a), par(r_k), par(ln_w), par(ln_b)],
        out_specs=row(0),
        out_shape=jax.ShapeDtypeStruct((batch * seq, width), F32),
        scratch_shapes=[pltpu.VMEM((3, SUBLANES, width), F32),
                        pltpu.VMEM((width // HEAD_PAIR, HEAD_PAIR, HEAD_PAIR), F32)],
        compiler_params=_compiler_params(("parallel", "arbitrary"), 32),
        name="rwkv_core",
    )(proj, proj, proj, lw, a, gate, mu_rkv, k_k, k_a, r_k, ln_w, ln_b)


def _s5_kernel(u_ref, toep_ref, bre_ref, bim_ref, cre_ref, cim_ref, lre_ref, lim_ref, y_ref,
               zre_ref, zim_ref, *, batch):
    u = u_ref[0]
    zre_ref[...] = _dot(u, bre_ref[0])
    zim_ref[...] = _dot(u, bim_ref[0])
    lre = lre_ref[0]
    lim = lim_ref[0]
    n_chunks = u.shape[0] // batch

    def step(c, carry):
        s_re, s_im = carry
        rows = pl.ds(pl.multiple_of(c * batch, batch), batch)
        z_re = zre_ref[rows, :]
        z_im = zim_ref[rows, :]
        zre_ref[rows, :] = s_re
        zim_ref[rows, :] = s_im
        return (lre * s_re - lim * s_im + z_re, lre * s_im + lim * s_re + z_im)

    zero = jnp.zeros((batch, S5_STATE), F32)
    lax.fori_loop(0, n_chunks, step, (zero, zero))
    y_ref[0] = (_dot(u, toep_ref[0]) + _dot(zre_ref[...].astype(BF16), cre_ref[0])
                + _dot(zim_ref[...].astype(BF16), cim_ref[0]))


def _s5_core(u_g, toep, bre, bim, cre, cim, lre, lim, *, batch):
    groups, rows, lanes = u_g.shape
    blk = lambda arr: pl.BlockSpec((1,) + arr.shape[1:], lambda g: (g, 0, 0))
    return pl.pallas_call(
        functools.partial(_s5_kernel, batch=batch),
        grid=(groups,),
        in_specs=[blk(u_g), blk(toep), blk(bre), blk(bim), blk(cre), blk(cim), blk(lre), blk(lim)],
        out_specs=pl.BlockSpec((1, rows, lanes), lambda g: (g, 0, 0)),
        out_shape=jax.ShapeDtypeStruct((groups, rows, lanes), F32),
        scratch_shapes=[pltpu.VMEM((rows, S5_STATE), F32), pltpu.VMEM((rows, S5_STATE), F32)],
        compiler_params=_compiler_params(("parallel",), 32),
        name="s5_core",
    )(u_g, toep, bre, bim, cre, cim, lre, lim)


def _s5_matrices(lam_re, lam_im, log_dt, b_re, b_im, c_re, c_im):
    t = S5_CHUNK
    dt = jnp.exp(log_dt)[:, None]
    mag = jnp.exp(lam_re * dt)
    lb_re, lb_im = mag * jnp.cos(lam_im * dt), mag * jnp.sin(lam_im * dt)
    den = lam_re * lam_re + lam_im * lam_im
    z_re = ((lb_re - 1.0) * lam_re + lb_im * lam_im) / den
    z_im = (lb_im * lam_re - (lb_re - 1.0) * lam_im) / den
    bb_re = z_re[..., None] * b_re - z_im[..., None] * b_im
    bb_im = z_re[..., None] * b_im + z_im[..., None] * b_re

    def mul(carry, _):
        pr, pi = carry
        return (pr * lb_re - pi * lb_im, pr * lb_im + pi * lb_re), (pr, pi)

    _, (pw_re, pw_im) = lax.scan(mul, (jnp.ones_like(lb_re), jnp.zeros_like(lb_re)), None, length=t + 1)

    cl_re = pw_re[:, :, None, :] * c_re[None] - pw_im[:, :, None, :] * c_im[None]
    cl_im = pw_re[:, :, None, :] * c_im[None] + pw_im[:, :, None, :] * c_re[None]
    taps = (jnp.einsum('tghp,gpi->tghi', cl_re[:t], bb_re)
            - jnp.einsum('tghp,gpi->tghi', cl_im[:t], bb_im))
    jj = jnp.arange(t)[:, None]
    ss = jnp.arange(t)[None, :]
    lag = ss - jj
    k_js = jnp.where((lag >= 0)[..., None, None, None], taps[jnp.clip(lag, 0, t - 1)], 0.0)
    groups = lam_re.shape[0]
    toep = jnp.transpose(k_js, (2, 0, 4, 1, 3)).reshape(groups, t * S5_GROUP, t * S5_GROUP)
    rev_re, rev_im = pw_re[t - 1::-1][:t], pw_im[t - 1::-1][:t]
    bm_re = rev_re[..., None] * bb_re[None] - rev_im[..., None] * bb_im[None]
    bm_im = rev_re[..., None] * bb_im[None] + rev_im[..., None] * bb_re[None]
    bre = jnp.transpose(bm_re, (1, 0, 3, 2)).reshape(groups, t * S5_GROUP, S5_STATE)
    bim = jnp.transpose(bm_im, (1, 0, 3, 2)).reshape(groups, t * S5_GROUP, S5_STATE)
    cre = jnp.transpose(cl_re[1:], (1, 3, 0, 2)).reshape(groups, S5_STATE, t * S5_GROUP)
    cim = -jnp.transpose(cl_im[1:], (1, 3, 0, 2)).reshape(groups, S5_STATE, t * S5_GROUP)
    lre, lim = pw_re[t][:, None, :], pw_im[t][:, None, :]
    return (toep.astype(BF16), bre.astype(BF16), bim.astype(BF16), cre.astype(BF16),
            cim.astype(BF16), lre, lim)


def _mix_out_kernel(x_ref, ya_ref, ys_ref, u_ref, d_ref, gw_ref, gb_ref, wa_ref, wb_ref, o_ref):
    yb = _gelu_tanh(ys_ref[...] + d_ref[...] * u_ref[...])
    yb = yb * _sigmoid(_dot(yb.astype(BF16), gw_ref[...]) + gb_ref[...])
    o_ref[...] = (x_ref[...] + _dot(ya_ref[...].astype(BF16), wa_ref[...])
                  + _dot(yb.astype(BF16), wb_ref[...]))


def _mix_out(x, ya, ys, proj, d_skip, glu_w, glu_b, w_out_a, w_out_b, *, tm):
    m, d = x.shape
    width = ya.shape[1]
    u_col = proj.shape[1] // width - 1
    full = lambda arr: pl.BlockSpec(arr.shape, lambda i: (0, 0))
    row = lambda w, col=0: pl.BlockSpec((tm, w), lambda i: (i, col))
    return pl.pallas_call(
        _mix_out_kernel,
        grid=(m // tm,),
        in_specs=[row(d), row(width), row(width), row(width, u_col), full(d_skip), full(glu_w),
                  full(glu_b), full(w_out_a), full(w_out_b)],
        out_specs=row(d),
        out_shape=jax.ShapeDtypeStruct((m, d), F32),
        compiler_params=_compiler_params(("parallel",), 48),
        name="mix_out",
    )(x, ya, ys, proj, d_skip, glu_w, glu_b, w_out_a, w_out_b)


def _mlp_kernel(x_ref, g_ref, w1_ref, w2_ref, gf_ref, o_ref, h_ref, *, final_norm):
    f = pl.program_id(1)

    @pl.when(f == 0)
    def _():
        x = x_ref[...]
        h_ref[...] = _rms(x, g_ref[...]).astype(BF16)
        o_ref[...] = x

    hid = jnp.maximum(_dot(h_ref[...], w1_ref[...]), 0.0)
    o_ref[...] += _dot((hid * hid).astype(BF16), w2_ref[...])

    if final_norm:
        @pl.when(f == pl.num_programs(1) - 1)
        def _():
            o_ref[...] = _rms(o_ref[...], gf_ref[...])


def _mlp(x, g, w1, w2, gf, *, final_norm, tm, tf):
    m, d = x.shape
    dff = w1.shape[1]
    return pl.pallas_call(
        functools.partial(_mlp_kernel, final_norm=final_norm),
        grid=(m // tm, dff // tf),
        in_specs=[pl.BlockSpec((tm, d), lambda i, f: (i, 0)),
                  pl.BlockSpec((1, d), lambda i, f: (0, 0)),
                  pl.BlockSpec((d, tf), lambda i, f: (0, f)),
                  pl.BlockSpec((tf, d), lambda i, f: (f, 0)),
                  pl.BlockSpec((1, d), lambda i, f: (0, 0))],
        out_specs=pl.BlockSpec((tm, d), lambda i, f: (i, 0)),
        out_shape=jax.ShapeDtypeStruct((m, d), F32),
        scratch_shapes=[pltpu.VMEM((tm, d), BF16)],
        compiler_params=_compiler_params(("parallel", "arbitrary"), 48),
        name="mlp",
    )(x, g, w1, w2, gf)


def _rglru_kernel(gate_ref, xr_ref, cw_ref, cb_ref, wa_ref, wx_ref, ba_ref, bx_ref, lam_ref, o_ref,
                  tail_ref, h_ref):
    tt, width = xr_ref.shape
    n_heads, blk = wa_ref.shape[0], wa_ref.shape[1]
    c = pl.program_id(1)

    @pl.when(c == 0)
    def _():
        tail_ref[...] = jnp.zeros_like(tail_ref)
        h_ref[...] = jnp.zeros_like(h_ref)

    xr = xr_ref[...]
    tail = tail_ref[...]
    row8 = lax.broadcasted_iota(jnp.int32, (SUBLANES, width), 0)
    xc = xr * cw_ref[CONV_WIDTH - 1:CONV_WIDTH, :] + cb_ref[...]
    for s in range(1, CONV_WIDTH):
        xs = pltpu.roll(xr, s, axis=0)
        head = jnp.where(row8 < s, pltpu.roll(tail, s, axis=0), xs[:SUBLANES])
        xs = jnp.concatenate([head, xs[SUBLANES:]], axis=0)
        xc = xc + xs * cw_ref[CONV_WIDTH - 1 - s:CONV_WIDTH - s, :]
    tail_ref[...] = xr[tt - SUBLANES:tt, :]

    xcb = xc.astype(BF16)
    pre_a = jnp.concatenate(
        [_dot(xcb[:, h * blk:(h + 1) * blk], wa_ref[h]) for h in range(n_heads)], axis=1)
    pre_x = jnp.concatenate(
        [_dot(xcb[:, h * blk:(h + 1) * blk], wx_ref[h]) for h in range(n_heads)], axis=1)
    gate_r = _sigmoid(pre_a + ba_ref[...])
    gate_i = _sigmoid(pre_x + bx_ref[...])
    log_a = LRU_C * gate_r * (-_softplus(-lam_ref[...]))
    a = jnp.exp(log_a)
    mult = jnp.sqrt(1.0 - a * a)
    row = lax.broadcasted_iota(jnp.int32, (tt, width), 0)
    mult = jnp.where((row == 0) & (c == 0), 1.0, mult)
    b = xc * gate_i * mult

    shift = 1
    while shift < tt:
        a_s = pltpu.roll(a, shift, axis=0)
        b_s = pltpu.roll(b, shift, axis=0)
        valid = row >= shift
        b = jnp.where(valid, b + a * b_s, b)
        a = jnp.where(valid, a * a_s, a)
        shift *= 2
    h = b + a * h_ref[SUBLANES - 1:SUBLANES, :]
    h_ref[...] = h[tt - SUBLANES:tt, :]
    o_ref[...] = _gelu_tanh(gate_ref[...]) * h


def _rglru(proj, conv_w, conv_b, w_a, w_x, b_a, b_x, lam, *, batch, seq, tt):
    width = lam.shape[1]
    nc = seq // tt
    row = lambda col: pl.BlockSpec((tt, width), lambda b, c: (b * nc + c, col))
    par = lambda arr: pl.BlockSpec(arr.shape, lambda b, c: (0,) * arr.ndim)
    return pl.pallas_call(
        _rglru_kernel,
        grid=(batch, nc),
        in_specs=[row(0), row(1), par(conv_w), par(conv_b), par(w_a), par(w_x), par(b_a), par(b_x),
                  par(lam)],
        out_specs=row(0),
        out_shape=jax.ShapeDtypeStruct((batch * seq, width), F32),
        scratch_shapes=[pltpu.VMEM((SUBLANES, width), F32), pltpu.VMEM((SUBLANES, width), F32)],
        compiler_params=_compiler_params(("parallel", "arbitrary"), 48),
        name="rglru",
    )(proj, proj, conv_w, conv_b, w_a, w_x, b_a, b_x, lam)


def _proj_res_kernel(x_ref, y_ref, w_ref, o_ref):
    o_ref[...] = x_ref[...] + _dot(y_ref[...].astype(BF16), w_ref[...])


def _proj_res(x, y, w, *, tm):
    m, d = x.shape
    return pl.pallas_call(
        _proj_res_kernel,
        grid=(m // tm,),
        in_specs=[pl.BlockSpec((tm, d), lambda i: (i, 0)),
                  pl.BlockSpec((tm, y.shape[1]), lambda i: (i, 0)),
                  pl.BlockSpec(w.shape, lambda i: (0, 0))],
        out_specs=pl.BlockSpec((tm, d), lambda i: (i, 0)),
        out_shape=jax.ShapeDtypeStruct((m, d), F32),
        compiler_params=_compiler_params(("parallel",), 48),
        name="proj_res",
    )(x, y, w)


def _pad_cols(w, n):
    return jnp.pad(w, ((0, 0), (0, n - w.shape[1])))


def _pad_rows(w, n):
    return jnp.pad(w, ((0, n - w.shape[0]), (0, 0)))


def kernel(x, norm_mix_g, norm_mlp_g, norm_f_g, mlp_w1, mlp_w2, hy_w_in, hy_w_out, rw_mu_rkv, rw_mu_wag, rw_w0, rw_w1, rw_w2, rw_a0, rw_a1, rw_a2, rw_g1, rw_g2, rw_k_k, rw_k_a, rw_r_k, rw_ln_w, rw_ln_b, s5_lam_re, s5_lam_im, s5_log_dt, s5_b_re, s5_b_im, s5_c_re, s5_c_im, s5_d, s5_glu_w, s5_glu_b, rg_w_in, rg_w_out, rg_conv_w, rg_conv_b, rg_w_a, rg_b_a, rg_w_x, rg_b_x, rg_lam):
    batch, seq, d = x.shape
    m = batch * seq
    rw = rw_w2.shape[-1]
    row = lambda v: v.reshape(1, -1)
    bf = lambda w: w.astype(BF16)
    lane = 128
    xf = x.reshape(m, d)

    proj = _norm_proj(xf, row(norm_mix_g[0]), bf(hy_w_in[0]), tm=512, tn=1024)
    lw, a, gate = _rwkv_lora(
        xf, row(norm_mix_g[0]), rw_mu_wag[0],
        bf(_pad_cols(rw_w1[0], lane)), bf(_pad_cols(rw_a1[0], lane)), bf(rw_g1[0]),
        bf(_pad_rows(rw_w2[0], lane)), bf(_pad_rows(rw_a2[0], lane)), bf(rw_g2[0]),
        row(rw_w0[0]), row(rw_a0[0]), seq=seq, tm=256)
    ya = _rwkv_core(proj, lw, a, gate, rw_mu_rkv[0], row(rw_k_k[0]), row(rw_k_a[0]),
                    row(rw_r_k[0]), row(rw_ln_w[0]), row(rw_ln_b[0]), batch=batch, seq=seq)

    groups = s5_lam_re.shape[1]
    n_chunks = seq // S5_CHUNK
    u = proj[:, 3 * rw:].astype(BF16).reshape(batch, n_chunks, S5_CHUNK, groups, S5_GROUP)
    u_g = jnp.transpose(u, (3, 1, 0, 2, 4)).reshape(groups, n_chunks * batch, S5_CHUNK * S5_GROUP)
    mats = _s5_matrices(s5_lam_re[0], s5_lam_im[0], s5_log_dt[0], s5_b_re[0], s5_b_im[0],
                        s5_c_re[0], s5_c_im[0])
    y_g = _s5_core(u_g, *mats, batch=batch)
    ys = jnp.transpose(y_g.reshape(groups, n_chunks, batch, S5_CHUNK, S5_GROUP),
                       (2, 1, 3, 0, 4)).reshape(m, groups * S5_GROUP)

    x1 = _mix_out(xf, ya, ys, proj, row(s5_d[0]), bf(s5_glu_w[0]), row(s5_glu_b[0]),
                  bf(hy_w_out[0][:rw]), bf(hy_w_out[0][rw:]), tm=256)
    x2 = _mlp(x1, row(norm_mlp_g[0]), bf(mlp_w1[0]), bf(mlp_w2[0]), row(norm_f_g),
              final_norm=False, tm=512, tf=1024)

    proj1 = _norm_proj(x2, row(norm_mix_g[1]), bf(rg_w_in[0]), tm=512, tn=1024)
    yr = _rglru(proj1, rg_conv_w[0], row(rg_conv_b[0]), bf(rg_w_a[0]), bf(rg_w_x[0]),
                row(rg_b_a[0]), row(rg_b_x[0]), row(rg_lam[0]), batch=batch, seq=seq, tt=256)
    x3 = _proj_res(x2, yr, bf(rg_w_out[0]), tm=512)
    out = _mlp(x3, row(norm_mlp_g[1]), bf(mlp_w1[1]), bf(mlp_w2[1]), row(norm_f_g),
               final_norm=True, tm=512, tf=1024)
    return out.reshape(batch, seq, d)
```

```python
import functools
import math

import jax
import jax.numpy as jnp
from jax import lax
from jax.experimental import pallas as pl
from jax.experimental.pallas import tpu as pltpu

F32 = jnp.float32
BF16 = jnp.bfloat16

NORM_EPS = 1e-6
RWKV_HEAD = 64
RWKV_GN_EPS = 64e-5
RWKV_CHUNK = 64
HEAD_PAIR = 2 * RWKV_HEAD
S5_GROUP = 16
S5_STATE = 64
S5_CHUNK = 16
LRU_C = 8.0
CONV_WIDTH = 4
SUBLANES = 8
V7X_VMEM_BYTES = 64 * 1024 * 1024


def _compiler_params(semantics, vmem_mib):
    assert vmem_mib * 1024 * 1024 < V7X_VMEM_BYTES
    return pltpu.CompilerParams(dimension_semantics=semantics,
                                vmem_limit_bytes=vmem_mib * 1024 * 1024)


def _rms(x, g):
    return x * lax.rsqrt(jnp.mean(x * x, axis=-1, keepdims=True) + NORM_EPS) * g


def _gelu_tanh(x):
    c = math.sqrt(2.0 / math.pi)
    return 0.5 * x * (1.0 + jnp.tanh(c * (x + 0.044715 * (x * x * x))))


def _sigmoid(x):
    return 1.0 / (1.0 + jnp.exp(-x))


def _softplus(x):
    return jnp.maximum(x, 0.0) + jnp.log1p(jnp.exp(-jnp.abs(x)))


def _dot(a, b):
    return jnp.dot(a, b, preferred_element_type=F32)


def _dot_nt(a, b):
    return lax.dot_general(a, b, (((1,), (1,)), ((), ())), preferred_element_type=F32)


def _dot_tn(a, b):
    return lax.dot_general(a, b, (((0,), (0,)), ((), ())), preferred_element_type=F32)


def _shift_rows(x, prev_row):
    row = lax.broadcasted_iota(jnp.int32, x.shape, 0)
    return jnp.where(row == 0, prev_row, pltpu.roll(x, 1, axis=0))


def _norm_proj_kernel(x_ref, g_ref, w_ref, o_ref, h_ref):
    @pl.when(pl.program_id(1) == 0)
    def _():
        h_ref[...] = _rms(x_ref[...], g_ref[...]).astype(BF16)

    o_ref[...] = _dot(h_ref[...], w_ref[...])


def _norm_proj(x, g, w, *, tm, tn):
    m, d = x.shape
    n = w.shape[1]
    return pl.pallas_call(
        _norm_proj_kernel,
        grid=(m // tm, n // tn),
        in_specs=[pl.BlockSpec((tm, d), lambda i, j: (i, 0)),
                  pl.BlockSpec((1, d), lambda i, j: (0, 0)),
                  pl.BlockSpec((d, tn), lambda i, j: (0, j))],
        out_specs=pl.BlockSpec((tm, tn), lambda i, j: (i, j)),
        out_shape=jax.ShapeDtypeStruct((m, n), F32),
        scratch_shapes=[pltpu.VMEM((tm, d), BF16)],
        compiler_params=_compiler_params(("parallel", "arbitrary"), 40),
        name="norm_proj",
    )(x, g, w)


def _lora_kernel(x_ref, xp_ref, g_ref, mu_ref, w1_ref, a1_ref, g1_ref, w2_ref, a2_ref, g2_ref,
                 w0_ref, a0_ref, lw_ref, a_ref, gate_ref, *, blocks_per_seq):
    g = g_ref[...]
    h = _rms(x_ref[...], g)
    hp = _rms(xp_ref[...], g)
    first = (pl.program_id(0) % blocks_per_seq) == 0
    prev_row = jnp.where(first, 0.0, hp[SUBLANES - 1:SUBLANES, :])
    dh = _shift_rows(h, prev_row) - h
    xw = (h + dh * mu_ref[0:1, :]).astype(BF16)
    xa = (h + dh * mu_ref[1:2, :]).astype(BF16)
    xg = (h + dh * mu_ref[2:3, :]).astype(BF16)
    tw = jnp.tanh(_dot(xw, w1_ref[...])).astype(BF16)
    w_pre = w0_ref[...] + _dot(tw, w2_ref[...])
    w_log = -_softplus(-w_pre) - 0.5
    lw_ref[...] = -jnp.exp(w_log)
    ta = _dot(xa, a1_ref[...]).astype(BF16)
    a_ref[...] = _sigmoid(a0_ref[...] + _dot(ta, a2_ref[...]))
    tg = _sigmoid(_dot(xg, g1_ref[...])).astype(BF16)
    gate_ref[...] = _dot(tg, g2_ref[...])


def _rwkv_lora(x, g, mu_wag, w1, a1, g1, w2, a2, g2, w0, a0, *, seq, tm):
    m, d = x.shape
    width = w2.shape[1]
    blocks_per_seq = seq // tm
    full = lambda arr: pl.BlockSpec(arr.shape, lambda i: (0, 0))
    out = jax.ShapeDtypeStruct((m, width), F32)
    return pl.pallas_call(
        functools.partial(_lora_kernel, blocks_per_seq=blocks_per_seq),
        grid=(m // tm,),
        in_specs=[pl.BlockSpec((tm, d), lambda i: (i, 0)),
                  pl.BlockSpec((SUBLANES, d), lambda i: (jnp.maximum(i * (tm // SUBLANES) - 1, 0), 0)),
                  full(g), full(mu_wag), full(w1), full(a1), full(g1), full(w2), full(a2), full(g2),
                  full(w0), full(a0)],
        out_specs=[pl.BlockSpec((tm, width), lambda i: (i, 0))] * 3,
        out_shape=[out, out, out],
        compiler_params=_compiler_params(("parallel",), 40),
        name="rwkv_lora",
    )(x, x, g, mu_wag, w1, a1, g1, w2, a2, g2, w0, a0)


def _split3(x):
    hi = x.astype(BF16)
    r1 = x - hi.astype(F32)
    mid = r1.astype(BF16)
    lo = (r1 - mid.astype(F32)).astype(BF16)
    return hi, mid, lo


def _split2(x):
    hi = x.astype(BF16)
    lo = (x - hi.astype(F32)).astype(BF16)
    return hi, lo


def _rwkv_kernel(pr_ref, pk_ref, pv_ref, lw_ref, a_ref, gate_ref, mu_ref, kk_ref, ka_ref, rk_ref,
                 lnw_ref, lnb_ref, o_ref, prev_ref, state_ref):
    t = RWKV_CHUNK
    n_pairs = o_ref.shape[1] // HEAD_PAIR

    @pl.when(pl.program_id(1) == 0)
    def _():
        prev_ref[...] = jnp.zeros_like(prev_ref)
        state_ref[...] = jnp.zeros_like(state_ref)

    def shift_mix(p_ref, idx):
        p = p_ref[...]
        ps = _shift_rows(p, prev_ref[idx, SUBLANES - 1:SUBLANES, :])
        prev_ref[idx] = p[t - SUBLANES:t, :]
        return p + (ps - p) * mu_ref[idx:idx + 1, :]

    r_all = shift_mix(pr_ref, 0)
    k_all = shift_mix(pk_ref, 1)
    v_all = shift_mix(pv_ref, 2)
    a_all = a_ref[...]
    lw_all = lw_ref[...]

    ri = lax.broadcasted_iota(jnp.int32, (t, t), 0)
    ci = lax.broadcasted_iota(jnp.int32, (t, t), 1)
    tril = jnp.where(ci <= ri, 1.0, 0.0).astype(BF16)
    hi, mid, lo = _split3(lw_all)
    cum_all = _dot(tril, hi) + _dot(tril, mid) + _dot(tril, lo)

    si = lax.broadcasted_iota(jnp.int32, (HEAD_PAIR, HEAD_PAIR), 0) // RWKV_HEAD
    sj = lax.broadcasted_iota(jnp.int32, (HEAD_PAIR, HEAD_PAIR), 1) // RWKV_HEAD
    seg = jnp.where(si == sj, 1.0, 0.0).astype(BF16)

    width = o_ref.shape[1]
    pairs = range(n_pairs)
    slab = lambda x, p: x[:, p * HEAD_PAIR:(p + 1) * HEAD_PAIR]
    lanes = lambda xs: jnp.concatenate(xs, axis=1)

    def head_sum_all(x):
        xh, xl = _split2(x)
        return lanes([_dot(slab(xh, p), seg) + _dot(slab(xl, p), seg) for p in pairs])

    head0 = (lax.broadcasted_iota(jnp.int32, (t, width), 1) % HEAD_PAIR) < RWKV_HEAD

    def expand(x):
        xb = x.astype(BF16)
        zero = jnp.zeros_like(xb)
        return jnp.concatenate([jnp.where(head0, xb, zero), jnp.where(head0, zero, xb)], axis=0)

    bi = lax.broadcasted_iota(jnp.int32, (2 * t, 2 * t), 0) % t
    bj = lax.broadcasted_iota(jnp.int32, (2 * t, 2 * t), 1) % t
    strict = bj < bi
    incl = bj <= bi
    eye = jnp.where(lax.broadcasted_iota(jnp.int32, (2 * t, 2 * t), 0)
                    == lax.broadcasted_iota(jnp.int32, (2 * t, 2 * t), 1), 1.0, 0.0)

    kk = k_all * kk_ref[...]
    kk = kk / jnp.maximum(jnp.sqrt(head_sum_all(kk * kk)), 1e-12)
    k_all = k_all * (1.0 + (a_all - 1.0) * ka_ref[...])
    b_all = kk * a_all
    cum_t = cum_all[t - 1:t, :]
    p_inv = jnp.exp(-cum_all)
    d_rest = jnp.exp(cum_t - cum_all)
    ar = jnp.concatenate([expand(kk * jnp.exp(cum_all - lw_all)), expand(r_all * jnp.exp(cum_all))], axis=0)
    kb = jnp.concatenate([expand(k_all * p_inv), expand(b_all * p_inv)], axis=0)
    kbt = jnp.concatenate([expand(k_all * d_rest), expand(-(b_all * d_rest))], axis=0)
    v_e = expand(v_all)
    decay_t = jnp.exp(cum_t)

    states = [state_ref[p] for p in pairs]
    scores = [_dot_nt(slab(ar, p), slab(kb, p)) for p in pairs]
    ar_h = [_dot_nt(slab(ar, p), states[p].astype(BF16)) for p in pairs]
    n_ab = [jnp.where(strict, s[:2 * t, 2 * t:], 0.0) for s in scores]
    a_ak = [jnp.where(strict, s[:2 * t, :2 * t], 0.0).astype(BF16) for s in scores]
    a_r = [jnp.concatenate([jnp.where(incl, s[2 * t:, :2 * t], 0.0),
                            jnp.where(incl, -s[2 * t:, 2 * t:], 0.0)], axis=1).astype(BF16)
           for s in scores]
    rhs = [(ar_h[p][:2 * t] + _dot(a_ak[p], slab(v_e, p))).astype(BF16) for p in pairs]

    inv = [eye - n for n in n_ab]
    n_pow = [n.astype(BF16) for n in n_ab]
    n_pow = [_dot(n, n).astype(BF16) for n in n_pow]
    n_levels = int(math.log2(t)) - 1
    for level in range(n_levels):
        nxt = [_dot(n, n).astype(BF16) for n in n_pow] if level + 1 < n_levels else None
        inv = [inv[p] + _dot(inv[p].astype(BF16), n_pow[p]) for p in pairs]
        n_pow = nxt

    u_e = [_dot(inv[p].astype(BF16), rhs[p]).astype(BF16) for p in pairs]
    vu = [jnp.concatenate([slab(v_e, p), u_e[p]], axis=0) for p in pairs]
    o_e = [ar_h[p][2 * t:] + _dot(a_r[p], vu[p]) for p in pairs]
    for p in pairs:
        state_ref[p] = states[p] * slab(decay_t, p) + _dot_tn(vu[p], slab(kbt, p))
    o = lanes([oe[:t] + oe[t:] for oe in o_e])

    inv_n = 1.0 / RWKV_HEAD
    mean = head_sum_all(o) * inv_n
    d = o - mean
    var = head_sum_all(d * d) * inv_n
    y = d * lax.rsqrt(var + RWKV_GN_EPS) * lnw_ref[...] + lnb_ref[...]
    y = y + head_sum_all(r_all * k_all * rk_ref[...]) * v_all
    o_ref[...] = y * gate_ref[...]


def _rwkv_core(proj, lw, a, gate, mu_rkv, k_k, k_a, r_k, ln_w, ln_b, *, batch, seq):
    width = lw.shape[1]
    t = RWKV_CHUNK
    nc = seq // t
    row = lambda col: pl.BlockSpec((t, width), lambda b, c: (b * nc + c, col))
    par = lambda arr: pl.BlockSpec(arr.shape, lambda b, c: (0, 0))
    return pl.pallas_call(
        _rwkv_kernel,
        grid=(batch, nc),
        in_specs=[row(0), row(1), row(2), row(0), row(0), row(0),
                  par(mu_rkv), par(k_k), par(k_a), par(r_k), par(ln_w), par(ln_b)],
        out_specs=row(0),
        out_shape=jax.ShapeDtypeStruct((batch * seq, width), F32),
        scratch_shapes=[pltpu.VMEM((3, SUBLANES, width), F32),
                        pltpu.VMEM((width // HEAD_PAIR, HEAD_PAIR, HEAD_PAIR), F32)],
        compiler_params=_compiler_params(("parallel", "arbitrary"), 32),
        name="rwkv_core",
    )(proj, proj, proj, lw, a, gate, mu_rkv, k_k, k_a, r_k, ln_w, ln_b)


def _s5_kernel(u_ref, toep_ref, bre_ref, bim_ref, cre_ref, cim_ref, lre_ref, lim_ref, y_ref,
               zre_ref, zim_ref, *, batch):
    u = u_ref[0]
    zre_ref[...] = _dot(u, bre_ref[0])
    zim_ref[...] = _dot(u, bim_ref[0])
    lre = lre_ref[0]
    lim = lim_ref[0]
    n_chunks = u.shape[0] // batch

    def step(c, carry):
        s_re, s_im = carry
        rows = pl.ds(pl.multiple_of(c * batch, batch), batch)
        z_re = zre_ref[rows, :]
        z_im = zim_ref[rows, :]
        zre_ref[rows, :] = s_re
        zim_ref[rows, :] = s_im
        return (lre * s_re - lim * s_im + z_re, lre * s_im + lim * s_re + z_im)

    zero = jnp.zeros((batch, S5_STATE), F32)
    lax.fori_loop(0, n_chunks, step, (zero, zero))
    y_ref[0] = (_dot(u, toep_ref[0]) + _dot(zre_ref[...].astype(BF16), cre_ref[0])
                + _dot(zim_ref[...].astype(BF16), cim_ref[0]))


def _s5_core(u_g, toep, bre, bim, cre, cim, lre, lim, *, batch):
    groups, rows, lanes = u_g.shape
    blk = lambda arr: pl.BlockSpec((1,) + arr.shape[1:], lambda g: (g, 0, 0))
    return pl.pallas_call(
        functools.partial(_s5_kernel, batch=batch),
        grid=(groups,),
        in_specs=[blk(u_g), blk(toep), blk(bre), blk(bim), blk(cre), blk(cim), blk(lre), blk(lim)],
        out_specs=pl.BlockSpec((1, rows, lanes), lambda g: (g, 0, 0)),
        out_shape=jax.ShapeDtypeStruct((groups, rows, lanes), F32),
        scratch_shapes=[pltpu.VMEM((rows, S5_STATE), F32), pltpu.VMEM((rows, S5_STATE), F32)],
        compiler_params=_compiler_params(("parallel",), 32),
        name="s5_core",
    )(u_g, toep, bre, bim, cre, cim, lre, lim)


def _s5_matrices(lam_re, lam_im, log_dt, b_re, b_im, c_re, c_im):
    t = S5_CHUNK
    dt = jnp.exp(log_dt)[:, None]
    mag = jnp.exp(lam_re * dt)
    lb_re, lb_im = mag * jnp.cos(lam_im * dt), mag * jnp.sin(lam_im * dt)
    den = lam_re * lam_re + lam_im * lam_im
    z_re = ((lb_re - 1.0) * lam_re + lb_im * lam_im) / den
    z_im = (lb_im * lam_re - (lb_re - 1.0) * lam_im) / den
    bb_re = z_re[..., None] * b_re - z_im[..., None] * b_im
    bb_im = z_re[..., None] * b_im + z_im[..., None] * b_re

    def mul(carry, _):
        pr, pi = carry
        return (pr * lb_re - pi * lb_im, pr * lb_im + pi * lb_re), (pr, pi)

    _, (pw_re, pw_im) = lax.scan(mul, (jnp.ones_like(lb_re), jnp.zeros_like(lb_re)), None, length=t + 1)

    cl_re = pw_re[:, :, None, :] * c_re[None] - pw_im[:, :, None, :] * c_im[None]
    cl_im = pw_re[:, :, None, :] * c_im[None] + pw_im[:, :, None, :] * c_re[None]
    taps = (jnp.einsum('tghp,gpi->tghi', cl_re[:t], bb_re)
            - jnp.einsum('tghp,gpi->tghi', cl_im[:t], bb_im))
    jj = jnp.arange(t)[:, None]
    ss = jnp.arange(t)[None, :]
    lag = ss - jj
    k_js = jnp.where((lag >= 0)[..., None, None, None], taps[jnp.clip(lag, 0, t - 1)], 0.0)
    groups = lam_re.shape[0]
    toep = jnp.transpose(k_js, (2, 0, 4, 1, 3)).reshape(groups, t * S5_GROUP, t * S5_GROUP)
    rev_re, rev_im = pw_re[t - 1::-1][:t], pw_im[t - 1::-1][:t]
    bm_re = rev_re[..., None] * bb_re[None] - rev_im[..., None] * bb_im[None]
    bm_im = rev_re[..., None] * bb_im[None] + rev_im[..., None] * bb_re[None]
    bre = jnp.transpose(bm_re, (1, 0, 3, 2)).reshape(groups, t * S5_GROUP, S5_STATE)
    bim = jnp.transpose(bm_im, (1, 0, 3, 2)).reshape(groups, t * S5_GROUP, S5_STATE)
    cre = jnp.transpose(cl_re[1:], (1, 3, 0, 2)).reshape(groups, S5_STATE, t * S5_GROUP)
    cim = -jnp.transpose(cl_im[1:], (1, 3, 0, 2)).reshape(groups, S5_STATE, t * S5_GROUP)
    lre, lim = pw_re[t][:, None, :], pw_im[t][:, None, :]
    return (toep.astype(BF16), bre.astype(BF16), bim.astype(BF16), cre.astype(BF16),
            cim.astype(BF16), lre, lim)


def _mix_out_kernel(x_ref, ya_ref, ys_ref, u_ref, d_ref, gw_ref, gb_ref, wa_ref, wb_ref, o_ref):
    yb = _gelu_tanh(ys_ref[...] + d_ref[...] * u_ref[...])
    yb = yb * _sigmoid(_dot(yb.astype(BF16), gw_ref[...]) + gb_ref[...])
    o_ref[...] = (x_ref[...] + _dot(ya_ref[...].astype(BF16), wa_ref[...])
                  + _dot(yb.astype(BF16), wb_ref[...]))


def _mix_out(x, ya, ys, proj, d_skip, glu_w, glu_b, w_out_a, w_out_b, *, tm):
    m, d = x.shape
    width = ya.shape[1]
    u_col = proj.shape[1] // width - 1
    full = lambda arr: pl.BlockSpec(arr.shape, lambda i: (0, 0))
    row = lambda w, col=0: pl.BlockSpec((tm, w), lambda i: (i, col))
    return pl.pallas_call(
        _mix_out_kernel,
        grid=(m // tm,),
        in_specs=[row(d), row(width), row(width), row(width, u_col), full(d_skip), full(glu_w),
                  full(glu_b), full(w_out_a), full(w_out_b)],
        out_specs=row(d),
        out_shape=jax.ShapeDtypeStruct((m, d), F32),
        compiler_params=_compiler_params(("parallel",), 48),
        name="mix_out",
    )(x, ya, ys, proj, d_skip, glu_w, glu_b, w_out_a, w_out_b)


def _mlp_kernel(x_ref, g_ref, w1_ref, w2_ref, gf_ref, o_ref, h_ref, *, final_norm):
    f = pl.program_id(1)

    @pl.when(f == 0)
    def _():
        x = x_ref[...]
        h_ref[...] = _rms(x, g_ref[...]).astype(BF16)
        o_ref[...] = x

    hid = jnp.maximum(_dot(h_ref[...], w1_ref[...]), 0.0)
    o_ref[...] += _dot((hid * hid).astype(BF16), w2_ref[...])

    if final_norm:
        @pl.when(f == pl.num_programs(1) - 1)
        def _():
            o_ref[...] = _rms(o_ref[...], gf_ref[...])


def _mlp(x, g, w1, w2, gf, *, final_norm, tm, tf):
    m, d = x.shape
    dff = w1.shape[1]
    return pl.pallas_call(
        functools.partial(_mlp_kernel, final_norm=final_norm),
        grid=(m // tm, dff // tf),
        in_specs=[pl.BlockSpec((tm, d), lambda i, f: (i, 0)),
                  pl.BlockSpec((1, d), lambda i, f: (0, 0)),
                  pl.BlockSpec((d, tf), lambda i, f: (0, f)),
                  pl.BlockSpec((tf, d), lambda i, f: (f, 0)),
                  pl.BlockSpec((1, d), lambda i, f: (0, 0))],
        out_specs=pl.BlockSpec((tm, d), lambda i, f: (i, 0)),
        out_shape=jax.ShapeDtypeStruct((m, d), F32),
        scratch_shapes=[pltpu.VMEM((tm, d), BF16)],
        compiler_params=_compiler_params(("parallel", "arbitrary"), 48),
        name="mlp",
    )(x, g, w1, w2, gf)


def _rglru_kernel(gate_ref, xr_ref, cw_ref, cb_ref, wa_ref, wx_ref, ba_ref, bx_ref, lam_ref, o_ref,
                  tail_ref, h_ref):
    tt, width = xr_ref.shape
    n_heads, blk = wa_ref.shape[0], wa_ref.shape[1]
    c = pl.program_id(1)

    @pl.when(c == 0)
    def _():
        tail_ref[...] = jnp.zeros_like(tail_ref)
        h_ref[...] = jnp.zeros_like(h_ref)

    xr = xr_ref[...]
    tail = tail_ref[...]
    row8 = lax.broadcasted_iota(jnp.int32, (SUBLANES, width), 0)
    xc = xr * cw_ref[CONV_WIDTH - 1:CONV_WIDTH, :] + cb_ref[...]
    for s in range(1, CONV_WIDTH):
        xs = pltpu.roll(xr, s, axis=0)
        head = jnp.where(row8 < s, pltpu.roll(tail, s, axis=0), xs[:SUBLANES])
        xs = jnp.concatenate([head, xs[SUBLANES:]], axis=0)
        xc = xc + xs * cw_ref[CONV_WIDTH - 1 - s:CONV_WIDTH - s, :]
    tail_ref[...] = xr[tt - SUBLANES:tt, :]

    xcb = xc.astype(BF16)
    pre_a = jnp.concatenate(
        [_dot(xcb[:, h * blk:(h + 1) * blk], wa_ref[h]) for h in range(n_heads)], axis=1)
    pre_x = jnp.concatenate(
        [_dot(xcb[:, h * blk:(h + 1) * blk], wx_ref[h]) for h in range(n_heads)], axis=1)
    gate_r = _sigmoid(pre_a + ba_ref[...])
    gate_i = _sigmoid(pre_x + bx_ref[...])
    log_a = LRU_C * gate_r * (-_softplus(-lam_ref[...]))
    a = jnp.exp(log_a)
    mult = jnp.sqrt(1.0 - a * a)
    row = lax.broadcasted_iota(jnp.int32, (tt, width), 0)
    mult = jnp.where((row == 0) & (c == 0), 1.0, mult)
    b = xc * gate_i * mult

    shift = 1
    while shift < tt:
        a_s = pltpu.roll(a, shift, axis=0)
        b_s = pltpu.roll(b, shift, axis=0)
        valid = row >= shift
        b = jnp.where(valid, b + a * b_s, b)
        a = jnp.where(valid, a * a_s, a)
        shift *= 2
    h = b + a * h_ref[SUBLANES - 1:SUBLANES, :]
    h_ref[...] = h[tt - SUBLANES:tt, :]
    o_ref[...] = _gelu_tanh(gate_ref[...]) * h


def _rglru(proj, conv_w, conv_b, w_a, w_x, b_a, b_x, lam, *, batch, seq, tt):
    width = lam.shape[1]
    nc = seq // tt
    row = lambda col: pl.BlockSpec((tt, width), lambda b, c: (b * nc + c, col))
    par = lambda arr: pl.BlockSpec(arr.shape, lambda b, c: (0,) * arr.ndim)
    return pl.pallas_call(
        _rglru_kernel,
        grid=(batch, nc),
        in_specs=[row(0), row(1), par(conv_w), par(conv_b), par(w_a), par(w_x), par(b_a), par(b_x),
                  par(lam)],
        out_specs=row(0),
        out_shape=jax.ShapeDtypeStruct((batch * seq, width), F32),
        scratch_shapes=[pltpu.VMEM((SUBLANES, width), F32), pltpu.VMEM((SUBLANES, width), F32)],
        compiler_params=_compiler_params(("parallel", "arbitrary"), 48),
        name="rglru",
    )(proj, proj, conv_w, conv_b, w_a, w_x, b_a, b_x, lam)


def _proj_res_kernel(x_ref, y_ref, w_ref, o_ref):
    o_ref[...] = x_ref[...] + _dot(y_ref[...].astype(BF16), w_ref[...])


def _proj_res(x, y, w, *, tm):
    m, d = x.shape
    return pl.pallas_call(
        _proj_res_kernel,
        grid=(m // tm,),
        in_specs=[pl.BlockSpec((tm, d), lambda i: (i, 0)),
                  pl.BlockSpec((tm, y.shape[1]), lambda i: (i, 0)),
                  pl.BlockSpec(w.shape, lambda i: (0, 0))],
        out_specs=pl.BlockSpec((tm, d), lambda i: (i, 0)),
        out_shape=jax.ShapeDtypeStruct((m, d), F32),
        compiler_params=_compiler_params(("parallel",), 48),
        name="proj_res",
    )(x, y, w)


def _pad_cols(w, n):
    return jnp.pad(w, ((0, 0), (0, n - w.shape[1])))


def _pad_rows(w, n):
    return jnp.pad(w, ((0, n - w.shape[0]), (0, 0)))


def kernel(x, norm_mix_g, norm_mlp_g, norm_f_g, mlp_w1, mlp_w2, hy_w_in, hy_w_out, rw_mu_rkv, rw_mu_wag, rw_w0, rw_w1, rw_w2, rw_a0, rw_a1, rw_a2, rw_g1, rw_g2, rw_k_k, rw_k_a, rw_r_k, rw_ln_w, rw_ln_b, s5_lam_re, s5_lam_im, s5_log_dt, s5_b_re, s5_b_im, s5_c_re, s5_c_im, s5_d, s5_glu_w, s5_glu_b, rg_w_in, rg_w_out, rg_conv_w, rg_conv_b, rg_w_a, rg_b_a, rg_w_x, rg_b_x, rg_lam):
    batch, seq, d = x.shape
    m = batch * seq
    rw = rw_w2.shape[-1]
    row = lambda v: v.reshape(1, -1)
    bf = lambda w: w.astype(BF16)
    lane = 128
    xf = x.reshape(m, d)

    proj = _norm_proj(xf, row(norm_mix_g[0]), bf(hy_w_in[0]), tm=512, tn=1024)
    lw, a, gate = _rwkv_lora(
        xf, row(norm_mix_g[0]), rw_mu_wag[0],
        bf(_pad_cols(rw_w1[0], lane)), bf(_pad_cols(rw_a1[0], lane)), bf(rw_g1[0]),
        bf(_pad_rows(rw_w2[0], lane)), bf(_pad_rows(rw_a2[0], lane)), bf(rw_g2[0]),
        row(rw_w0[0]), row(rw_a0[0]), seq=seq, tm=256)
    ya = _rwkv_core(proj, lw, a, gate, rw_mu_rkv[0], row(rw_k_k[0]), row(rw_k_a[0]),
                    row(rw_r_k[0]), row(rw_ln_w[0]), row(rw_ln_b[0]), batch=batch, seq=seq)

    groups = s5_lam_re.shape[1]
    n_chunks = seq // S5_CHUNK
    u = proj[:, 3 * rw:].astype(BF16).reshape(batch, n_chunks, S5_CHUNK, groups, S5_GROUP)
    u_g = jnp.transpose(u, (3, 1, 0, 2, 4)).reshape(groups, n_chunks * batch, S5_CHUNK * S5_GROUP)
    mats = _s5_matrices(s5_lam_re[0], s5_lam_im[0], s5_log_dt[0], s5_b_re[0], s5_b_im[0],
                        s5_c_re[0], s5_c_im[0])
    y_g = _s5_core(u_g, *mats, batch=batch)
    ys = jnp.transpose(y_g.reshape(groups, n_chunks, batch, S5_CHUNK, S5_GROUP),
                       (2, 1, 3, 0, 4)).reshape(m, groups * S5_GROUP)

    x1 = _mix_out(xf, ya, ys, proj, row(s5_d[0]), bf(s5_glu_w[0]), row(s5_glu_b[0]),
                  bf(hy_w_out[0][:rw]), bf(hy_w_out[0][rw:]), tm=256)
    x2 = _mlp(x1, row(norm_mlp_g[0]), bf(mlp_w1[0]), bf(mlp_w2[0]), row(norm_f_g),
              final_norm=False, tm=512, tf=1024)

    proj1 = _norm_proj(x2, row(norm_mix_g[1]), bf(rg_w_in[0]), tm=512, tn=1024)
    yr = _rglru(proj1, rg_conv_w[0], row(rg_conv_b[0]), bf(rg_w_a[0]), bf(rg_w_x[0]),
                row(rg_b_a[0]), row(rg_b_x[0]), row(rg_lam[0]), batch=batch, seq=seq, tt=256)
    x3 = _proj_res(x2, yr, bf(rg_w_out[0]), tm=512)
    out = _mlp(x3, row(norm_mlp_g[1]), bf(mlp_w1[1]), bf(mlp_w2[1]), row(norm_f_g),
               final_norm=True, tm=512, tf=1024)
    return out.reshape(batch, seq, d)
```

```python
import functools
import math

import jax
import jax.numpy as jnp
from jax import lax
from jax.experimental import pallas as pl
from jax.experimental.pallas import tpu as pltpu

F32 = jnp.float32
BF16 = jnp.bfloat16

NORM_EPS = 1e-6
RWKV_HEAD = 64
RWKV_GN_EPS = 64e-5
RWKV_CHUNK = 64
HEAD_PAIR = 2 * RWKV_HEAD
S5_GROUP = 16
S5_STATE = 64
S5_CHUNK = 16
S5_SLAB = 256
LRU_C = 8.0
CONV_WIDTH = 4
SUBLANES = 8
LANES = 128
V7X_VMEM_BYTES = 64 * 1024 * 1024


def _compiler_params(semantics, vmem_mib):
    assert vmem_mib * 1024 * 1024 < V7X_VMEM_BYTES
    return pltpu.CompilerParams(dimension_semantics=semantics,
                                vmem_limit_bytes=vmem_mib * 1024 * 1024)


def _rms(x, g):
    return x * lax.rsqrt(jnp.mean(x * x, axis=-1, keepdims=True) + NORM_EPS) * g


def _gelu_tanh(x):
    c = math.sqrt(2.0 / math.pi)
    return 0.5 * x * (1.0 + jnp.tanh(c * (x + 0.044715 * (x * x * x))))


def _sigmoid(x):
    return 1.0 / (1.0 + jnp.exp(-x))


def _softplus(x):
    return jnp.maximum(x, 0.0) + jnp.log1p(jnp.exp(-jnp.abs(x)))


def _dot(a, b):
    return jnp.dot(a, b, preferred_element_type=F32)


def _dot_nt(a, b):
    return lax.dot_general(a, b, (((1,), (1,)), ((), ())), preferred_element_type=F32)


def _dot_tn(a, b):
    return lax.dot_general(a, b, (((0,), (0,)), ((), ())), preferred_element_type=F32)


def _shift_rows(x, prev_row):
    row = lax.broadcasted_iota(jnp.int32, x.shape, 0)
    return jnp.where(row == 0, prev_row, pltpu.roll(x, 1, axis=0))


def _matmul_kernel(h_ref, w_ref, o_ref):
    o_ref[...] = _dot(h_ref[...], w_ref[...])


def _matmul(h, w, *, tm, tn):
    m, d = h.shape
    n = w.shape[1]
    return pl.pallas_call(
        _matmul_kernel,
        grid=(m // tm, n // tn),
        in_specs=[pl.BlockSpec((tm, d), lambda i, j: (i, 0)),
                  pl.BlockSpec((d, tn), lambda i, j: (0, j))],
        out_specs=pl.BlockSpec((tm, tn), lambda i, j: (i, j)),
        out_shape=jax.ShapeDtypeStruct((m, n), F32),
        compiler_params=_compiler_params(("parallel", "arbitrary"), 40),
        name="in_proj",
    )(h, w)


def _lora_kernel(x_ref, xp_ref, g_ref, mu_ref, w1_ref, a1_ref, g1_ref, w2_ref, a2_ref, g2_ref,
                 w0_ref, a0_ref, lw_ref, a_ref, gate_ref, hn_ref, *, blocks_per_seq):
    g = g_ref[...]
    h = _rms(x_ref[...], g)
    hn_ref[...] = h.astype(BF16)
    hp = _rms(xp_ref[...], g)
    first = (pl.program_id(0) % blocks_per_seq) == 0
    prev_row = jnp.where(first, 0.0, hp[SUBLANES - 1:SUBLANES, :])
    dh = _shift_rows(h, prev_row) - h
    xw = (h + dh * mu_ref[0:1, :]).astype(BF16)
    xa = (h + dh * mu_ref[1:2, :]).astype(BF16)
    xg = (h + dh * mu_ref[2:3, :]).astype(BF16)
    tw = jnp.tanh(_dot(xw, w1_ref[...])).astype(BF16)
    w_pre = w0_ref[...] + _dot(tw, w2_ref[...])
    w_log = -_softplus(-w_pre) - 0.5
    lw_ref[...] = -jnp.exp(w_log)
    ta = _dot(xa, a1_ref[...]).astype(BF16)
    a_ref[...] = _sigmoid(a0_ref[...] + _dot(ta, a2_ref[...]))
    tg = _sigmoid(_dot(xg, g1_ref[...])).astype(BF16)
    gate_ref[...] = _dot(tg, g2_ref[...])


def _rwkv_lora(x, g, mu_wag, w1, a1, g1, w2, a2, g2, w0, a0, *, seq, tm):
    m, d = x.shape
    width = w2.shape[1]
    blocks_per_seq = seq // tm
    full = lambda arr: pl.BlockSpec(arr.shape, lambda i: (0, 0))
    out = jax.ShapeDtypeStruct((m, width), F32)
    return pl.pallas_call(
        functools.partial(_lora_kernel, blocks_per_seq=blocks_per_seq),
        grid=(m // tm,),
        in_specs=[pl.BlockSpec((tm, d), lambda i: (i, 0)),
                  pl.BlockSpec((SUBLANES, d), lambda i: (jnp.maximum(i * (tm // SUBLANES) - 1, 0), 0)),
                  full(g), full(mu_wag), full(w1), full(a1), full(g1), full(w2), full(a2), full(g2),
                  full(w0), full(a0)],
        out_specs=[pl.BlockSpec((tm, width), lambda i: (i, 0))] * 3 + [pl.BlockSpec((tm, d), lambda i: (i, 0))],
        out_shape=[out, out, out, jax.ShapeDtypeStruct((m, d), BF16)],
        compiler_params=_compiler_params(("parallel",), 40),
        name="rwkv_lora",
    )(x, x, g, mu_wag, w1, a1, g1, w2, a2, g2, w0, a0)


def _split3(x):
    hi = x.astype(BF16)
    r1 = x - hi.astype(F32)
    mid = r1.astype(BF16)
    lo = (r1 - mid.astype(F32)).astype(BF16)
    return hi, mid, lo


def _split2(x):
    hi = x.astype(BF16)
    lo = (x - hi.astype(F32)).astype(BF16)
    return hi, lo


def _rwkv_kernel(pr_ref, pk_ref, pv_ref, lw_ref, a_ref, gate_ref, mu_ref, kk_ref, ka_ref, rk_ref,
                 lnw_ref, lnb_ref, o_ref, prev_ref, state_ref):
    t = RWKV_CHUNK
    n_pairs = o_ref.shape[1] // HEAD_PAIR

    @pl.when(pl.program_id(1) == 0)
    def _():
        prev_ref[...] = jnp.zeros_like(prev_ref)
        state_ref[...] = jnp.zeros_like(state_ref)

    def shift_mix(p_ref, idx):
        p = p_ref[...]
        ps = _shift_rows(p, prev_ref[idx, SUBLANES - 1:SUBLANES, :])
        prev_ref[idx] = p[t - SUBLANES:t, :]
        return p + (ps - p) * mu_ref[idx:idx + 1, :]

    r_all = shift_mix(pr_ref, 0)
    k_all = shift_mix(pk_ref, 1)
    v_all = shift_mix(pv_ref, 2)
    a_all = a_ref[...]
    lw_all = lw_ref[...]

    ri = lax.broadcasted_iota(jnp.int32, (t, t), 0)
    ci = lax.broadcasted_iota(jnp.int32, (t, t), 1)
    tril = jnp.where(ci <= ri, 1.0, 0.0).astype(BF16)
    hi, mid, lo = _split3(lw_all)
    cum_all = _dot(tril, hi) + _dot(tril, mid) + _dot(tril, lo)

    si = lax.broadcasted_iota(jnp.int32, (HEAD_PAIR, HEAD_PAIR), 0) // RWKV_HEAD
    sj = lax.broadcasted_iota(jnp.int32, (HEAD_PAIR, HEAD_PAIR), 1) // RWKV_HEAD
    seg = jnp.where(si == sj, 1.0, 0.0).astype(BF16)

    width = o_ref.shape[1]
    pairs = range(n_pairs)
    slab = lambda x, p: x[:, p * HEAD_PAIR:(p + 1) * HEAD_PAIR]
    lanes = lambda xs: jnp.concatenate(xs, axis=1)

    def head_sum_all(x):
        xh, xl = _split2(x)
        return lanes([_dot(slab(xh, p), seg) + _dot(slab(xl, p), seg) for p in pairs])

    head0 = (lax.broadcasted_iota(jnp.int32, (t, width), 1) % HEAD_PAIR) < RWKV_HEAD

    def expand(x):
        xb = x.astype(BF16)
        zero = jnp.zeros_like(xb)
        return jnp.concatenate([jnp.where(head0, xb, zero), jnp.where(head0, zero, xb)], axis=0)

    bi = lax.broadcasted_iota(jnp.int32, (2 * t, 2 * t), 0) % t
    bj = lax.broadcasted_iota(jnp.int32, (2 * t, 2 * t), 1) % t
    strict = bj < bi
    incl = bj <= bi
    eye = jnp.where(lax.broadcasted_iota(jnp.int32, (2 * t, 2 * t), 0)
                    == lax.broadcasted_iota(jnp.int32, (2 * t, 2 * t), 1), 1.0, 0.0)

    kk = k_all * kk_ref[...]
    kk = kk / jnp.maximum(jnp.sqrt(head_sum_all(kk * kk)), 1e-12)
    k_all = k_all * (1.0 + (a_all - 1.0) * ka_ref[...])
    b_all = kk * a_all
    cum_t = cum_all[t - 1:t, :]
    p_inv = jnp.exp(-cum_all)
    d_rest = jnp.exp(cum_t - cum_all)
    ar = jnp.concatenate([expand(kk * jnp.exp(cum_all - lw_all)), expand(r_all * jnp.exp(cum_all))], axis=0)
    kb = jnp.concatenate([expand(k_all * p_inv), expand(b_all * p_inv)], axis=0)
    kbt = jnp.concatenate([expand(k_all * d_rest), expand(-(b_all * d_rest))], axis=0)
    v_e = expand(v_all)
    decay_t = jnp.exp(cum_t)

    states = [state_ref[p] for p in pairs]
    scores = [_dot_nt(slab(ar, p), slab(kb, p)) for p in pairs]
    ar_h = [_dot_nt(slab(ar, p), states[p].astype(BF16)) for p in pairs]
    n_ab = [jnp.where(strict, s[:2 * t, 2 * t:], 0.0) for s in scores]
    a_ak = [jnp.where(strict, s[:2 * t, :2 * t], 0.0).astype(BF16) for s in scores]
    a_r = [jnp.concatenate([jnp.where(incl, s[2 * t:, :2 * t], 0.0),
                            jnp.where(incl, -s[2 * t:, 2 * t:], 0.0)], axis=1).astype(BF16)
           for s in scores]
    rhs = [(ar_h[p][:2 * t] + _dot(a_ak[p], slab(v_e, p))).astype(BF16) for p in pairs]

    inv = [eye - n for n in n_ab]
    n_pow = [n.astype(BF16) for n in n_ab]
    n_pow = [_dot(n, n).astype(BF16) for n in n_pow]
    n_levels = int(math.log2(t)) - 1
    for level in range(n_levels):
        nxt = [_dot(n, n).astype(BF16) for n in n_pow] if level + 1 < n_levels else None
        inv = [inv[p] + _dot(inv[p].astype(BF16), n_pow[p]) for p in pairs]
        n_pow = nxt

    u_e = [_dot(inv[p].astype(BF16), rhs[p]).astype(BF16) for p in pairs]
    vu = [jnp.concatenate([slab(v_e, p), u_e[p]], axis=0) for p in pairs]
    o_e = [ar_h[p][2 * t:] + _dot(a_r[p], vu[p]) for p in pairs]
    for p in pairs:
        state_ref[p] = states[p] * slab(decay_t, p) + _dot_tn(vu[p], slab(kbt, p))
    o = lanes([oe[:t] + oe[t:] for oe in o_e])

    inv_n = 1.0 / RWKV_HEAD
    mean = head_sum_all(o) * inv_n
    d = o - mean
    var = head_sum_all(d * d) * inv_n
    y = d * lax.rsqrt(var + RWKV_GN_EPS) * lnw_ref[...] + lnb_ref[...]
    y = y + head_sum_all(r_all * k_all * rk_ref[...]) * v_all
    o_ref[...] = y * gate_ref[...]


def _rwkv_core(proj, lw, a, gate, mu_rkv, k_k, k_a, r_k, ln_w, ln_b, *, batch, seq):
    width = lw.shape[1]
    t = RWKV_CHUNK
    nc = seq // t
    row = lambda col: pl.BlockSpec((t, width), lambda b, c: (b * nc + c, col))
    par = lambda arr: pl.BlockSpec(arr.shape, lambda b, c: (0, 0))
    return pl.pallas_call(
        _rwkv_kernel,
        grid=(batch, nc),
        in_specs=[row(0), row(1), row(2), row(0), row(0), row(0),
                  par(mu_rkv), par(k_k), par(k_a), par(r_k), par(ln_w), par(ln_b)],
        out_specs=row(0),
        out_shape=jax.ShapeDtypeStruct((batch * seq, width), F32),
        scratch_shapes=[pltpu.VMEM((3, SUBLANES, width), F32),
                        pltpu.VMEM((width // HEAD_PAIR, HEAD_PAIR, HEAD_PAIR), F32)],
        compiler_params=_compiler_params(("parallel", "arbitrary"), 32),
        name="rwkv_core",
    )(proj, proj, proj, lw, a, gate, mu_rkv, k_k, k_a, r_k, ln_w, ln_b)


def _s5_kernel(u_ref, fir_ref, bmat_ref, cmat_ref, pw_ref, y_ref,
               nat_ref, mid_ref, ud_ref, bu_ref, sr_ref, yd_ref):
    seq = u_ref.shape[0]
    t = S5_CHUNK
    nc = seq // t
    quarter = seq // 4
    n_lane_halves = u_ref.shape[1] // LANES
    n_state = bmat_ref.shape[2] // 2

    for h in range(n_lane_halves):
        lanes = slice(h * LANES, (h + 1) * LANES)
        nat_ref[h] = u_ref[:, lanes]
        for q in range(4):
            mid_ref[h, q * quarter:(q + 1) * quarter, :] = nat_ref[h, pl.ds(q, quarter, stride=4), :]
        for q in range(4):
            for r in range(4):
                pos = 4 * r + q
                ud_ref[pos * nc:(pos + 1) * nc, lanes] = (
                    mid_ref[h, pl.ds(q * quarter + r, nc, stride=4), :].astype(BF16))

    yd_ref[...] = _dot(ud_ref[...], fir_ref[0])
    for tau in range(1, t):
        yd_ref[tau * nc:, :] += _dot(ud_ref[0:(t - tau) * nc, :], fir_ref[tau])

    row = lax.broadcasted_iota(jnp.int32, (nc, n_state), 0)
    for hf in range(bmat_ref.shape[0]):
        pw = lambda k: (pw_ref[hf, k:k + 1, :n_state], pw_ref[hf, k:k + 1, n_state:])
        cmul = lambda x, y: (x[0] * y[0] - x[1] * y[1], x[0] * y[1] + x[1] * y[0])
        bu_ref[...] = _dot(ud_ref[...], bmat_ref[hf])
        lam = pw(0)
        z = (bu_ref[0:nc, :n_state], bu_ref[0:nc, n_state:])
        for pos in range(1, t):
            rows = slice(pos * nc, (pos + 1) * nc)
            lz = cmul(lam, z)
            z = (lz[0] + bu_ref[rows, :n_state], lz[1] + bu_ref[rows, n_state:])
        shift, level = 1, 0
        while shift < nc:
            valid = row >= shift
            prev = (jnp.where(valid, pltpu.roll(z[0], shift, axis=0), 0.0),
                    jnp.where(valid, pltpu.roll(z[1], shift, axis=0), 0.0))
            step = cmul(pw(t + level), prev)
            z = (z[0] + step[0], z[1] + step[1])
            shift *= 2
            level += 1
        first = row >= 1
        s_in = (jnp.where(first, pltpu.roll(z[0], 1, axis=0), 0.0),
                jnp.where(first, pltpu.roll(z[1], 1, axis=0), 0.0))
        for pos in range(t):
            rows = slice(pos * nc, (pos + 1) * nc)
            rot = cmul(pw(pos), s_in)
            sr_ref[rows, :n_state] = rot[0].astype(BF16)
            sr_ref[rows, n_state:] = rot[1].astype(BF16)
        yd_ref[...] += _dot(sr_ref[...], cmat_ref[hf])

    for h in range(n_lane_halves):
        lanes = slice(h * LANES, (h + 1) * LANES)
        for q in range(4):
            for r in range(4):
                pos = 4 * r + q
                mid_ref[h, pl.ds(q * quarter + r, nc, stride=4), :] = yd_ref[pos * nc:(pos + 1) * nc, lanes]
        for q in range(4):
            nat_ref[h, pl.ds(q, quarter, stride=4), :] = mid_ref[h, q * quarter:(q + 1) * quarter, :]
        y_ref[:, lanes] = nat_ref[h]


def _s5_core(proj, fir, bmat, cmat, pw, *, batch, seq, width):
    n_slabs = width // S5_SLAB
    u_col0 = (proj.shape[1] - width) // S5_SLAB
    wblk = lambda arr: pl.BlockSpec((None,) + arr.shape[1:], lambda b, s: (s,) + (0,) * (arr.ndim - 1))
    n_state2 = bmat.shape[-1]
    return pl.pallas_call(
        _s5_kernel,
        grid=(batch, n_slabs),
        in_specs=[pl.BlockSpec((seq, S5_SLAB), lambda b, s: (b, u_col0 + s)),
                  wblk(fir), wblk(bmat), wblk(cmat), wblk(pw)],
        out_specs=pl.BlockSpec((seq, S5_SLAB), lambda b, s: (b, s)),
        out_shape=jax.ShapeDtypeStruct((batch * seq, width), F32),
        scratch_shapes=[pltpu.VMEM((S5_SLAB // LANES, seq, LANES), F32),
                        pltpu.VMEM((S5_SLAB // LANES, seq, LANES), F32),
                        pltpu.VMEM((seq, S5_SLAB), BF16),
                        pltpu.VMEM((seq, n_state2), F32),
                        pltpu.VMEM((seq, n_state2), BF16),
                        pltpu.VMEM((seq, S5_SLAB), F32)],
        compiler_params=_compiler_params(("parallel", "arbitrary"), 48),
        name="s5_core",
    )(proj, fir, bmat, cmat, pw)


def _s5_matrices(lam_re, lam_im, log_dt, b_re, b_im, c_re, c_im, *, seq):
    t = S5_CHUNK
    groups = lam_re.shape[0]
    gs = S5_SLAB // S5_GROUP
    gh = gs // 2
    n_slabs = groups // gs
    dt = jnp.exp(log_dt)[:, None]
    mag = jnp.exp(lam_re * dt)
    lb_re, lb_im = mag * jnp.cos(lam_im * dt), mag * jnp.sin(lam_im * dt)
    den = lam_re * lam_re + lam_im * lam_im
    z_re = ((lb_re - 1.0) * lam_re + lb_im * lam_im) / den
    z_im = (lb_im * lam_re - (lb_re - 1.0) * lam_im) / den
    bb_re = z_re[..., None] * b_re - z_im[..., None] * b_im
    bb_im = z_re[..., None] * b_im + z_im[..., None] * b_re

    def mul(carry, _):
        pr, pi = carry
        return (pr * lb_re - pi * lb_im, pr * lb_im + pi * lb_re), (pr, pi)

    _, (pw_re, pw_im) = lax.scan(mul, (jnp.ones_like(lb_re), jnp.zeros_like(lb_re)), None, length=t + 1)

    cl_re = pw_re[:t, :, None, :] * c_re[None] - pw_im[:t, :, None, :] * c_im[None]
    cl_im = pw_re[:t, :, None, :] * c_im[None] + pw_im[:t, :, None, :] * c_re[None]
    taps = jnp.einsum('tghp,gpi->tghi', cl_re, bb_re) - jnp.einsum('tghp,gpi->tghi', cl_im, bb_im)
    fir = jnp.einsum('tsgoi,gk->stgiko', taps.reshape(t, n_slabs, gs, S5_GROUP, S5_GROUP),
                     jnp.eye(gs, dtype=F32)).reshape(n_slabs, t, S5_SLAB, S5_SLAB)

    eye_h = jnp.eye(gh, dtype=F32)
    eye_2 = jnp.eye(2, dtype=F32)

    def b_slab(bb):
        blk = jnp.einsum('abgph,gk->abghkp', bb.reshape(n_slabs, 2, gh, S5_STATE, S5_GROUP), eye_h)
        blk = blk.reshape(n_slabs, 2, gh * S5_GROUP, gh * S5_STATE)
        return (eye_2[None, :, :, None, None] * blk[:, :, None]).reshape(n_slabs, 2, S5_SLAB, gh * S5_STATE)

    def c_slab(cc):
        blk = jnp.einsum('abghp,gk->abgpkh', cc.reshape(n_slabs, 2, gh, S5_GROUP, S5_STATE), eye_h)
        blk = blk.reshape(n_slabs, 2, gh * S5_STATE, gh * S5_GROUP)
        return (eye_2[None, :, None, :, None] * blk[:, :, :, None]).reshape(n_slabs, 2, gh * S5_STATE, S5_SLAB)

    bmat = jnp.concatenate([b_slab(bb_re), b_slab(bb_im)], axis=-1)
    cmat = jnp.concatenate([c_slab(c_re), -c_slab(c_im)], axis=2)

    squares = [(pw_re[t], pw_im[t])]
    for _ in range((seq // t - 1).bit_length() - 1):
        sr, si = squares[-1]
        squares.append((sr * sr - si * si, 2.0 * sr * si))
    tab_re = jnp.concatenate([pw_re[1:], jnp.stack([s[0] for s in squares])], axis=0)
    tab_im = jnp.concatenate([pw_im[1:], jnp.stack([s[1] for s in squares])], axis=0)

    def state_lanes(x):
        return jnp.transpose(x.reshape(x.shape[0], n_slabs, 2, gh * S5_STATE), (1, 2, 0, 3))

    pw = jnp.concatenate([state_lanes(tab_re), state_lanes(tab_im)], axis=-1)
    return fir.astype(BF16), bmat.astype(BF16), cmat.astype(BF16), pw


def _mix_out_kernel(x_ref, ya_ref, ys_ref, u_ref, d_ref, gw_ref, gb_ref, wa_ref, wb_ref, o_ref):
    yb = _gelu_tanh(ys_ref[...] + d_ref[...] * u_ref[...])
    yb = yb * _sigmoid(_dot(yb.astype(BF16), gw_ref[...]) + gb_ref[...])
    o_ref[...] = (x_ref[...] + _dot(ya_ref[...].astype(BF16), wa_ref[...])
                  + _dot(yb.astype(BF16), wb_ref[...]))


def _mix_out(x, ya, ys, proj, d_skip, glu_w, glu_b, w_out_a, w_out_b, *, tm):
    m, d = x.shape
    width = ya.shape[1]
    u_col = proj.shape[1] // width - 1
    full = lambda arr: pl.BlockSpec(arr.shape, lambda i: (0, 0))
    row = lambda w, col=0: pl.BlockSpec((tm, w), lambda i: (i, col))
    return pl.pallas_call(
        _mix_out_kernel,
        grid=(m // tm,),
        in_specs=[row(d), row(width), row(width), row(width, u_col), full(d_skip), full(glu_w),
                  full(glu_b), full(w_out_a), full(w_out_b)],
        out_specs=row(d),
        out_shape=jax.ShapeDtypeStruct((m, d), F32),
        compiler_params=_compiler_params(("parallel",), 48),
        name="mix_out",
    )(x, ya, ys, proj, d_skip, glu_w, glu_b, w_out_a, w_out_b)


def _mlp_kernel(x_ref, g_ref, w1_ref, w2_ref, gn_ref, o_ref, *rest, norm_mode):
    if norm_mode == "next":
        hn_ref, h_ref = rest
    else:
        (h_ref,) = rest
    f = pl.program_id(1)

    @pl.when(f == 0)
    def _():
        x = x_ref[...]
        h_ref[...] = _rms(x, g_ref[...]).astype(BF16)
        o_ref[...] = x

    hid = jnp.maximum(_dot(h_ref[...], w1_ref[...]), 0.0)
    o_ref[...] += _dot((hid * hid).astype(BF16), w2_ref[...])

    @pl.when(f == pl.num_programs(1) - 1)
    def _():
        normed = _rms(o_ref[...], gn_ref[...])
        if norm_mode == "next":
            hn_ref[...] = normed.astype(BF16)
        else:
            o_ref[...] = normed


def _mlp(x, g, w1, w2, gn, *, norm_mode, tm, tf):
    assert norm_mode in ("final", "next")
    m, d = x.shape
    dff = w1.shape[1]
    row = pl.BlockSpec((tm, d), lambda i, f: (i, 0))
    out_specs, out_shape = [row], [jax.ShapeDtypeStruct((m, d), F32)]
    if norm_mode == "next":
        out_specs.append(row)
        out_shape.append(jax.ShapeDtypeStruct((m, d), BF16))
    return pl.pallas_call(
        functools.partial(_mlp_kernel, norm_mode=norm_mode),
        grid=(m // tm, dff // tf),
        in_specs=[row,
                  pl.BlockSpec((1, d), lambda i, f: (0, 0)),
                  pl.BlockSpec((d, tf), lambda i, f: (0, f)),
                  pl.BlockSpec((tf, d), lambda i, f: (f, 0)),
                  pl.BlockSpec((1, d), lambda i, f: (0, 0))],
        out_specs=out_specs,
        out_shape=out_shape,
        scratch_shapes=[pltpu.VMEM((tm, d), BF16)],
        compiler_params=_compiler_params(("parallel", "arbitrary"), 48),
        name="mlp",
    )(x, g, w1, w2, gn)


def _rglru_kernel(gate_ref, xr_ref, cw_ref, cb_ref, wa_ref, wx_ref, ba_ref, bx_ref, lam_ref, o_ref,
                  tail_ref, h_ref):
    tt, width = xr_ref.shape
    n_heads, blk = wa_ref.shape[0], wa_ref.shape[1]
    c = pl.program_id(1)

    @pl.when(c == 0)
    def _():
        tail_ref[...] = jnp.zeros_like(tail_ref)
        h_ref[...] = jnp.zeros_like(h_ref)

    xr = xr_ref[...]
    tail = tail_ref[...]
    row8 = lax.broadcasted_iota(jnp.int32, (SUBLANES, width), 0)
    xc = xr * cw_ref[CONV_WIDTH - 1:CONV_WIDTH, :] + cb_ref[...]
    for s in range(1, CONV_WIDTH):
        xs = pltpu.roll(xr, s, axis=0)
        head = jnp.where(row8 < s, pltpu.roll(tail, s, axis=0), xs[:SUBLANES])
        xs = jnp.concatenate([head, xs[SUBLANES:]], axis=0)
        xc = xc + xs * cw_ref[CONV_WIDTH - 1 - s:CONV_WIDTH - s, :]
    tail_ref[...] = xr[tt - SUBLANES:tt, :]

    xcb = xc.astype(BF16)
    pre_a = jnp.concatenate(
        [_dot(xcb[:, h * blk:(h + 1) * blk], wa_ref[h]) for h in range(n_heads)], axis=1)
    pre_x = jnp.concatenate(
        [_dot(xcb[:, h * blk:(h + 1) * blk], wx_ref[h]) for h in range(n_heads)], axis=1)
    gate_r = _sigmoid(pre_a + ba_ref[...])
    gate_i = _sigmoid(pre_x + bx_ref[...])
    log_a = LRU_C * gate_r * (-_softplus(-lam_ref[...]))
    a = jnp.exp(log_a)
    mult = jnp.sqrt(1.0 - a * a)
    row = lax.broadcasted_iota(jnp.int32, (tt, width), 0)
    mult = jnp.where((row == 0) & (c == 0), 1.0, mult)
    b = xc * gate_i * mult

    n_groups = tt // SUBLANES
    a3 = a.reshape(n_groups, SUBLANES, width)
    b3 = b.reshape(n_groups, SUBLANES, width)
    sub = lax.broadcasted_iota(jnp.int32, a3.shape, 1)
    shift = 1
    while shift < SUBLANES:
        valid = sub >= shift
        b3 = b3 + a3 * jnp.where(valid, pltpu.roll(b3, shift, axis=1), 0.0)
        a3 = a3 * jnp.where(valid, pltpu.roll(a3, shift, axis=1), 1.0)
        shift *= 2
    carry = h_ref[0:1, :]
    groups = []
    for i in range(n_groups):
        h_i = b3[i] + a3[i] * carry
        groups.append(h_i)
        carry = h_i[SUBLANES - 1:SUBLANES, :]
    h_ref[...] = jnp.broadcast_to(carry, (SUBLANES, width))
    o_ref[...] = _gelu_tanh(gate_ref[...]) * jnp.concatenate(groups, axis=0)


def _rglru(proj, conv_w, conv_b, w_a, w_x, b_a, b_x, lam, *, batch, seq, tt):
    width = lam.shape[1]
    nc = seq // tt
    row = lambda col: pl.BlockSpec((tt, width), lambda b, c: (b * nc + c, col))
    par = lambda arr: pl.BlockSpec(arr.shape, lambda b, c: (0,) * arr.ndim)
    return pl.pallas_call(
        _rglru_kernel,
        grid=(batch, nc),
        in_specs=[row(0), row(1), par(conv_w), par(conv_b), par(w_a), par(w_x), par(b_a), par(b_x),
                  par(lam)],
        out_specs=row(0),
        out_shape=jax.ShapeDtypeStruct((batch * seq, width), F32),
        scratch_shapes=[pltpu.VMEM((SUBLANES, width), F32), pltpu.VMEM((SUBLANES, width), F32)],
        compiler_params=_compiler_params(("parallel", "arbitrary"), 48),
        name="rglru",
    )(proj, proj, conv_w, conv_b, w_a, w_x, b_a, b_x, lam)


def _proj_res_kernel(x_ref, y_ref, w_ref, o_ref):
    o_ref[...] = x_ref[...] + _dot(y_ref[...].astype(BF16), w_ref[...])


def _proj_res(x, y, w, *, tm):
    m, d = x.shape
    return pl.pallas_call(
        _proj_res_kernel,
        grid=(m // tm,),
        in_specs=[pl.BlockSpec((tm, d), lambda i: (i, 0)),
                  pl.BlockSpec((tm, y.shape[1]), lambda i: (i, 0)),
                  pl.BlockSpec(w.shape, lambda i: (0, 0))],
        out_specs=pl.BlockSpec((tm, d), lambda i: (i, 0)),
        out_shape=jax.ShapeDtypeStruct((m, d), F32),
        compiler_params=_compiler_params(("parallel",), 48),
        name="proj_res",
    )(x, y, w)


def _pad_cols(w, n):
    return jnp.pad(w, ((0, 0), (0, n - w.shape[1])))


def _pad_rows(w, n):
    return jnp.pad(w, ((0, n - w.shape[0]), (0, 0)))


def kernel(x, norm_mix_g, norm_mlp_g, norm_f_g, mlp_w1, mlp_w2, hy_w_in, hy_w_out, rw_mu_rkv, rw_mu_wag, rw_w0, rw_w1, rw_w2, rw_a0, rw_a1, rw_a2, rw_g1, rw_g2, rw_k_k, rw_k_a, rw_r_k, rw_ln_w, rw_ln_b, s5_lam_re, s5_lam_im, s5_log_dt, s5_b_re, s5_b_im, s5_c_re, s5_c_im, s5_d, s5_glu_w, s5_glu_b, rg_w_in, rg_w_out, rg_conv_w, rg_conv_b, rg_w_a, rg_b_a, rg_w_x, rg_b_x, rg_lam):
    batch, seq, d = x.shape
    m = batch * seq
    rw = rw_w2.shape[-1]
    row = lambda v: v.reshape(1, -1)
    bf = lambda w: w.astype(BF16)
    lane = 128
    xf = x.reshape(m, d)

    lw, a, gate, hn = _rwkv_lora(
        xf, row(norm_mix_g[0]), rw_mu_wag[0],
        bf(_pad_cols(rw_w1[0], lane)), bf(_pad_cols(rw_a1[0], lane)), bf(rw_g1[0]),
        bf(_pad_rows(rw_w2[0], lane)), bf(_pad_rows(rw_a2[0], lane)), bf(rw_g2[0]),
        row(rw_w0[0]), row(rw_a0[0]), seq=seq, tm=256)
    proj = _matmul(hn, bf(hy_w_in[0]), tm=512, tn=1024)
    ya = _rwkv_core(proj, lw, a, gate, rw_mu_rkv[0], row(rw_k_k[0]), row(rw_k_a[0]),
                    row(rw_r_k[0]), row(rw_ln_w[0]), row(rw_ln_b[0]), batch=batch, seq=seq)

    mats = _s5_matrices(s5_lam_re[0], s5_lam_im[0], s5_log_dt[0], s5_b_re[0], s5_b_im[0],
                        s5_c_re[0], s5_c_im[0], seq=seq)
    ys = _s5_core(proj, *mats, batch=batch, seq=seq, width=s5_d.shape[-1])

    x1 = _mix_out(xf, ya, ys, proj, row(s5_d[0]), bf(s5_glu_w[0]), row(s5_glu_b[0]),
                  bf(hy_w_out[0][:rw]), bf(hy_w_out[0][rw:]), tm=256)
    x2, hn1 = _mlp(x1, row(norm_mlp_g[0]), bf(mlp_w1[0]), bf(mlp_w2[0]), row(norm_mix_g[1]),
                   norm_mode="next", tm=512, tf=1024)

    proj1 = _matmul(hn1, bf(rg_w_in[0]), tm=512, tn=1024)
    yr = _rglru(proj1, rg_conv_w[0], row(rg_conv_b[0]), bf(rg_w_a[0]), bf(rg_w_x[0]),
                row(rg_b_a[0]), row(rg_b_x[0]), row(rg_lam[0]), batch=batch, seq=seq, tt=256)
    x3 = _proj_res(x2, yr, bf(rg_w_out[0]), tm=512)
    (out,) = _mlp(x3, row(norm_mlp_g[1]), bf(mlp_w1[1]), bf(mlp_w2[1]), row(norm_f_g),
                  norm_mode="final", tm=512, tf=1024)
    return out.reshape(batch, seq, d)
```

```python
import functools
import math

import jax
import jax.numpy as jnp
from jax import lax
from jax.experimental import pallas as pl
from jax.experimental.pallas import tpu as pltpu

F32 = jnp.float32
BF16 = jnp.bfloat16

NORM_EPS = 1e-6
RWKV_HEAD = 64
RWKV_GN_EPS = 64e-5
RWKV_CHUNK = 64
HEAD_PAIR = 2 * RWKV_HEAD
S5_GROUP = 16
S5_STATE = 64
S5_CHUNK = 16
S5_SLAB = 256
LRU_C = 8.0
CONV_WIDTH = 4
SUBLANES = 8
LANES = 128
V7X_VMEM_BYTES = 64 * 1024 * 1024


def _compiler_params(semantics, vmem_mib):
    assert vmem_mib * 1024 * 1024 < V7X_VMEM_BYTES
    return pltpu.CompilerParams(dimension_semantics=semantics,
                                vmem_limit_bytes=vmem_mib * 1024 * 1024)


def _rms(x, g):
    return x * lax.rsqrt(jnp.mean(x * x, axis=-1, keepdims=True) + NORM_EPS) * g


def _gelu_tanh(x):
    c = math.sqrt(2.0 / math.pi)
    return 0.5 * x * (1.0 + jnp.tanh(c * (x + 0.044715 * (x * x * x))))


def _sigmoid(x):
    return 1.0 / (1.0 + jnp.exp(-x))


def _softplus(x):
    return jnp.maximum(x, 0.0) + jnp.log1p(jnp.exp(-jnp.abs(x)))


def _dot(a, b):
    return jnp.dot(a, b, preferred_element_type=F32)


def _dot_nt(a, b):
    return lax.dot_general(a, b, (((1,), (1,)), ((), ())), preferred_element_type=F32)


def _dot_tn(a, b):
    return lax.dot_general(a, b, (((0,), (0,)), ((), ())), preferred_element_type=F32)


def _shift_rows(x, prev_row):
    row = lax.broadcasted_iota(jnp.int32, x.shape, 0)
    return jnp.where(row == 0, prev_row, pltpu.roll(x, 1, axis=0))


def _matmul_kernel(h_ref, w_ref, o_ref):
    o_ref[...] = _dot(h_ref[...], w_ref[...])


def _matmul(h, w, *, tm, tn):
    m, d = h.shape
    n = w.shape[1]
    return pl.pallas_call(
        _matmul_kernel,
        grid=(m // tm, n // tn),
        in_specs=[pl.BlockSpec((tm, d), lambda i, j: (i, 0)),
                  pl.BlockSpec((d, tn), lambda i, j: (0, j))],
        out_specs=pl.BlockSpec((tm, tn), lambda i, j: (i, j)),
        out_shape=jax.ShapeDtypeStruct((m, n), F32),
        compiler_params=_compiler_params(("parallel", "arbitrary"), 40),
        name="in_proj",
    )(h, w)


def _lora_kernel(x_ref, xp_ref, g_ref, mu_ref, w1_ref, a1_ref, g1_ref, w2_ref, a2_ref, g2_ref,
                 w0_ref, a0_ref, win_ref, lw_ref, a_ref, gate_ref, proj_ref, *, blocks_per_seq):
    g = g_ref[...]
    h = _rms(x_ref[...], g)
    proj_ref[...] = _dot(h.astype(BF16), win_ref[...])
    hp = _rms(xp_ref[...], g)
    first = (pl.program_id(0) % blocks_per_seq) == 0
    prev_row = jnp.where(first, 0.0, hp[SUBLANES - 1:SUBLANES, :])
    dh = _shift_rows(h, prev_row) - h
    xw = (h + dh * mu_ref[0:1, :]).astype(BF16)
    xa = (h + dh * mu_ref[1:2, :]).astype(BF16)
    xg = (h + dh * mu_ref[2:3, :]).astype(BF16)
    tw = jnp.tanh(_dot(xw, w1_ref[...])).astype(BF16)
    w_pre = w0_ref[...] + _dot(tw, w2_ref[...])
    w_log = -_softplus(-w_pre) - 0.5
    lw_ref[...] = -jnp.exp(w_log)
    ta = _dot(xa, a1_ref[...]).astype(BF16)
    a_ref[...] = _sigmoid(a0_ref[...] + _dot(ta, a2_ref[...]))
    tg = _sigmoid(_dot(xg, g1_ref[...])).astype(BF16)
    gate_ref[...] = _dot(tg, g2_ref[...])


def _rwkv_lora(x, g, mu_wag, w1, a1, g1, w2, a2, g2, w0, a0, w_in, *, seq, tm):
    m, d = x.shape
    width = w2.shape[1]
    n_proj = w_in.shape[1]
    blocks_per_seq = seq // tm
    full = lambda arr: pl.BlockSpec(arr.shape, lambda i: (0, 0), pipeline_mode=pl.Buffered(1))
    out = jax.ShapeDtypeStruct((m, width), F32)
    return pl.pallas_call(
        functools.partial(_lora_kernel, blocks_per_seq=blocks_per_seq),
        grid=(m // tm,),
        in_specs=[pl.BlockSpec((tm, d), lambda i: (i, 0)),
                  pl.BlockSpec((SUBLANES, d), lambda i: (jnp.maximum(i * (tm // SUBLANES) - 1, 0), 0)),
                  full(g), full(mu_wag), full(w1), full(a1), full(g1), full(w2), full(a2), full(g2),
                  full(w0), full(a0), full(w_in)],
        out_specs=[pl.BlockSpec((tm, width), lambda i: (i, 0))] * 3 + [pl.BlockSpec((tm, n_proj), lambda i: (i, 0))],
        out_shape=[out, out, out, jax.ShapeDtypeStruct((m, n_proj), F32)],
        compiler_params=_compiler_params(("parallel",), 56),
        name="rwkv_lora",
    )(x, x, g, mu_wag, w1, a1, g1, w2, a2, g2, w0, a0, w_in)


def _split3(x):
    hi = x.astype(BF16)
    r1 = x - hi.astype(F32)
    mid = r1.astype(BF16)
    lo = (r1 - mid.astype(F32)).astype(BF16)
    return hi, mid, lo


def _split2(x):
    hi = x.astype(BF16)
    lo = (x - hi.astype(F32)).astype(BF16)
    return hi, lo


def _rwkv_kernel(pr_ref, pk_ref, pv_ref, lw_ref, a_ref, gate_ref, mu_ref, kk_ref, ka_ref, rk_ref,
                 lnw_ref, lnb_ref, o_ref, prev_ref, state_ref):
    t = RWKV_CHUNK
    n_pairs = o_ref.shape[1] // HEAD_PAIR

    @pl.when(pl.program_id(1) == 0)
    def _():
        prev_ref[...] = jnp.zeros_like(prev_ref)
        state_ref[...] = jnp.zeros_like(state_ref)

    def shift_mix(p_ref, idx):
        p = p_ref[...]
        ps = _shift_rows(p, prev_ref[idx, SUBLANES - 1:SUBLANES, :])
        prev_ref[idx] = p[t - SUBLANES:t, :]
        return p + (ps - p) * mu_ref[idx:idx + 1, :]

    r_all = shift_mix(pr_ref, 0)
    k_all = shift_mix(pk_ref, 1)
    v_all = shift_mix(pv_ref, 2)
    a_all = a_ref[...]
    lw_all = lw_ref[...]

    ri = lax.broadcasted_iota(jnp.int32, (t, t), 0)
    ci = lax.broadcasted_iota(jnp.int32, (t, t), 1)
    tril = jnp.where(ci <= ri, 1.0, 0.0).astype(BF16)
    hi, mid, lo = _split3(lw_all)
    cum_all = _dot(tril, hi) + _dot(tril, mid) + _dot(tril, lo)

    si = lax.broadcasted_iota(jnp.int32, (HEAD_PAIR, HEAD_PAIR), 0) // RWKV_HEAD
    sj = lax.broadcasted_iota(jnp.int32, (HEAD_PAIR, HEAD_PAIR), 1) // RWKV_HEAD
    seg = jnp.where(si == sj, 1.0, 0.0).astype(BF16)

    width = o_ref.shape[1]
    pairs = range(n_pairs)
    slab = lambda x, p: x[:, p * HEAD_PAIR:(p + 1) * HEAD_PAIR]
    lanes = lambda xs: jnp.concatenate(xs, axis=1)

    def head_sum_all(x):
        xh, xl = _split2(x)
        return lanes([_dot(slab(xh, p), seg) + _dot(slab(xl, p), seg) for p in pairs])

    head0 = (lax.broadcasted_iota(jnp.int32, (t, width), 1) % HEAD_PAIR) < RWKV_HEAD

    def expand(x):
        xb = x.astype(BF16)
        zero = jnp.zeros_like(xb)
        return jnp.concatenate([jnp.where(head0, xb, zero), jnp.where(head0, zero, xb)], axis=0)

    bi = lax.broadcasted_iota(jnp.int32, (2 * t, 2 * t), 0) % t
    bj = lax.broadcasted_iota(jnp.int32, (2 * t, 2 * t), 1) % t
    strict = bj < bi
    incl = bj <= bi
    eye = jnp.where(lax.broadcasted_iota(jnp.int32, (2 * t, 2 * t), 0)
                    == lax.broadcasted_iota(jnp.int32, (2 * t, 2 * t), 1), 1.0, 0.0)

    kk = k_all * kk_ref[...]
    kk = kk / jnp.maximum(jnp.sqrt(head_sum_all(kk * kk)), 1e-12)
    k_all = k_all * (1.0 + (a_all - 1.0) * ka_ref[...])
    b_all = kk * a_all
    cum_t = cum_all[t - 1:t, :]
    p_inv = jnp.exp(-cum_all)
    d_rest = jnp.exp(cum_t - cum_all)
    ar = jnp.concatenate([expand(kk * jnp.exp(cum_all - lw_all)), expand(r_all * jnp.exp(cum_all))], axis=0)
    kb = jnp.concatenate([expand(k_all * p_inv), expand(b_all * p_inv)], axis=0)
    kbt = jnp.concatenate([expand(k_all * d_rest), expand(-(b_all * d_rest))], axis=0)
    v_e = expand(v_all)
    decay_t = jnp.exp(cum_t)

    states = [state_ref[p] for p in pairs]
    scores = [_dot_nt(slab(ar, p), slab(kb, p)) for p in pairs]
    ar_h = [_dot_nt(slab(ar, p), states[p].astype(BF16)) for p in pairs]
    n_ab = [jnp.where(strict, s[:2 * t, 2 * t:], 0.0) for s in scores]
    a_ak = [jnp.where(strict, s[:2 * t, :2 * t], 0.0).astype(BF16) for s in scores]
    a_r = [jnp.concatenate([jnp.where(incl, s[2 * t:, :2 * t], 0.0),
                            jnp.where(incl, -s[2 * t:, 2 * t:], 0.0)], axis=1).astype(BF16)
           for s in scores]
    rhs = [(ar_h[p][:2 * t] + _dot(a_ak[p], slab(v_e, p))).astype(BF16) for p in pairs]

    inv = [eye - n for n in n_ab]
    n_pow = [n.astype(BF16) for n in n_ab]
    n_pow = [_dot(n, n).astype(BF16) for n in n_pow]
    n_levels = int(math.log2(t)) - 1
    for level in range(n_levels):
        nxt = [_dot(n, n).astype(BF16) for n in n_pow] if level + 1 < n_levels else None
        inv = [inv[p] + _dot(inv[p].astype(BF16), n_pow[p]) for p in pairs]
        n_pow = nxt

    u_e = [_dot(inv[p].astype(BF16), rhs[p]).astype(BF16) for p in pairs]
    vu = [jnp.concatenate([slab(v_e, p), u_e[p]], axis=0) for p in pairs]
    o_e = [ar_h[p][2 * t:] + _dot(a_r[p], vu[p]) for p in pairs]
    for p in pairs:
        state_ref[p] = states[p] * slab(decay_t, p) + _dot_tn(vu[p], slab(kbt, p))
    o = lanes([oe[:t] + oe[t:] for oe in o_e])

    inv_n = 1.0 / RWKV_HEAD
    mean = head_sum_all(o) * inv_n
    d = o - mean
    var = head_sum_all(d * d) * inv_n
    y = d * lax.rsqrt(var + RWKV_GN_EPS) * lnw_ref[...] + lnb_ref[...]
    y = y + head_sum_all(r_all * k_all * rk_ref[...]) * v_all
    o_ref[...] = y * gate_ref[...]


def _rwkv_core(proj, lw, a, gate, mu_rkv, k_k, k_a, r_k, ln_w, ln_b, *, batch, seq):
    width = lw.shape[1]
    t = RWKV_CHUNK
    nc = seq // t
    row = lambda col: pl.BlockSpec((t, width), lambda b, c: (b * nc + c, col))
    par = lambda arr: pl.BlockSpec(arr.shape, lambda b, c: (0, 0))
    return pl.pallas_call(
        _rwkv_kernel,
        grid=(batch, nc),
        in_specs=[row(0), row(1), row(2), row(0), row(0), row(0),
                  par(mu_rkv), par(k_k), par(k_a), par(r_k), par(ln_w), par(ln_b)],
        out_specs=row(0),
        out_shape=jax.ShapeDtypeStruct((batch * seq, width), F32),
        scratch_shapes=[pltpu.VMEM((3, SUBLANES, width), F32),
                        pltpu.VMEM((width // HEAD_PAIR, HEAD_PAIR, HEAD_PAIR), F32)],
        compiler_params=_compiler_params(("parallel", "arbitrary"), 32),
        name="rwkv_core",
    )(proj, proj, proj, lw, a, gate, mu_rkv, k_k, k_a, r_k, ln_w, ln_b)


def _s5_kernel(u_ref, fir_ref, bmat_ref, cmat_ref, pw_ref, y_ref,
               nat_ref, mid_ref, ud_ref, bu_ref, sr_ref, yd_ref):
    seq = u_ref.shape[0]
    t = S5_CHUNK
    nc = seq // t
    quarter = seq // 4
    n_lane_halves = u_ref.shape[1] // LANES
    n_state = bmat_ref.shape[2] // 2

    for h in range(n_lane_halves):
        lanes = slice(h * LANES, (h + 1) * LANES)
        nat_ref[h] = u_ref[:, lanes]
        for q in range(4):
            mid_ref[h, q * quarter:(q + 1) * quarter, :] = nat_ref[h, pl.ds(q, quarter, stride=4), :]
        for q in range(4):
            for r in range(4):
                pos = 4 * r + q
                ud_ref[pos * nc:(pos + 1) * nc, lanes] = (
                    mid_ref[h, pl.ds(q * quarter + r, nc, stride=4), :].astype(BF16))

    yd_ref[...] = _dot(ud_ref[...], fir_ref[0])
    for tau in range(1, t):
        yd_ref[tau * nc:, :] += _dot(ud_ref[0:(t - tau) * nc, :], fir_ref[tau])

    row = lax.broadcasted_iota(jnp.int32, (nc, n_state), 0)
    for hf in range(bmat_ref.shape[0]):
        pw = lambda k: (pw_ref[hf, k:k + 1, :n_state], pw_ref[hf, k:k + 1, n_state:])
        cmul = lambda x, y: (x[0] * y[0] - x[1] * y[1], x[0] * y[1] + x[1] * y[0])
        bu_ref[...] = _dot(ud_ref[...], bmat_ref[hf])
        lam = pw(0)
        z = (bu_ref[0:nc, :n_state], bu_ref[0:nc, n_state:])
        for pos in range(1, t):
            rows = slice(pos * nc, (pos + 1) * nc)
            lz = cmul(lam, z)
            z = (lz[0] + bu_ref[rows, :n_state], lz[1] + bu_ref[rows, n_state:])
        shift, level = 1, 0
        while shift < nc:
            valid = row >= shift
            prev = (jnp.where(valid, pltpu.roll(z[0], shift, axis=0), 0.0),
                    jnp.where(valid, pltpu.roll(z[1], shift, axis=0), 0.0))
            step = cmul(pw(t + level), prev)
            z = (z[0] + step[0], z[1] + step[1])
            shift *= 2
            level += 1
        first = row >= 1
        s_in = (jnp.where(first, pltpu.roll(z[0], 1, axis=0), 0.0),
                jnp.where(first, pltpu.roll(z[1], 1, axis=0), 0.0))
        for pos in range(t):
            rows = slice(pos * nc, (pos + 1) * nc)
            rot = cmul(pw(pos), s_in)
            sr_ref[rows, :n_state] = rot[0].astype(BF16)
            sr_ref[rows, n_state:] = rot[1].astype(BF16)
        yd_ref[...] += _dot(sr_ref[...], cmat_ref[hf])

    for h in range(n_lane_halves):
        lanes = slice(h * LANES, (h + 1) * LANES)
        for q in range(4):
            for r in range(4):
                pos = 4 * r + q
                mid_ref[h, pl.ds(q * quarter + r, nc, stride=4), :] = yd_ref[pos * nc:(pos + 1) * nc, lanes]
        for q in range(4):
            nat_ref[h, pl.ds(q, quarter, stride=4), :] = mid_ref[h, q * quarter:(q + 1) * quarter, :]
        y_ref[:, lanes] = nat_ref[h]


def _s5_core(proj, fir, bmat, cmat, pw, *, batch, seq, width):
    n_slabs = width // S5_SLAB
    u_col0 = (proj.shape[1] - width) // S5_SLAB
    wblk = lambda arr: pl.BlockSpec((None,) + arr.shape[1:], lambda b, s: (s,) + (0,) * (arr.ndim - 1))
    n_state2 = bmat.shape[-1]
    return pl.pallas_call(
        _s5_kernel,
        grid=(batch, n_slabs),
        in_specs=[pl.BlockSpec((seq, S5_SLAB), lambda b, s: (b, u_col0 + s)),
                  wblk(fir), wblk(bmat), wblk(cmat), wblk(pw)],
        out_specs=pl.BlockSpec((seq, S5_SLAB), lambda b, s: (b, s)),
        out_shape=jax.ShapeDtypeStruct((batch * seq, width), F32),
        scratch_shapes=[pltpu.VMEM((S5_SLAB // LANES, seq, LANES), F32),
                        pltpu.VMEM((S5_SLAB // LANES, seq, LANES), F32),
                        pltpu.VMEM((seq, S5_SLAB), BF16),
                        pltpu.VMEM((seq, n_state2), F32),
                        pltpu.VMEM((seq, n_state2), BF16),
                        pltpu.VMEM((seq, S5_SLAB), F32)],
        compiler_params=_compiler_params(("parallel", "arbitrary"), 48),
        name="s5_core",
    )(proj, fir, bmat, cmat, pw)


def _s5_matrices(lam_re, lam_im, log_dt, b_re, b_im, c_re, c_im, *, seq):
    t = S5_CHUNK
    groups = lam_re.shape[0]
    gs = S5_SLAB // S5_GROUP
    gh = gs // 2
    n_slabs = groups // gs
    dt = jnp.exp(log_dt)[:, None]
    mag = jnp.exp(lam_re * dt)
    lb_re, lb_im = mag * jnp.cos(lam_im * dt), mag * jnp.sin(lam_im * dt)
    den = lam_re * lam_re + lam_im * lam_im
    z_re = ((lb_re - 1.0) * lam_re + lb_im * lam_im) / den
    z_im = (lb_im * lam_re - (lb_re - 1.0) * lam_im) / den
    bb_re = z_re[..., None] * b_re - z_im[..., None] * b_im
    bb_im = z_re[..., None] * b_im + z_im[..., None] * b_re

    def mul(carry, _):
        pr, pi = carry
        return (pr * lb_re - pi * lb_im, pr * lb_im + pi * lb_re), (pr, pi)

    _, (pw_re, pw_im) = lax.scan(mul, (jnp.ones_like(lb_re), jnp.zeros_like(lb_re)), None, length=t + 1)

    cl_re = pw_re[:t, :, None, :] * c_re[None] - pw_im[:t, :, None, :] * c_im[None]
    cl_im = pw_re[:t, :, None, :] * c_im[None] + pw_im[:t, :, None, :] * c_re[None]
    taps = jnp.einsum('tghp,gpi->tghi', cl_re, bb_re) - jnp.einsum('tghp,gpi->tghi', cl_im, bb_im)
    fir = jnp.einsum('tsgoi,gk->stgiko', taps.reshape(t, n_slabs, gs, S5_GROUP, S5_GROUP),
                     jnp.eye(gs, dtype=F32)).reshape(n_slabs, t, S5_SLAB, S5_SLAB)

    eye_h = jnp.eye(gh, dtype=F32)
    eye_2 = jnp.eye(2, dtype=F32)

    def b_slab(bb):
        blk = jnp.einsum('abgph,gk->abghkp', bb.reshape(n_slabs, 2, gh, S5_STATE, S5_GROUP), eye_h)
        blk = blk.reshape(n_slabs, 2, gh * S5_GROUP, gh * S5_STATE)
        return (eye_2[None, :, :, None, None] * blk[:, :, None]).reshape(n_slabs, 2, S5_SLAB, gh * S5_STATE)

    def c_slab(cc):
        blk = jnp.einsum('abghp,gk->abgpkh', cc.reshape(n_slabs, 2, gh, S5_GROUP, S5_STATE), eye_h)
        blk = blk.reshape(n_slabs, 2, gh * S5_STATE, gh * S5_GROUP)
        return (eye_2[None, :, None, :, None] * blk[:, :, :, None]).reshape(n_slabs, 2, gh * S5_STATE, S5_SLAB)

    bmat = jnp.concatenate([b_slab(bb_re), b_slab(bb_im)], axis=-1)
    cmat = jnp.concatenate([c_slab(c_re), -c_slab(c_im)], axis=2)

    squares = [(pw_re[t], pw_im[t])]
    for _ in range((seq // t - 1).bit_length() - 1):
        sr, si = squares[-1]
        squares.append((sr * sr - si * si, 2.0 * sr * si))
    tab_re = jnp.concatenate([pw_re[1:], jnp.stack([s[0] for s in squares])], axis=0)
    tab_im = jnp.concatenate([pw_im[1:], jnp.stack([s[1] for s in squares])], axis=0)

    def state_lanes(x):
        return jnp.transpose(x.reshape(x.shape[0], n_slabs, 2, gh * S5_STATE), (1, 2, 0, 3))

    pw = jnp.concatenate([state_lanes(tab_re), state_lanes(tab_im)], axis=-1)
    return fir.astype(BF16), bmat.astype(BF16), cmat.astype(BF16), pw


def _mix_out_kernel(x_ref, ya_ref, ys_ref, u_ref, d_ref, gw_ref, gb_ref, wa_ref, wb_ref, o_ref):
    yb = _gelu_tanh(ys_ref[...] + d_ref[...] * u_ref[...])
    yb = yb * _sigmoid(_dot(yb.astype(BF16), gw_ref[...]) + gb_ref[...])
    o_ref[...] = (x_ref[...] + _dot(ya_ref[...].astype(BF16), wa_ref[...])
                  + _dot(yb.astype(BF16), wb_ref[...]))


def _mix_out(x, ya, ys, proj, d_skip, glu_w, glu_b, w_out_a, w_out_b, *, tm):
    m, d = x.shape
    width = ya.shape[1]
    u_col = proj.shape[1] // width - 1
    full = lambda arr: pl.BlockSpec(arr.shape, lambda i: (0, 0))
    row = lambda w, col=0: pl.BlockSpec((tm, w), lambda i: (i, col))
    return pl.pallas_call(
        _mix_out_kernel,
        grid=(m // tm,),
        in_specs=[row(d), row(width), row(width), row(width, u_col), full(d_skip), full(glu_w),
                  full(glu_b), full(w_out_a), full(w_out_b)],
        out_specs=row(d),
        out_shape=jax.ShapeDtypeStruct((m, d), F32),
        compiler_params=_compiler_params(("parallel",), 48),
        name="mix_out",
    )(x, ya, ys, proj, d_skip, glu_w, glu_b, w_out_a, w_out_b)


def _mlp_kernel(x_ref, g_ref, w1_ref, w2_ref, gn_ref, o_ref, *rest, norm_mode):
    if norm_mode == "next":
        hn_ref, h_ref = rest
    else:
        (h_ref,) = rest
    f = pl.program_id(1)

    @pl.when(f == 0)
    def _():
        x = x_ref[...]
        h_ref[...] = _rms(x, g_ref[...]).astype(BF16)
        o_ref[...] = x

    hid = jnp.maximum(_dot(h_ref[...], w1_ref[...]), 0.0)
    o_ref[...] += _dot((hid * hid).astype(BF16), w2_ref[...])

    @pl.when(f == pl.num_programs(1) - 1)
    def _():
        normed = _rms(o_ref[...], gn_ref[...])
        if norm_mode == "next":
            hn_ref[...] = normed.astype(BF16)
        else:
            o_ref[...] = normed


def _mlp(x, g, w1, w2, gn, *, norm_mode, tm, tf):
    assert norm_mode in ("final", "next")
    m, d = x.shape
    dff = w1.shape[1]
    row = pl.BlockSpec((tm, d), lambda i, f: (i, 0))
    out_specs, out_shape = [row], [jax.ShapeDtypeStruct((m, d), F32)]
    if norm_mode == "next":
        out_specs.append(row)
        out_shape.append(jax.ShapeDtypeStruct((m, d), BF16))
    return pl.pallas_call(
        functools.partial(_mlp_kernel, norm_mode=norm_mode),
        grid=(m // tm, dff // tf),
        in_specs=[row,
                  pl.BlockSpec((1, d), lambda i, f: (0, 0)),
                  pl.BlockSpec((d, tf), lambda i, f: (0, f)),
                  pl.BlockSpec((tf, d), lambda i, f: (f, 0)),
                  pl.BlockSpec((1, d), lambda i, f: (0, 0))],
        out_specs=out_specs,
        out_shape=out_shape,
        scratch_shapes=[pltpu.VMEM((tm, d), BF16)],
        compiler_params=_compiler_params(("parallel", "arbitrary"), 48),
        name="mlp",
    )(x, g, w1, w2, gn)


def _rglru_kernel(gate_ref, xr_ref, x_ref, cw_ref, cb_ref, wa_ref, wx_ref, ba_ref, bx_ref, lam_ref,
                  wout_ref, o_ref, tail_ref, h_ref):
    tt, width = xr_ref.shape
    n_heads, blk = wa_ref.shape[0], wa_ref.shape[1]
    c = pl.program_id(1)

    @pl.when(c == 0)
    def _():
        tail_ref[...] = jnp.zeros_like(tail_ref)
        h_ref[...] = jnp.zeros_like(h_ref)

    xr = xr_ref[...]
    tail = tail_ref[...]
    row8 = lax.broadcasted_iota(jnp.int32, (SUBLANES, width), 0)
    xc = xr * cw_ref[CONV_WIDTH - 1:CONV_WIDTH, :] + cb_ref[...]
    for s in range(1, CONV_WIDTH):
        xs = pltpu.roll(xr, s, axis=0)
        head = jnp.where(row8 < s, pltpu.roll(tail, s, axis=0), xs[:SUBLANES])
        xs = jnp.concatenate([head, xs[SUBLANES:]], axis=0)
        xc = xc + xs * cw_ref[CONV_WIDTH - 1 - s:CONV_WIDTH - s, :]
    tail_ref[...] = xr[tt - SUBLANES:tt, :]

    xcb = xc.astype(BF16)
    pre_a = jnp.concatenate(
        [_dot(xcb[:, h * blk:(h + 1) * blk], wa_ref[h]) for h in range(n_heads)], axis=1)
    pre_x = jnp.concatenate(
        [_dot(xcb[:, h * blk:(h + 1) * blk], wx_ref[h]) for h in range(n_heads)], axis=1)
    gate_r = _sigmoid(pre_a + ba_ref[...])
    gate_i = _sigmoid(pre_x + bx_ref[...])
    log_a = LRU_C * gate_r * (-_softplus(-lam_ref[...]))
    a = jnp.exp(log_a)
    mult = jnp.sqrt(1.0 - a * a)
    row = lax.broadcasted_iota(jnp.int32, (tt, width), 0)
    mult = jnp.where((row == 0) & (c == 0), 1.0, mult)
    b = xc * gate_i * mult

    n_groups = tt // SUBLANES
    a3 = a.reshape(n_groups, SUBLANES, width)
    b3 = b.reshape(n_groups, SUBLANES, width)
    sub = lax.broadcasted_iota(jnp.int32, a3.shape, 1)
    shift = 1
    while shift < SUBLANES:
        valid = sub >= shift
        b3 = b3 + a3 * jnp.where(valid, pltpu.roll(b3, shift, axis=1), 0.0)
        a3 = a3 * jnp.where(valid, pltpu.roll(a3, shift, axis=1), 1.0)
        shift *= 2
    carry = h_ref[0:1, :]
    groups = []
    for i in range(n_groups):
        h_i = b3[i] + a3[i] * carry
        groups.append(h_i)
        carry = h_i[SUBLANES - 1:SUBLANES, :]
    h_ref[...] = jnp.broadcast_to(carry, (SUBLANES, width))
    y = _gelu_tanh(gate_ref[...]) * jnp.concatenate(groups, axis=0)
    o_ref[...] = x_ref[...] + _dot(y.astype(BF16), wout_ref[...])


def _rglru(proj, x, conv_w, conv_b, w_a, w_x, b_a, b_x, lam, w_out, *, batch, seq, tt):
    width = lam.shape[1]
    d = x.shape[1]
    nc = seq // tt
    row = lambda w, col: pl.BlockSpec((tt, w), lambda b, c: (b * nc + c, col))
    par = lambda arr: pl.BlockSpec(arr.shape, lambda b, c: (0,) * arr.ndim, pipeline_mode=pl.Buffered(1))
    return pl.pallas_call(
        _rglru_kernel,
        grid=(batch, nc),
        in_specs=[row(width, 0), row(width, 1), row(d, 0), par(conv_w), par(conv_b), par(w_a), par(w_x),
                  par(b_a), par(b_x), par(lam), par(w_out)],
        out_specs=row(d, 0),
        out_shape=jax.ShapeDtypeStruct((batch * seq, d), F32),
        scratch_shapes=[pltpu.VMEM((SUBLANES, width), F32), pltpu.VMEM((SUBLANES, width), F32)],
        compiler_params=_compiler_params(("parallel", "arbitrary"), 56),
        name="rglru",
    )(proj, proj, x, conv_w, conv_b, w_a, w_x, b_a, b_x, lam, w_out)


def _pad_cols(w, n):
    return jnp.pad(w, ((0, 0), (0, n - w.shape[1])))


def _pad_rows(w, n):
    return jnp.pad(w, ((0, n - w.shape[0]), (0, 0)))


def kernel(x, norm_mix_g, norm_mlp_g, norm_f_g, mlp_w1, mlp_w2, hy_w_in, hy_w_out, rw_mu_rkv, rw_mu_wag, rw_w0, rw_w1, rw_w2, rw_a0, rw_a1, rw_a2, rw_g1, rw_g2, rw_k_k, rw_k_a, rw_r_k, rw_ln_w, rw_ln_b, s5_lam_re, s5_lam_im, s5_log_dt, s5_b_re, s5_b_im, s5_c_re, s5_c_im, s5_d, s5_glu_w, s5_glu_b, rg_w_in, rg_w_out, rg_conv_w, rg_conv_b, rg_w_a, rg_b_a, rg_w_x, rg_b_x, rg_lam):
    batch, seq, d = x.shape
    m = batch * seq
    rw = rw_w2.shape[-1]
    row = lambda v: v.reshape(1, -1)
    bf = lambda w: w.astype(BF16)
    lane = 128
    xf = x.reshape(m, d)

    lw, a, gate, proj = _rwkv_lora(
        xf, row(norm_mix_g[0]), rw_mu_wag[0],
        bf(_pad_cols(rw_w1[0], lane)), bf(_pad_cols(rw_a1[0], lane)), bf(rw_g1[0]),
        bf(_pad_rows(rw_w2[0], lane)), bf(_pad_rows(rw_a2[0], lane)), bf(rw_g2[0]),
        row(rw_w0[0]), row(rw_a0[0]), bf(hy_w_in[0]), seq=seq, tm=256)
    ya = _rwkv_core(proj, lw, a, gate, rw_mu_rkv[0], row(rw_k_k[0]), row(rw_k_a[0]),
                    row(rw_r_k[0]), row(rw_ln_w[0]), row(rw_ln_b[0]), batch=batch, seq=seq)

    mats = _s5_matrices(s5_lam_re[0], s5_lam_im[0], s5_log_dt[0], s5_b_re[0], s5_b_im[0],
                        s5_c_re[0], s5_c_im[0], seq=seq)
    ys = _s5_core(proj, *mats, batch=batch, seq=seq, width=s5_d.shape[-1])

    x1 = _mix_out(xf, ya, ys, proj, row(s5_d[0]), bf(s5_glu_w[0]), row(s5_glu_b[0]),
                  bf(hy_w_out[0][:rw]), bf(hy_w_out[0][rw:]), tm=256)
    x2, hn1 = _mlp(x1, row(norm_mlp_g[0]), bf(mlp_w1[0]), bf(mlp_w2[0]), row(norm_mix_g[1]),
                   norm_mode="next", tm=512, tf=1024)

    proj1 = _matmul(hn1, bf(rg_w_in[0]), tm=1024, tn=1024)
    x3 = _rglru(proj1, x2, rg_conv_w[0], row(rg_conv_b[0]), bf(rg_w_a[0]), bf(rg_w_x[0]),
                row(rg_b_a[0]), row(rg_b_x[0]), row(rg_lam[0]), bf(rg_w_out[0]),
                batch=batch, seq=seq, tt=256)
    (out,) = _mlp(x3, row(norm_mlp_g[1]), bf(mlp_w1[1]), bf(mlp_w2[1]), row(norm_f_g),
                  norm_mode="final", tm=512, tf=1024)
    return out.reshape(batch, seq, d)
```

```python
import functools
import math

import jax
import jax.numpy as jnp
from jax import lax
from jax.experimental import pallas as pl
from jax.experimental.pallas import tpu as pltpu

F32 = jnp.float32
BF16 = jnp.bfloat16

NORM_EPS = 1e-6
RWKV_HEAD = 64
RWKV_GN_EPS = 64e-5
RWKV_CHUNK = 64
HEAD_PAIR = 2 * RWKV_HEAD
S5_GROUP = 16
S5_STATE = 64
S5_CHUNK = 16
S5_SLAB = 256
S5_PART_GROUPS = 4
LRU_C = 8.0
CONV_WIDTH = 4
SUBLANES = 8
LANES = 128
V7X_VMEM_BYTES = 64 * 1024 * 1024


def _compiler_params(semantics, vmem_mib):
    assert vmem_mib * 1024 * 1024 < V7X_VMEM_BYTES
    return pltpu.CompilerParams(dimension_semantics=semantics,
                                vmem_limit_bytes=vmem_mib * 1024 * 1024)


def _rms(x, g):
    return x * lax.rsqrt(jnp.mean(x * x, axis=-1, keepdims=True) + NORM_EPS) * g


def _gelu_tanh(x):
    c = math.sqrt(2.0 / math.pi)
    return 0.5 * x * (1.0 + jnp.tanh(c * (x + 0.044715 * (x * x * x))))


def _sigmoid(x):
    return 1.0 / (1.0 + jnp.exp(-x))


def _softplus(x):
    return jnp.maximum(x, 0.0) + jnp.log1p(jnp.exp(-jnp.abs(x)))


def _dot(a, b):
    return jnp.dot(a, b, preferred_element_type=F32)


def _dot_nt(a, b):
    return lax.dot_general(a, b, (((1,), (1,)), ((), ())), preferred_element_type=F32)


def _dot_tn(a, b):
    return lax.dot_general(a, b, (((0,), (0,)), ((), ())), preferred_element_type=F32)


def _shift_rows(x, prev_row):
    row = lax.broadcasted_iota(jnp.int32, x.shape, 0)
    return jnp.where(row == 0, prev_row, pltpu.roll(x, 1, axis=0))


def _matmul_kernel(h_ref, w_ref, o_ref):
    o_ref[...] = _dot(h_ref[...], w_ref[...])


def _matmul(h, w, *, tm, tn):
    m, d = h.shape
    n = w.shape[1]
    return pl.pallas_call(
        _matmul_kernel,
        grid=(m // tm, n // tn),
        in_specs=[pl.BlockSpec((tm, d), lambda i, j: (i, 0)),
                  pl.BlockSpec((d, tn), lambda i, j: (0, j))],
        out_specs=pl.BlockSpec((tm, tn), lambda i, j: (i, j)),
        out_shape=jax.ShapeDtypeStruct((m, n), F32),
        compiler_params=_compiler_params(("parallel", "arbitrary"), 40),
        name="in_proj",
    )(h, w)


def _lora_kernel(x_ref, xp_ref, g_ref, mu_ref, w1_ref, a1_ref, g1_ref, w2_ref, a2_ref, g2_ref,
                 w0_ref, a0_ref, win_ref, lw_ref, a_ref, gate_ref, proj_ref, *, blocks_per_seq):
    g = g_ref[...]
    h = _rms(x_ref[...], g)
    proj_ref[...] = _dot(h.astype(BF16), win_ref[...])
    hp = _rms(xp_ref[...], g)
    first = (pl.program_id(0) % blocks_per_seq) == 0
    prev_row = jnp.where(first, 0.0, hp[SUBLANES - 1:SUBLANES, :])
    dh = _shift_rows(h, prev_row) - h
    xw = (h + dh * mu_ref[0:1, :]).astype(BF16)
    xa = (h + dh * mu_ref[1:2, :]).astype(BF16)
    xg = (h + dh * mu_ref[2:3, :]).astype(BF16)
    tw = jnp.tanh(_dot(xw, w1_ref[...])).astype(BF16)
    w_pre = w0_ref[...] + _dot(tw, w2_ref[...])
    w_log = -_softplus(-w_pre) - 0.5
    lw_ref[...] = -jnp.exp(w_log)
    ta = _dot(xa, a1_ref[...]).astype(BF16)
    a_ref[...] = _sigmoid(a0_ref[...] + _dot(ta, a2_ref[...]))
    tg = _sigmoid(_dot(xg, g1_ref[...])).astype(BF16)
    gate_ref[...] = _dot(tg, g2_ref[...])


def _rwkv_lora(x, g, mu_wag, w1, a1, g1, w2, a2, g2, w0, a0, w_in, *, seq, tm):
    m, d = x.shape
    width = w2.shape[1]
    n_proj = w_in.shape[1]
    blocks_per_seq = seq // tm
    full = lambda arr: pl.BlockSpec(arr.shape, lambda i: (0, 0), pipeline_mode=pl.Buffered(1))
    out = jax.ShapeDtypeStruct((m, width), F32)
    return pl.pallas_call(
        functools.partial(_lora_kernel, blocks_per_seq=blocks_per_seq),
        grid=(m // tm,),
        in_specs=[pl.BlockSpec((tm, d), lambda i: (i, 0)),
                  pl.BlockSpec((SUBLANES, d), lambda i: (jnp.maximum(i * (tm // SUBLANES) - 1, 0), 0)),
                  full(g), full(mu_wag), full(w1), full(a1), full(g1), full(w2), full(a2), full(g2),
                  full(w0), full(a0), full(w_in)],
        out_specs=[pl.BlockSpec((tm, width), lambda i: (i, 0))] * 3 + [pl.BlockSpec((tm, n_proj), lambda i: (i, 0))],
        out_shape=[out, out, out, jax.ShapeDtypeStruct((m, n_proj), F32)],
        compiler_params=_compiler_params(("parallel",), 56),
        name="rwkv_lora",
    )(x, x, g, mu_wag, w1, a1, g1, w2, a2, g2, w0, a0, w_in)


def _split3(x):
    hi = x.astype(BF16)
    r1 = x - hi.astype(F32)
    mid = r1.astype(BF16)
    lo = (r1 - mid.astype(F32)).astype(BF16)
    return hi, mid, lo


def _split2(x):
    hi = x.astype(BF16)
    lo = (x - hi.astype(F32)).astype(BF16)
    return hi, lo


def _rwkv_kernel(pr_ref, pk_ref, pv_ref, lw_ref, a_ref, gate_ref, mu_ref, kk_ref, ka_ref, rk_ref,
                 lnw_ref, lnb_ref, o_ref, prev_ref, state_ref):
    t = RWKV_CHUNK
    n_pairs = o_ref.shape[1] // HEAD_PAIR

    @pl.when(pl.program_id(1) == 0)
    def _():
        prev_ref[...] = jnp.zeros_like(prev_ref)
        state_ref[...] = jnp.zeros_like(state_ref)

    def shift_mix(p_ref, idx):
        p = p_ref[...]
        ps = _shift_rows(p, prev_ref[idx, SUBLANES - 1:SUBLANES, :])
        prev_ref[idx] = p[t - SUBLANES:t, :]
        return p + (ps - p) * mu_ref[idx:idx + 1, :]

    r_all = shift_mix(pr_ref, 0)
    k_all = shift_mix(pk_ref, 1)
    v_all = shift_mix(pv_ref, 2)
    a_all = a_ref[...]
    lw_all = lw_ref[...]

    ri = lax.broadcasted_iota(jnp.int32, (t, t), 0)
    ci = lax.broadcasted_iota(jnp.int32, (t, t), 1)
    tril = jnp.where(ci <= ri, 1.0, 0.0).astype(BF16)
    cum_all = _dot(jnp.concatenate([tril, tril, tril], axis=1),
                   jnp.concatenate(_split3(lw_all), axis=0))

    si = lax.broadcasted_iota(jnp.int32, (HEAD_PAIR, HEAD_PAIR), 0) // RWKV_HEAD
    sj = lax.broadcasted_iota(jnp.int32, (HEAD_PAIR, HEAD_PAIR), 1) // RWKV_HEAD
    seg = jnp.where(si == sj, 1.0, 0.0).astype(BF16)

    width = o_ref.shape[1]
    pairs = range(n_pairs)
    slab = lambda x, p: x[:, p * HEAD_PAIR:(p + 1) * HEAD_PAIR]
    lanes = lambda xs: jnp.concatenate(xs, axis=1)

    seg2 = jnp.concatenate([seg, seg], axis=0)

    def head_sum_all(x):
        xh, xl = _split2(x)
        return lanes([_dot(jnp.concatenate([slab(xh, p), slab(xl, p)], axis=1), seg2) for p in pairs])

    head0 = (lax.broadcasted_iota(jnp.int32, (t, width), 1) % HEAD_PAIR) < RWKV_HEAD

    def expand(x):
        xb = x.astype(BF16)
        zero = jnp.zeros_like(xb)
        return jnp.concatenate([jnp.where(head0, xb, zero), jnp.where(head0, zero, xb)], axis=0)

    bi = lax.broadcasted_iota(jnp.int32, (2 * t, 2 * t), 0) % t
    bj = lax.broadcasted_iota(jnp.int32, (2 * t, 2 * t), 1) % t
    strict = bj < bi
    incl = bj <= bi
    eye = jnp.where(lax.broadcasted_iota(jnp.int32, (2 * t, 2 * t), 0)
                    == lax.broadcasted_iota(jnp.int32, (2 * t, 2 * t), 1), 1.0, 0.0)

    kk = k_all * kk_ref[...]
    kk = kk / jnp.maximum(jnp.sqrt(head_sum_all(kk * kk)), 1e-12)
    k_all = k_all * (1.0 + (a_all - 1.0) * ka_ref[...])
    b_all = kk * a_all
    cum_t = cum_all[t - 1:t, :]
    p_inv = jnp.exp(-cum_all)
    d_rest = jnp.exp(cum_t - cum_all)
    ar = jnp.concatenate([expand(kk * jnp.exp(cum_all - lw_all)), expand(r_all * jnp.exp(cum_all))], axis=0)
    kb = jnp.concatenate([expand(k_all * p_inv), expand(b_all * p_inv)], axis=0)
    kbt = jnp.concatenate([expand(k_all * d_rest), expand(-(b_all * d_rest))], axis=0)
    v_e = expand(v_all)
    decay_t = jnp.exp(cum_t)

    states = [state_ref[p] for p in pairs]
    scores = [_dot_nt(slab(ar, p), slab(kb, p)) for p in pairs]
    ar_h = [_dot_nt(slab(ar, p), states[p].astype(BF16)) for p in pairs]
    n_ab = [jnp.where(strict, s[:2 * t, 2 * t:], 0.0) for s in scores]
    a_ak = [jnp.where(strict, s[:2 * t, :2 * t], 0.0).astype(BF16) for s in scores]
    a_r = [jnp.concatenate([jnp.where(incl, s[2 * t:, :2 * t], 0.0),
                            jnp.where(incl, -s[2 * t:, 2 * t:], 0.0)], axis=1).astype(BF16)
           for s in scores]
    rhs = [(ar_h[p][:2 * t] + _dot(a_ak[p], slab(v_e, p))).astype(BF16) for p in pairs]

    inv = [eye - n for n in n_ab]
    n_pow = [n.astype(BF16) for n in n_ab]
    n_pow = [_dot(n, n).astype(BF16) for n in n_pow]
    n_levels = int(math.log2(t)) - 1
    for level in range(n_levels):
        nxt = [_dot(n, n).astype(BF16) for n in n_pow] if level + 1 < n_levels else None
        inv = [inv[p] + _dot(inv[p].astype(BF16), n_pow[p]) for p in pairs]
        n_pow = nxt

    u_e = [_dot(inv[p].astype(BF16), rhs[p]).astype(BF16) for p in pairs]
    vu = [jnp.concatenate([slab(v_e, p), u_e[p]], axis=0) for p in pairs]
    o_e = [ar_h[p][2 * t:] + _dot(a_r[p], vu[p]) for p in pairs]
    for p in pairs:
        state_ref[p] = states[p] * slab(decay_t, p) + _dot_tn(vu[p], slab(kbt, p))
    o = lanes([oe[:t] + oe[t:] for oe in o_e])

    inv_n = 1.0 / RWKV_HEAD
    mean = head_sum_all(o) * inv_n
    d = o - mean
    var = head_sum_all(d * d) * inv_n
    y = d * lax.rsqrt(var + RWKV_GN_EPS) * lnw_ref[...] + lnb_ref[...]
    y = y + head_sum_all(r_all * k_all * rk_ref[...]) * v_all
    o_ref[...] = y * gate_ref[...]


def _rwkv_core(proj, lw, a, gate, mu_rkv, k_k, k_a, r_k, ln_w, ln_b, *, batch, seq):
    width = lw.shape[1]
    t = RWKV_CHUNK
    nc = seq // t
    row = lambda col: pl.BlockSpec((t, width), lambda b, c: (b * nc + c, col))
    par = lambda arr: pl.BlockSpec(arr.shape, lambda b, c: (0, 0))
    return pl.pallas_call(
        _rwkv_kernel,
        grid=(batch, nc),
        in_specs=[row(0), row(1), row(2), row(0), row(0), row(0),
                  par(mu_rkv), par(k_k), par(k_a), par(r_k), par(ln_w), par(ln_b)],
        out_specs=row(0),
        out_shape=jax.ShapeDtypeStruct((batch * seq, width), F32),
        scratch_shapes=[pltpu.VMEM((3, SUBLANES, width), F32),
                        pltpu.VMEM((width // HEAD_PAIR, HEAD_PAIR, HEAD_PAIR), F32)],
        compiler_params=_compiler_params(("parallel", "arbitrary"), 32),
        name="rwkv_core",
    )(proj, proj, proj, lw, a, gate, mu_rkv, k_k, k_a, r_k, ln_w, ln_b)


def _s5_kernel(u_ref, fir_ref, bmat_ref, cmat_ref, pw_ref, y_ref,
               nat_ref, mid_ref, ud_ref, bu_ref, sr_ref, yd_ref):
    seq = u_ref.shape[0]
    t = S5_CHUNK
    nc = seq // t
    quarter = seq // 4
    n_lane_halves = u_ref.shape[1] // LANES
    n_state = bmat_ref.shape[2] // 2

    for h in range(n_lane_halves):
        lanes = slice(h * LANES, (h + 1) * LANES)
        nat_ref[h] = u_ref[:, lanes]
        for q in range(4):
            mid_ref[h, q * quarter:(q + 1) * quarter, :] = nat_ref[h, pl.ds(q, quarter, stride=4), :]
        for q in range(4):
            for r in range(4):
                pos = 4 * r + q
                ud_ref[pos * nc:(pos + 1) * nc, lanes] = (
                    mid_ref[h, pl.ds(q * quarter + r, nc, stride=4), :].astype(BF16))

    yd_ref[...] = _dot(ud_ref[...], fir_ref[0])
    for tau in range(1, t):
        yd_ref[tau * nc:, :] += _dot(ud_ref[0:(t - tau) * nc, :], fir_ref[tau])

    n_parts = bmat_ref.shape[0]
    for part in range(n_parts):
        bu_ref[part] = _dot(ud_ref[...], bmat_ref[part])
    row = lax.broadcasted_iota(jnp.int32, (nc, LANES), 0)
    cmul = lambda x, y: (x[0] * y[0] - x[1] * y[1], x[0] * y[1] + x[1] * y[0])
    for part in range(n_parts):
        for lc in range(n_state // LANES):
            re = slice(lc * LANES, (lc + 1) * LANES)
            im = slice(n_state + lc * LANES, n_state + (lc + 1) * LANES)
            pw = lambda k: (pw_ref[part, k:k + 1, re], pw_ref[part, k:k + 1, im])
            lam = pw(0)
            z = (bu_ref[part, 0:nc, re], bu_ref[part, 0:nc, im])
            for pos in range(1, t):
                rows = slice(pos * nc, (pos + 1) * nc)
                lz = cmul(lam, z)
                z = (lz[0] + bu_ref[part, rows, re], lz[1] + bu_ref[part, rows, im])
            shift, level = 1, 0
            while shift < nc:
                valid = row >= shift
                prev = (jnp.where(valid, pltpu.roll(z[0], shift, axis=0), 0.0),
                        jnp.where(valid, pltpu.roll(z[1], shift, axis=0), 0.0))
                step = cmul(pw(t + level), prev)
                z = (z[0] + step[0], z[1] + step[1])
                shift *= 2
                level += 1
            first = row >= 1
            s_in = (jnp.where(first, pltpu.roll(z[0], 1, axis=0), 0.0),
                    jnp.where(first, pltpu.roll(z[1], 1, axis=0), 0.0))
            for pos in range(t):
                rows = slice(pos * nc, (pos + 1) * nc)
                rot = cmul(pw(pos), s_in)
                sr_ref[part, rows, re] = rot[0].astype(BF16)
                sr_ref[part, rows, im] = rot[1].astype(BF16)
        yd_ref[...] += _dot(sr_ref[part], cmat_ref[part])

    for h in range(n_lane_halves):
        lanes = slice(h * LANES, (h + 1) * LANES)
        for q in range(4):
            for r in range(4):
                pos = 4 * r + q
                mid_ref[h, pl.ds(q * quarter + r, nc, stride=4), :] = yd_ref[pos * nc:(pos + 1) * nc, lanes]
        for q in range(4):
            nat_ref[h, pl.ds(q, quarter, stride=4), :] = mid_ref[h, q * quarter:(q + 1) * quarter, :]
        y_ref[:, lanes] = nat_ref[h]


def _s5_core(proj, fir, bmat, cmat, pw, *, batch, seq, width):
    n_slabs = width // S5_SLAB
    u_col0 = (proj.shape[1] - width) // S5_SLAB
    wblk = lambda arr: pl.BlockSpec((None,) + arr.shape[1:], lambda b, s: (s,) + (0,) * (arr.ndim - 1))
    n_parts, n_state2 = bmat.shape[1], bmat.shape[-1]
    return pl.pallas_call(
        _s5_kernel,
        grid=(batch, n_slabs),
        in_specs=[pl.BlockSpec((seq, S5_SLAB), lambda b, s: (b, u_col0 + s)),
                  wblk(fir), wblk(bmat), wblk(cmat), wblk(pw)],
        out_specs=pl.BlockSpec((seq, S5_SLAB), lambda b, s: (b, s)),
        out_shape=jax.ShapeDtypeStruct((batch * seq, width), F32),
        scratch_shapes=[pltpu.VMEM((S5_SLAB // LANES, seq, LANES), F32),
                        pltpu.VMEM((S5_SLAB // LANES, seq, LANES), F32),
                        pltpu.VMEM((seq, S5_SLAB), BF16),
                        pltpu.VMEM((n_parts, seq, n_state2), F32),
                        pltpu.VMEM((n_parts, seq, n_state2), BF16),
                        pltpu.VMEM((seq, S5_SLAB), F32)],
        compiler_params=_compiler_params(("parallel", "arbitrary"), 56),
        name="s5_core",
    )(proj, fir, bmat, cmat, pw)


def _s5_matrices(lam_re, lam_im, log_dt, b_re, b_im, c_re, c_im, *, seq):
    t = S5_CHUNK
    groups = lam_re.shape[0]
    gs = S5_SLAB // S5_GROUP
    gp = S5_PART_GROUPS
    n_parts = gs // gp
    n_slabs = groups // gs
    dt = jnp.exp(log_dt)[:, None]
    mag = jnp.exp(lam_re * dt)
    lb_re, lb_im = mag * jnp.cos(lam_im * dt), mag * jnp.sin(lam_im * dt)
    den = lam_re * lam_re + lam_im * lam_im
    z_re = ((lb_re - 1.0) * lam_re + lb_im * lam_im) / den
    z_im = (lb_im * lam_re - (lb_re - 1.0) * lam_im) / den
    bb_re = z_re[..., None] * b_re - z_im[..., None] * b_im
    bb_im = z_re[..., None] * b_im + z_im[..., None] * b_re

    def mul(carry, _):
        pr, pi = carry
        return (pr * lb_re - pi * lb_im, pr * lb_im + pi * lb_re), (pr, pi)

    _, (pw_re, pw_im) = lax.scan(mul, (jnp.ones_like(lb_re), jnp.zeros_like(lb_re)), None, length=t + 1)

    cl_re = pw_re[:t, :, None, :] * c_re[None] - pw_im[:t, :, None, :] * c_im[None]
    cl_im = pw_re[:t, :, None, :] * c_im[None] + pw_im[:t, :, None, :] * c_re[None]
    taps = jnp.einsum('tghp,gpi->tghi', cl_re, bb_re) - jnp.einsum('tghp,gpi->tghi', cl_im, bb_im)
    fir = jnp.einsum('tsgoi,gk->stgiko', taps.reshape(t, n_slabs, gs, S5_GROUP, S5_GROUP),
                     jnp.eye(gs, dtype=F32)).reshape(n_slabs, t, S5_SLAB, S5_SLAB)

    eye_g = jnp.eye(gp, dtype=F32)
    eye_p = jnp.eye(n_parts, dtype=F32)

    def b_slab(bb):
        blk = jnp.einsum('abgph,gk->abghkp', bb.reshape(n_slabs, n_parts, gp, S5_STATE, S5_GROUP), eye_g)
        blk = blk.reshape(n_slabs, n_parts, gp * S5_GROUP, gp * S5_STATE)
        return (eye_p[None, :, :, None, None] * blk[:, :, None]).reshape(n_slabs, n_parts, S5_SLAB, gp * S5_STATE)

    def c_slab(cc):
        blk = jnp.einsum('abghp,gk->abgpkh', cc.reshape(n_slabs, n_parts, gp, S5_GROUP, S5_STATE), eye_g)
        blk = blk.reshape(n_slabs, n_parts, gp * S5_STATE, gp * S5_GROUP)
        return (eye_p[None, :, None, :, None] * blk[:, :, :, None]).reshape(n_slabs, n_parts, gp * S5_STATE, S5_SLAB)

    bmat = jnp.concatenate([b_slab(bb_re), b_slab(bb_im)], axis=-1)
    cmat = jnp.concatenate([c_slab(c_re), -c_slab(c_im)], axis=2)

    squares = [(pw_re[t], pw_im[t])]
    for _ in range((seq // t - 1).bit_length() - 1):
        sr, si = squares[-1]
        squares.append((sr * sr - si * si, 2.0 * sr * si))
    tab_re = jnp.concatenate([pw_re[1:], jnp.stack([s[0] for s in squares])], axis=0)
    tab_im = jnp.concatenate([pw_im[1:], jnp.stack([s[1] for s in squares])], axis=0)

    def state_lanes(x):
        return jnp.transpose(x.reshape(x.shape[0], n_slabs, n_parts, gp * S5_STATE), (1, 2, 0, 3))

    pw = jnp.concatenate([state_lanes(tab_re), state_lanes(tab_im)], axis=-1)
    return fir.astype(BF16), bmat.astype(BF16), cmat.astype(BF16), pw


def _mix_out_kernel(x_ref, ya_ref, ys_ref, u_ref, d_ref, gw_ref, gb_ref, wa_ref, wb_ref, o_ref):
    yb = _gelu_tanh(ys_ref[...] + d_ref[...] * u_ref[...])
    yb = yb * _sigmoid(_dot(yb.astype(BF16), gw_ref[...]) + gb_ref[...])
    o_ref[...] = (x_ref[...] + _dot(ya_ref[...].astype(BF16), wa_ref[...])
                  + _dot(yb.astype(BF16), wb_ref[...]))


def _mix_out(x, ya, ys, proj, d_skip, glu_w, glu_b, w_out_a, w_out_b, *, tm):
    m, d = x.shape
    width = ya.shape[1]
    u_col = proj.shape[1] // width - 1
    full = lambda arr: pl.BlockSpec(arr.shape, lambda i: (0, 0))
    row = lambda w, col=0: pl.BlockSpec((tm, w), lambda i: (i, col))
    return pl.pallas_call(
        _mix_out_kernel,
        grid=(m // tm,),
        in_specs=[row(d), row(width), row(width), row(width, u_col), full(d_skip), full(glu_w),
                  full(glu_b), full(w_out_a), full(w_out_b)],
        out_specs=row(d),
        out_shape=jax.ShapeDtypeStruct((m, d), F32),
        compiler_params=_compiler_params(("parallel",), 48),
        name="mix_out",
    )(x, ya, ys, proj, d_skip, glu_w, glu_b, w_out_a, w_out_b)


def _mlp_kernel(x_ref, g_ref, w1_ref, w2_ref, gn_ref, o_ref, *rest, norm_mode):
    if norm_mode == "next":
        hn_ref, h_ref = rest
    else:
        (h_ref,) = rest
    f = pl.program_id(1)

    @pl.when(f == 0)
    def _():
        x = x_ref[...]
        h_ref[...] = _rms(x, g_ref[...]).astype(BF16)
        o_ref[...] = x

    hid = jnp.maximum(_dot(h_ref[...], w1_ref[...]), 0.0)
    o_ref[...] += _dot((hid * hid).astype(BF16), w2_ref[...])

    @pl.when(f == pl.num_programs(1) - 1)
    def _():
        normed = _rms(o_ref[...], gn_ref[...])
        if norm_mode == "next":
            hn_ref[...] = normed.astype(BF16)
        else:
            o_ref[...] = normed


def _mlp(x, g, w1, w2, gn, *, norm_mode, tm, tf):
    assert norm_mode in ("final", "next")
    m, d = x.shape
    dff = w1.shape[1]
    row = pl.BlockSpec((tm, d), lambda i, f: (i, 0))
    out_specs, out_shape = [row], [jax.ShapeDtypeStruct((m, d), F32)]
    if norm_mode == "next":
        out_specs.append(row)
        out_shape.append(jax.ShapeDtypeStruct((m, d), BF16))
    return pl.pallas_call(
        functools.partial(_mlp_kernel, norm_mode=norm_mode),
        grid=(m // tm, dff // tf),
        in_specs=[row,
                  pl.BlockSpec((1, d), lambda i, f: (0, 0)),
                  pl.BlockSpec((d, tf), lambda i, f: (0, f)),
                  pl.BlockSpec((tf, d), lambda i, f: (f, 0)),
                  pl.BlockSpec((1, d), lambda i, f: (0, 0))],
        out_specs=out_specs,
        out_shape=out_shape,
        scratch_shapes=[pltpu.VMEM((tm, d), BF16)],
        compiler_params=_compiler_params(("parallel", "arbitrary"), 48),
        name="mlp",
    )(x, g, w1, w2, gn)


def _rglru_kernel(gate_ref, xr_ref, x_ref, cw_ref, cb_ref, wa_ref, wx_ref, ba_ref, bx_ref, lam_ref,
                  wout_ref, o_ref, tail_ref, h_ref, y_ref):
    c = pl.program_id(1)
    n_blocks = pl.num_programs(1) - 1
    n_heads, blk = wa_ref.shape[0], wa_ref.shape[1]
    old, new = (c + 1) % 2, c % 2

    @pl.when(c == 0)
    def _():
        tail_ref[...] = jnp.zeros_like(tail_ref)
        h_ref[...] = jnp.zeros_like(h_ref)
        y_ref[...] = jnp.zeros_like(y_ref)

    def project(h):
        cols = slice(h * blk, (h + 1) * blk)
        o_ref[:, cols] = x_ref[:, cols] + _dot(y_ref[old], wout_ref[:, cols])

    @pl.when(c < n_blocks)
    def _():
        for h in range(n_heads):
            project(h)
            cols = slice(h * blk, (h + 1) * blk)
            y_ref[new, :, cols] = _rglru_head(
                gate_ref, xr_ref, cw_ref, cb_ref, wa_ref, wx_ref, ba_ref, bx_ref, lam_ref,
                tail_ref, h_ref, h, c == 0).astype(BF16)

    @pl.when(c == n_blocks)
    def _():
        for h in range(n_heads):
            project(h)


def _rglru_head(gate_ref, xr_ref, cw_ref, cb_ref, wa_ref, wx_ref, ba_ref, bx_ref, lam_ref,
                tail_ref, h_ref, head, is_first):
    tt = xr_ref.shape[0]
    blk = wa_ref.shape[1]
    cols = slice(head * blk, (head + 1) * blk)

    xr = xr_ref[:, cols]
    tail = tail_ref[:, cols]
    row8 = lax.broadcasted_iota(jnp.int32, (SUBLANES, blk), 0)
    xc = xr * cw_ref[CONV_WIDTH - 1:CONV_WIDTH, cols] + cb_ref[:, cols]
    for s in range(1, CONV_WIDTH):
        xs = pltpu.roll(xr, s, axis=0)
        top = jnp.where(row8 < s, pltpu.roll(tail, s, axis=0), xs[:SUBLANES])
        xs = jnp.concatenate([top, xs[SUBLANES:]], axis=0)
        xc = xc + xs * cw_ref[CONV_WIDTH - 1 - s:CONV_WIDTH - s, cols]
    tail_ref[:, cols] = xr[tt - SUBLANES:tt, :]

    xcb = xc.astype(BF16)
    gate_r = _sigmoid(_dot(xcb, wa_ref[head]) + ba_ref[:, cols])
    gate_i = _sigmoid(_dot(xcb, wx_ref[head]) + bx_ref[:, cols])
    log_a = LRU_C * gate_r * (-_softplus(-lam_ref[:, cols]))
    a = jnp.exp(log_a)
    mult = jnp.sqrt(1.0 - a * a)
    row = lax.broadcasted_iota(jnp.int32, (tt, blk), 0)
    mult = jnp.where((row == 0) & is_first, 1.0, mult)
    b = xc * gate_i * mult

    n_groups = tt // SUBLANES
    a3 = a.reshape(n_groups, SUBLANES, blk)
    b3 = b.reshape(n_groups, SUBLANES, blk)
    sub = lax.broadcasted_iota(jnp.int32, a3.shape, 1)
    shift = 1
    while shift < SUBLANES:
        valid = sub >= shift
        b3 = b3 + a3 * jnp.where(valid, pltpu.roll(b3, shift, axis=1), 0.0)
        a3 = a3 * jnp.where(valid, pltpu.roll(a3, shift, axis=1), 1.0)
        shift *= 2
    carry = h_ref[0:1, cols]
    groups = []
    for i in range(n_groups):
        h_i = b3[i] + a3[i] * carry
        groups.append(h_i)
        carry = h_i[SUBLANES - 1:SUBLANES, :]
    h_ref[:, cols] = jnp.broadcast_to(carry, (SUBLANES, blk))
    return _gelu_tanh(gate_ref[:, cols]) * jnp.concatenate(groups, axis=0)


def _rglru(proj, x, conv_w, conv_b, w_a, w_x, b_a, b_x, lam, w_out, *, batch, seq, tt):
    width = lam.shape[1]
    d = x.shape[1]
    nc = seq // tt
    cur = lambda col: pl.BlockSpec((tt, width), lambda b, c: (b * nc + jnp.minimum(c, nc - 1), col))
    prev = pl.BlockSpec((tt, d), lambda b, c: (b * nc + jnp.maximum(c - 1, 0), 0))
    par = lambda arr: pl.BlockSpec(arr.shape, lambda b, c: (0,) * arr.ndim, pipeline_mode=pl.Buffered(1))
    return pl.pallas_call(
        _rglru_kernel,
        grid=(batch, nc + 1),
        in_specs=[cur(0), cur(1), prev, par(conv_w), par(conv_b), par(w_a), par(w_x),
                  par(b_a), par(b_x), par(lam), par(w_out)],
        out_specs=prev,
        out_shape=jax.ShapeDtypeStruct((batch * seq, d), F32),
        scratch_shapes=[pltpu.VMEM((SUBLANES, width), F32), pltpu.VMEM((SUBLANES, width), F32),
                        pltpu.VMEM((2, tt, width), BF16)],
        compiler_params=_compiler_params(("parallel", "arbitrary"), 56),
        name="rglru",
    )(proj, proj, x, conv_w, conv_b, w_a, w_x, b_a, b_x, lam, w_out)


def _pad_cols(w, n):
    return jnp.pad(w, ((0, 0), (0, n - w.shape[1])))


def _pad_rows(w, n):
    return jnp.pad(w, ((0, n - w.shape[0]), (0, 0)))


def kernel(x, norm_mix_g, norm_mlp_g, norm_f_g, mlp_w1, mlp_w2, hy_w_in, hy_w_out, rw_mu_rkv, rw_mu_wag, rw_w0, rw_w1, rw_w2, rw_a0, rw_a1, rw_a2, rw_g1, rw_g2, rw_k_k, rw_k_a, rw_r_k, rw_ln_w, rw_ln_b, s5_lam_re, s5_lam_im, s5_log_dt, s5_b_re, s5_b_im, s5_c_re, s5_c_im, s5_d, s5_glu_w, s5_glu_b, rg_w_in, rg_w_out, rg_conv_w, rg_conv_b, rg_w_a, rg_b_a, rg_w_x, rg_b_x, rg_lam):
    batch, seq, d = x.shape
    m = batch * seq
    rw = rw_w2.shape[-1]
    row = lambda v: v.reshape(1, -1)
    bf = lambda w: w.astype(BF16)
    lane = 128
    xf = x.reshape(m, d)

    lw, a, gate, proj = _rwkv_lora(
        xf, row(norm_mix_g[0]), rw_mu_wag[0],
        bf(_pad_cols(rw_w1[0], lane)), bf(_pad_cols(rw_a1[0], lane)), bf(rw_g1[0]),
        bf(_pad_rows(rw_w2[0], lane)), bf(_pad_rows(rw_a2[0], lane)), bf(rw_g2[0]),
        row(rw_w0[0]), row(rw_a0[0]), bf(hy_w_in[0]), seq=seq, tm=256)
    ya = _rwkv_core(proj, lw, a, gate, rw_mu_rkv[0], row(rw_k_k[0]), row(rw_k_a[0]),
                    row(rw_r_k[0]), row(rw_ln_w[0]), row(rw_ln_b[0]), batch=batch, seq=seq)

    mats = _s5_matrices(s5_lam_re[0], s5_lam_im[0], s5_log_dt[0], s5_b_re[0], s5_b_im[0],
                        s5_c_re[0], s5_c_im[0], seq=seq)
    ys = _s5_core(proj, *mats, batch=batch, seq=seq, width=s5_d.shape[-1])

    x1 = _mix_out(xf, ya, ys, proj, row(s5_d[0]), bf(s5_glu_w[0]), row(s5_glu_b[0]),
                  bf(hy_w_out[0][:rw]), bf(hy_w_out[0][rw:]), tm=256)
    x2, hn1 = _mlp(x1, row(norm_mlp_g[0]), bf(mlp_w1[0]), bf(mlp_w2[0]), row(norm_mix_g[1]),
                   norm_mode="next", tm=512, tf=1024)

    proj1 = _matmul(hn1, bf(rg_w_in[0]), tm=1024, tn=1024)
    x3 = _rglru(proj1, x2, rg_conv_w[0], row(rg_conv_b[0]), bf(rg_w_a[0]), bf(rg_w_x[0]),
                row(rg_b_a[0]), row(rg_b_x[0]), row(rg_lam[0]), bf(rg_w_out[0]),
                batch=batch, seq=seq, tt=256)
    (out,) = _mlp(x3, row(norm_mlp_g[1]), bf(mlp_w1[1]), bf(mlp_w2[1]), row(norm_f_g),
                  norm_mode="final", tm=512, tf=1024)
    return out.reshape(batch, seq, d)
```

```python
import functools
import math

import jax
import jax.numpy as jnp
from jax import lax
from jax.experimental import pallas as pl
from jax.experimental.pallas import tpu as pltpu

F32 = jnp.float32
BF16 = jnp.bfloat16

NORM_EPS = 1e-6
RWKV_HEAD = 64
RWKV_GN_EPS = 64e-5
RWKV_CHUNK = 32
RWKV_SEQS = 2
HEAD_PAIR = 2 * RWKV_HEAD
S5_GROUP = 16
S5_STATE = 64
S5_CHUNK = 16
S5_SLAB = 256
S5_PART_GROUPS = 4
LRU_C = 8.0
CONV_WIDTH = 4
SUBLANES = 8
LANES = 128
V7X_VMEM_BYTES = 64 * 1024 * 1024


def _compiler_params(semantics, vmem_mib):
    assert vmem_mib * 1024 * 1024 < V7X_VMEM_BYTES
    return pltpu.CompilerParams(dimension_semantics=semantics,
                                vmem_limit_bytes=vmem_mib * 1024 * 1024)


def _rms(x, g):
    return x * lax.rsqrt(jnp.mean(x * x, axis=-1, keepdims=True) + NORM_EPS) * g


def _gelu_tanh(x):
    c = math.sqrt(2.0 / math.pi)
    return 0.5 * x * (1.0 + jnp.tanh(c * (x + 0.044715 * (x * x * x))))


def _sigmoid(x):
    return 1.0 / (1.0 + jnp.exp(-x))


def _softplus(x):
    return jnp.maximum(x, 0.0) + jnp.log1p(jnp.exp(-jnp.abs(x)))


def _dot(a, b):
    return jnp.dot(a, b, preferred_element_type=F32)


def _dot_nt(a, b):
    return lax.dot_general(a, b, (((1,), (1,)), ((), ())), preferred_element_type=F32)


def _dot_tn(a, b):
    return lax.dot_general(a, b, (((0,), (0,)), ((), ())), preferred_element_type=F32)


def _shift_rows(x, prev_row):
    row = lax.broadcasted_iota(jnp.int32, x.shape, 0)
    return jnp.where(row == 0, prev_row, pltpu.roll(x, 1, axis=0))


def _matmul_kernel(h_ref, w_ref, o_ref):
    o_ref[...] = _dot(h_ref[...], w_ref[...])


def _matmul(h, w, *, tm, tn):
    m, d = h.shape
    n = w.shape[1]
    return pl.pallas_call(
        _matmul_kernel,
        grid=(m // tm, n // tn),
        in_specs=[pl.BlockSpec((tm, d), lambda i, j: (i, 0)),
                  pl.BlockSpec((d, tn), lambda i, j: (0, j))],
        out_specs=pl.BlockSpec((tm, tn), lambda i, j: (i, j)),
        out_shape=jax.ShapeDtypeStruct((m, n), F32),
        compiler_params=_compiler_params(("parallel", "arbitrary"), 40),
        name="in_proj",
    )(h, w)


def _lora_kernel(x_ref, xp_ref, g_ref, mu_ref, w1_ref, a1_ref, g1_ref, w2_ref, a2_ref, g2_ref,
                 w0_ref, a0_ref, win_ref, lw_ref, a_ref, gate_ref, proj_ref, *, blocks_per_seq):
    g = g_ref[...]
    h = _rms(x_ref[...], g)
    proj_ref[...] = _dot(h.astype(BF16), win_ref[...])
    hp = _rms(xp_ref[...], g)
    first = (pl.program_id(0) % blocks_per_seq) == 0
    prev_row = jnp.where(first, 0.0, hp[SUBLANES - 1:SUBLANES, :])
    dh = _shift_rows(h, prev_row) - h
    xw = (h + dh * mu_ref[0:1, :]).astype(BF16)
    xa = (h + dh * mu_ref[1:2, :]).astype(BF16)
    xg = (h + dh * mu_ref[2:3, :]).astype(BF16)
    tw = jnp.tanh(_dot(xw, w1_ref[...])).astype(BF16)
    w_pre = w0_ref[...] + _dot(tw, w2_ref[...])
    w_log = -_softplus(-w_pre) - 0.5
    lw_ref[...] = -jnp.exp(w_log)
    ta = _dot(xa, a1_ref[...]).astype(BF16)
    a_ref[...] = _sigmoid(a0_ref[...] + _dot(ta, a2_ref[...]))
    tg = _sigmoid(_dot(xg, g1_ref[...])).astype(BF16)
    gate_ref[...] = _dot(tg, g2_ref[...])


def _rwkv_lora(x, g, mu_wag, w1, a1, g1, w2, a2, g2, w0, a0, w_in, *, seq, tm):
    m, d = x.shape
    width = w2.shape[1]
    n_proj = w_in.shape[1]
    blocks_per_seq = seq // tm
    full = lambda arr: pl.BlockSpec(arr.shape, lambda i: (0, 0), pipeline_mode=pl.Buffered(1))
    out = jax.ShapeDtypeStruct((m, width), F32)
    return pl.pallas_call(
        functools.partial(_lora_kernel, blocks_per_seq=blocks_per_seq),
        grid=(m // tm,),
        in_specs=[pl.BlockSpec((tm, d), lambda i: (i, 0)),
                  pl.BlockSpec((SUBLANES, d), lambda i: (jnp.maximum(i * (tm // SUBLANES) - 1, 0), 0)),
                  full(g), full(mu_wag), full(w1), full(a1), full(g1), full(w2), full(a2), full(g2),
                  full(w0), full(a0), full(w_in)],
        out_specs=[pl.BlockSpec((tm, width), lambda i: (i, 0))] * 3 + [pl.BlockSpec((tm, n_proj), lambda i: (i, 0))],
        out_shape=[out, out, out, jax.ShapeDtypeStruct((m, n_proj), F32)],
        compiler_params=_compiler_params(("parallel",), 56),
        name="rwkv_lora",
    )(x, x, g, mu_wag, w1, a1, g1, w2, a2, g2, w0, a0, w_in)


def _split3(x):
    hi = x.astype(BF16)
    r1 = x - hi.astype(F32)
    mid = r1.astype(BF16)
    lo = (r1 - mid.astype(F32)).astype(BF16)
    return hi, mid, lo


def _split2(x):
    hi = x.astype(BF16)
    lo = (x - hi.astype(F32)).astype(BF16)
    return hi, lo


def _rwkv_kernel(pr_ref, pk_ref, pv_ref, lw_ref, a_ref, gate_ref, mu_ref, kk_ref, ka_ref, rk_ref,
                 lnw_ref, lnb_ref, o_ref, prev_ref, state_ref):
    n_seq, t, width = o_ref.shape
    rows = n_seq * t
    n_pairs = width // HEAD_PAIR
    seqs = range(n_seq)
    pairs = range(n_pairs)
    chains = [(s, p) for s in seqs for p in pairs]
    slab = lambda x, p: x[:, p * HEAD_PAIR:(p + 1) * HEAD_PAIR]
    seq_rows = lambda x, s: x[s * t:(s + 1) * t]
    lanes = lambda xs: jnp.concatenate(xs, axis=1)
    flat = lambda ref: ref[...].reshape(rows, width)

    @pl.when(pl.program_id(1) == 0)
    def _():
        prev_ref[...] = jnp.zeros_like(prev_ref)
        state_ref[...] = jnp.zeros_like(state_ref)

    row = lax.broadcasted_iota(jnp.int32, (rows, width), 0)

    def per_seq(vals):
        out = vals[-1]
        for s in reversed(range(n_seq - 1)):
            out = jnp.where(row < (s + 1) * t, vals[s], out)
        return jnp.broadcast_to(out, (rows, width))

    def shift_mix(p_ref, idx):
        p = flat(p_ref)
        ps = pltpu.roll(p, 1, axis=0)
        for s in seqs:
            ps = jnp.where(row == s * t, prev_ref[idx, s, SUBLANES - 1:SUBLANES, :], ps)
            prev_ref[idx, s] = p[(s + 1) * t - SUBLANES:(s + 1) * t, :]
        return p + (ps - p) * mu_ref[idx:idx + 1, :]

    r_all = shift_mix(pr_ref, 0)
    k_all = shift_mix(pk_ref, 1)
    v_all = shift_mix(pv_ref, 2)
    a_all = flat(a_ref)
    lw_all = flat(lw_ref)

    ri = lax.broadcasted_iota(jnp.int32, (rows, rows), 0)
    ci = lax.broadcasted_iota(jnp.int32, (rows, rows), 1)
    tril = jnp.where((ci <= ri) & (ci // t == ri // t), 1.0, 0.0).astype(BF16)
    cum_all = _dot(jnp.concatenate([tril, tril, tril], axis=1),
                   jnp.concatenate(_split3(lw_all), axis=0))

    si = lax.broadcasted_iota(jnp.int32, (2 * HEAD_PAIR, HEAD_PAIR), 0) % HEAD_PAIR // RWKV_HEAD
    sj = lax.broadcasted_iota(jnp.int32, (2 * HEAD_PAIR, HEAD_PAIR), 1) // RWKV_HEAD
    seg2 = jnp.where(si == sj, 1.0, 0.0).astype(BF16)

    def head_sum_all(x):
        xh, xl = _split2(x)
        stacked = jnp.concatenate([lanes([slab(xh, p), slab(xl, p)]) for p in pairs], axis=0)
        sums = _dot(stacked, seg2)
        return lanes([sums[p * rows:(p + 1) * rows] for p in pairs])

    head0 = (lax.broadcasted_iota(jnp.int32, (t, width), 1) % HEAD_PAIR) < RWKV_HEAD

    def expand(x, s):
        xb = seq_rows(x, s).astype(BF16)
        zero = jnp.zeros_like(xb)
        return jnp.concatenate([jnp.where(head0, xb, zero), jnp.where(head0, zero, xb)], axis=0)

    bi = lax.broadcasted_iota(jnp.int32, (2 * t, 2 * t), 0) % t
    bj = lax.broadcasted_iota(jnp.int32, (2 * t, 2 * t), 1) % t
    strict = bj < bi
    incl = bj <= bi
    eye = jnp.where(lax.broadcasted_iota(jnp.int32, (2 * t, 2 * t), 0)
                    == lax.broadcasted_iota(jnp.int32, (2 * t, 2 * t), 1), 1.0, 0.0)

    kk = k_all * kk_ref[...]
    kk = kk / jnp.maximum(jnp.sqrt(head_sum_all(kk * kk)), 1e-12)
    k_all = k_all * (1.0 + (a_all - 1.0) * ka_ref[...])
    b_all = kk * a_all
    cum_last = [cum_all[(s + 1) * t - 1:(s + 1) * t, :] for s in seqs]
    cum_t = per_seq(cum_last)
    p_inv = jnp.exp(-cum_all)
    d_rest = jnp.exp(cum_t - cum_all)
    a_t = kk * jnp.exp(cum_all - lw_all)
    r_t = r_all * jnp.exp(cum_all)
    k_t, b_t = k_all * p_inv, b_all * p_inv
    k_end, b_end = k_all * d_rest, -(b_all * d_rest)
    ar = [jnp.concatenate([expand(a_t, s), expand(r_t, s)], axis=0) for s in seqs]
    kb = [jnp.concatenate([expand(k_t, s), expand(b_t, s)], axis=0) for s in seqs]
    kbt = [jnp.concatenate([expand(k_end, s), expand(b_end, s)], axis=0) for s in seqs]
    v_e = [expand(v_all, s) for s in seqs]
    decay_t = [jnp.exp(c) for c in cum_last]

    states = [state_ref[s, p] for s, p in chains]
    scores = [_dot_nt(slab(ar[s], p), slab(kb[s], p)) for s, p in chains]
    ar_h = [_dot_nt(slab(ar[s], p), st.astype(BF16)) for (s, p), st in zip(chains, states)]
    n_ab = [jnp.where(strict, sc[:2 * t, 2 * t:], 0.0) for sc in scores]
    a_ak = [jnp.where(strict, sc[:2 * t, :2 * t], 0.0).astype(BF16) for sc in scores]
    a_r = [jnp.concatenate([jnp.where(incl, sc[2 * t:, :2 * t], 0.0),
                            jnp.where(incl, -sc[2 * t:, 2 * t:], 0.0)], axis=1).astype(BF16)
           for sc in scores]
    rhs = [(ar_h[i][:2 * t] + _dot(a_ak[i], slab(v_e[s], p))).astype(BF16)
           for i, (s, p) in enumerate(chains)]

    inv = [eye - n for n in n_ab]
    n_pow = [n.astype(BF16) for n in n_ab]
    n_pow = [_dot(n, n).astype(BF16) for n in n_pow]
    n_levels = int(math.log2(t)) - 1
    for level in range(n_levels):
        nxt = [_dot(n, n).astype(BF16) for n in n_pow] if level + 1 < n_levels else None
        inv = [x + _dot(x.astype(BF16), n) for x, n in zip(inv, n_pow)]
        n_pow = nxt

    u_e = [_dot(x.astype(BF16), rh).astype(BF16) for x, rh in zip(inv, rhs)]
    vu = [jnp.concatenate([slab(v_e[s], p), u_e[i]], axis=0) for i, (s, p) in enumerate(chains)]
    o_e = [ar_h[i][2 * t:] + _dot(a_r[i], vu[i]) for i in range(len(chains))]
    for i, (s, p) in enumerate(chains):
        state_ref[s, p] = states[i] * slab(decay_t[s], p) + _dot_tn(vu[i], slab(kbt[s], p))
    o = jnp.concatenate(
        [lanes([o_e[s * n_pairs + p][:t] + o_e[s * n_pairs + p][t:] for p in pairs]) for s in seqs],
        axis=0)

    inv_n = 1.0 / RWKV_HEAD
    mean = head_sum_all(o) * inv_n
    d = o - mean
    var = head_sum_all(d * d) * inv_n
    y = d * lax.rsqrt(var + RWKV_GN_EPS) * lnw_ref[...] + lnb_ref[...]
    y = y + head_sum_all(r_all * k_all * rk_ref[...]) * v_all
    o_ref[...] = (y * flat(gate_ref)).reshape(n_seq, t, width)


def _rwkv_core(proj, lw, a, gate, mu_rkv, k_k, k_a, r_k, ln_w, ln_b, *, batch, seq):
    width = lw.shape[1]
    t, n_seq = RWKV_CHUNK, RWKV_SEQS
    assert batch % n_seq == 0 and seq % t == 0
    nc = seq // t
    as_seqs = lambda arr: arr.reshape(batch, seq, arr.shape[1])
    blk = lambda col: pl.BlockSpec((n_seq, t, width), lambda b, c: (b, c, col))
    par = lambda arr: pl.BlockSpec(arr.shape, lambda b, c: (0, 0))
    proj3 = as_seqs(proj)
    out = pl.pallas_call(
        _rwkv_kernel,
        grid=(batch // n_seq, nc),
        in_specs=[blk(0), blk(1), blk(2), blk(0), blk(0), blk(0),
                  par(mu_rkv), par(k_k), par(k_a), par(r_k), par(ln_w), par(ln_b)],
        out_specs=blk(0),
        out_shape=jax.ShapeDtypeStruct((batch, seq, width), F32),
        scratch_shapes=[pltpu.VMEM((3, n_seq, SUBLANES, width), F32),
                        pltpu.VMEM((n_seq, width // HEAD_PAIR, HEAD_PAIR, HEAD_PAIR), F32)],
        compiler_params=_compiler_params(("parallel", "arbitrary"), 32),
        name="rwkv_core",
    )(proj3, proj3, proj3, as_seqs(lw), as_seqs(a), as_seqs(gate), mu_rkv, k_k, k_a, r_k, ln_w, ln_b)
    return out.reshape(batch * seq, width)


def _s5_kernel(u_ref, fir_ref, bmat_ref, cmat_ref, pw_ref, y_ref,
               nat_ref, mid_ref, ud_ref, bu_ref, sr_ref, yd_ref):
    seq = u_ref.shape[0]
    t = S5_CHUNK
    nc = seq // t
    quarter = seq // 4
    n_lane_halves = u_ref.shape[1] // LANES
    n_state = bmat_ref.shape[2] // 2

    for h in range(n_lane_halves):
        lanes = slice(h * LANES, (h + 1) * LANES)
        nat_ref[h] = u_ref[:, lanes]
        for q in range(4):
            mid_ref[h, q * quarter:(q + 1) * quarter, :] = nat_ref[h, pl.ds(q, quarter, stride=4), :]
        for q in range(4):
            for r in range(4):
                pos = 4 * r + q
                ud_ref[pos * nc:(pos + 1) * nc, lanes] = (
                    mid_ref[h, pl.ds(q * quarter + r, nc, stride=4), :].astype(BF16))

    yd_ref[...] = _dot(ud_ref[...], fir_ref[0])
    for tau in range(1, t):
        yd_ref[tau * nc:, :] += _dot(ud_ref[0:(t - tau) * nc, :], fir_ref[tau])

    n_parts = bmat_ref.shape[0]
    for part in range(n_parts):
        bu_ref[part] = _dot(ud_ref[...], bmat_ref[part])
    row = lax.broadcasted_iota(jnp.int32, (nc, LANES), 0)
    cmul = lambda x, y: (x[0] * y[0] - x[1] * y[1], x[0] * y[1] + x[1] * y[0])
    for part in range(n_parts):
        for lc in range(n_state // LANES):
            re = slice(lc * LANES, (lc + 1) * LANES)
            im = slice(n_state + lc * LANES, n_state + (lc + 1) * LANES)
            pw = lambda k: (pw_ref[part, k:k + 1, re], pw_ref[part, k:k + 1, im])
            lam = pw(0)
            z = (bu_ref[part, 0:nc, re], bu_ref[part, 0:nc, im])
            for pos in range(1, t):
                rows = slice(pos * nc, (pos + 1) * nc)
                lz = cmul(lam, z)
                z = (lz[0] + bu_ref[part, rows, re], lz[1] + bu_ref[part, rows, im])
            shift, level = 1, 0
            while shift < nc:
                valid = row >= shift
                prev = (jnp.where(valid, pltpu.roll(z[0], shift, axis=0), 0.0),
                        jnp.where(valid, pltpu.roll(z[1], shift, axis=0), 0.0))
                step = cmul(pw(t + level), prev)
                z = (z[0] + step[0], z[1] + step[1])
                shift *= 2
                level += 1
            first = row >= 1
            s_in = (jnp.where(first, pltpu.roll(z[0], 1, axis=0), 0.0),
                    jnp.where(first, pltpu.roll(z[1], 1, axis=0), 0.0))
            for pos in range(t):
                rows = slice(pos * nc, (pos + 1) * nc)
                rot = cmul(pw(pos), s_in)
                sr_ref[part, rows, re] = rot[0].astype(BF16)
                sr_ref[part, rows, im] = rot[1].astype(BF16)
        yd_ref[...] += _dot(sr_ref[part], cmat_ref[part])

    for h in range(n_lane_halves):
        lanes = slice(h * LANES, (h + 1) * LANES)
        for q in range(4):
            for r in range(4):
                pos = 4 * r + q
                mid_ref[h, pl.ds(q * quarter + r, nc, stride=4), :] = yd_ref[pos * nc:(pos + 1) * nc, lanes]
        for q in range(4):
            nat_ref[h, pl.ds(q, quarter, stride=4), :] = mid_ref[h, q * quarter:(q + 1) * quarter, :]
        y_ref[:, lanes] = nat_ref[h]


def _s5_core(proj, fir, bmat, cmat, pw, *, batch, seq, width):
    n_slabs = width // S5_SLAB
    u_col0 = (proj.shape[1] - width) // S5_SLAB
    wblk = lambda arr: pl.BlockSpec((None,) + arr.shape[1:], lambda b, s: (s,) + (0,) * (arr.ndim - 1))
    n_parts, n_state2 = bmat.shape[1], bmat.shape[-1]
    return pl.pallas_call(
        _s5_kernel,
        grid=(batch, n_slabs),
        in_specs=[pl.BlockSpec((seq, S5_SLAB), lambda b, s: (b, u_col0 + s)),
                  wblk(fir), wblk(bmat), wblk(cmat), wblk(pw)],
        out_specs=pl.BlockSpec((seq, S5_SLAB), lambda b, s: (b, s)),
        out_shape=jax.ShapeDtypeStruct((batch * seq, width), F32),
        scratch_shapes=[pltpu.VMEM((S5_SLAB // LANES, seq, LANES), F32),
                        pltpu.VMEM((S5_SLAB // LANES, seq, LANES), F32),
                        pltpu.VMEM((seq, S5_SLAB), BF16),
                        pltpu.VMEM((n_parts, seq, n_state2), F32),
                        pltpu.VMEM((n_parts, seq, n_state2), BF16),
                        pltpu.VMEM((seq, S5_SLAB), F32)],
        compiler_params=_compiler_params(("parallel", "arbitrary"), 56),
        name="s5_core",
    )(proj, fir, bmat, cmat, pw)


def _s5_matrices(lam_re, lam_im, log_dt, b_re, b_im, c_re, c_im, *, seq):
    t = S5_CHUNK
    groups = lam_re.shape[0]
    gs = S5_SLAB // S5_GROUP
    gp = S5_PART_GROUPS
    n_parts = gs // gp
    n_slabs = groups // gs
    dt = jnp.exp(log_dt)[:, None]
    mag = jnp.exp(lam_re * dt)
    lb_re, lb_im = mag * jnp.cos(lam_im * dt), mag * jnp.sin(lam_im * dt)
    den = lam_re * lam_re + lam_im * lam_im
    z_re = ((lb_re - 1.0) * lam_re + lb_im * lam_im) / den
    z_im = (lb_im * lam_re - (lb_re - 1.0) * lam_im) / den
    bb_re = z_re[..., None] * b_re - z_im[..., None] * b_im
    bb_im = z_re[..., None] * b_im + z_im[..., None] * b_re

    def mul(carry, _):
        pr, pi = carry
        return (pr * lb_re - pi * lb_im, pr * lb_im + pi * lb_re), (pr, pi)

    _, (pw_re, pw_im) = lax.scan(mul, (jnp.ones_like(lb_re), jnp.zeros_like(lb_re)), None, length=t + 1)

    cl_re = pw_re[:t, :, None, :] * c_re[None] - pw_im[:t, :, None, :] * c_im[None]
    cl_im = pw_re[:t, :, None, :] * c_im[None] + pw_im[:t, :, None, :] * c_re[None]
    taps = jnp.einsum('tghp,gpi->tghi', cl_re, bb_re) - jnp.einsum('tghp,gpi->tghi', cl_im, bb_im)
    fir = jnp.einsum('tsgoi,gk->stgiko', taps.reshape(t, n_slabs, gs, S5_GROUP, S5_GROUP),
                     jnp.eye(gs, dtype=F32)).reshape(n_slabs, t, S5_SLAB, S5_SLAB)

    eye_g = jnp.eye(gp, dtype=F32)
    eye_p = jnp.eye(n_parts, dtype=F32)

    def b_slab(bb):
        blk = jnp.einsum('abgph,gk->abghkp', bb.reshape(n_slabs, n_parts, gp, S5_STATE, S5_GROUP), eye_g)
        blk = blk.reshape(n_slabs, n_parts, gp * S5_GROUP, gp * S5_STATE)
        return (eye_p[None, :, :, None, None] * blk[:, :, None]).reshape(n_slabs, n_parts, S5_SLAB, gp * S5_STATE)

    def c_slab(cc):
        blk = jnp.einsum('abghp,gk->abgpkh', cc.reshape(n_slabs, n_parts, gp, S5_GROUP, S5_STATE), eye_g)
        blk = blk.reshape(n_slabs, n_parts, gp * S5_STATE, gp * S5_GROUP)
        return (eye_p[None, :, None, :, None] * blk[:, :, :, None]).reshape(n_slabs, n_parts, gp * S5_STATE, S5_SLAB)

    bmat = jnp.concatenate([b_slab(bb_re), b_slab(bb_im)], axis=-1)
    cmat = jnp.concatenate([c_slab(c_re), -c_slab(c_im)], axis=2)

    squares = [(pw_re[t], pw_im[t])]
    for _ in range((seq // t - 1).bit_length() - 1):
        sr, si = squares[-1]
        squares.append((sr * sr - si * si, 2.0 * sr * si))
    tab_re = jnp.concatenate([pw_re[1:], jnp.stack([s[0] for s in squares])], axis=0)
    tab_im = jnp.concatenate([pw_im[1:], jnp.stack([s[1] for s in squares])], axis=0)

    def state_lanes(x):
        return jnp.transpose(x.reshape(x.shape[0], n_slabs, n_parts, gp * S5_STATE), (1, 2, 0, 3))

    pw = jnp.concatenate([state_lanes(tab_re), state_lanes(tab_im)], axis=-1)
    return fir.astype(BF16), bmat.astype(BF16), cmat.astype(BF16), pw


def _mix_out_kernel(x_ref, ya_ref, ys_ref, u_ref, d_ref, gw_ref, gb_ref, wa_ref, wb_ref, o_ref):
    yb = _gelu_tanh(ys_ref[...] + d_ref[...] * u_ref[...])
    yb = yb * _sigmoid(_dot(yb.astype(BF16), gw_ref[...]) + gb_ref[...])
    o_ref[...] = (x_ref[...] + _dot(ya_ref[...].astype(BF16), wa_ref[...])
                  + _dot(yb.astype(BF16), wb_ref[...]))


def _mix_out(x, ya, ys, proj, d_skip, glu_w, glu_b, w_out_a, w_out_b, *, tm):
    m, d = x.shape
    width = ya.shape[1]
    u_col = proj.shape[1] // width - 1
    full = lambda arr: pl.BlockSpec(arr.shape, lambda i: (0, 0), pipeline_mode=pl.Buffered(1))
    row = lambda w, col=0: pl.BlockSpec((tm, w), lambda i: (i, col))
    return pl.pallas_call(
        _mix_out_kernel,
        grid=(m // tm,),
        in_specs=[row(d), row(width), row(width), row(width, u_col), full(d_skip), full(glu_w),
                  full(glu_b), full(w_out_a), full(w_out_b)],
        out_specs=row(d),
        out_shape=jax.ShapeDtypeStruct((m, d), F32),
        compiler_params=_compiler_params(("parallel",), 56),
        name="mix_out",
    )(x, ya, ys, proj, d_skip, glu_w, glu_b, w_out_a, w_out_b)


def _mlp_kernel(x_ref, g_ref, w1_ref, w2_ref, gn_ref, o_ref, *rest, norm_mode):
    if norm_mode == "next":
        hn_ref, h_ref = rest
    else:
        (h_ref,) = rest
    f = pl.program_id(1)

    @pl.when(f == 0)
    def _():
        x = x_ref[...]
        h_ref[...] = _rms(x, g_ref[...]).astype(BF16)
        o_ref[...] = x

    hid = jnp.maximum(_dot(h_ref[...], w1_ref[...]), 0.0)
    o_ref[...] += _dot((hid * hid).astype(BF16), w2_ref[...])

    @pl.when(f == pl.num_programs(1) - 1)
    def _():
        normed = _rms(o_ref[...], gn_ref[...])
        if norm_mode == "next":
            hn_ref[...] = normed.astype(BF16)
        else:
            o_ref[...] = normed


def _mlp(x, g, w1, w2, gn, *, norm_mode, tm, tf):
    assert norm_mode in ("final", "next")
    m, d = x.shape
    dff = w1.shape[1]
    row = pl.BlockSpec((tm, d), lambda i, f: (i, 0))
    out_specs, out_shape = [row], [jax.ShapeDtypeStruct((m, d), F32)]
    if norm_mode == "next":
        out_specs.append(row)
        out_shape.append(jax.ShapeDtypeStruct((m, d), BF16))
    return pl.pallas_call(
        functools.partial(_mlp_kernel, norm_mode=norm_mode),
        grid=(m // tm, dff // tf),
        in_specs=[row,
                  pl.BlockSpec((1, d), lambda i, f: (0, 0)),
                  pl.BlockSpec((d, tf), lambda i, f: (0, f)),
                  pl.BlockSpec((tf, d), lambda i, f: (f, 0)),
                  pl.BlockSpec((1, d), lambda i, f: (0, 0))],
        out_specs=out_specs,
        out_shape=out_shape,
        scratch_shapes=[pltpu.VMEM((tm, d), BF16)],
        compiler_params=_compiler_params(("parallel", "arbitrary"), 48),
        name="mlp",
    )(x, g, w1, w2, gn)


def _rglru_kernel(gate_ref, xr_ref, x_ref, cw_ref, cb_ref, wa_ref, wx_ref, ba_ref, bx_ref, lam_ref,
                  wout_ref, o_ref, tail_ref, h_ref, y_ref):
    c = pl.program_id(1)
    n_blocks = pl.num_programs(1) - 1
    n_heads, blk = wa_ref.shape[0], wa_ref.shape[1]
    old, new = (c + 1) % 2, c % 2

    @pl.when(c == 0)
    def _():
        tail_ref[...] = jnp.zeros_like(tail_ref)
        h_ref[...] = jnp.zeros_like(h_ref)
        y_ref[...] = jnp.zeros_like(y_ref)

    def project(h):
        cols = slice(h * blk, (h + 1) * blk)
        o_ref[:, cols] = x_ref[:, cols] + _dot(y_ref[old], wout_ref[:, cols])

    @pl.when(c < n_blocks)
    def _():
        for h in range(n_heads):
            project(h)
            cols = slice(h * blk, (h + 1) * blk)
            y_ref[new, :, cols] = _rglru_head(
                gate_ref, xr_ref, cw_ref, cb_ref, wa_ref, wx_ref, ba_ref, bx_ref, lam_ref,
                tail_ref, h_ref, h, c == 0).astype(BF16)

    @pl.when(c == n_blocks)
    def _():
        for h in range(n_heads):
            project(h)


def _rglru_head(gate_ref, xr_ref, cw_ref, cb_ref, wa_ref, wx_ref, ba_ref, bx_ref, lam_ref,
                tail_ref, h_ref, head, is_first):
    tt = xr_ref.shape[0]
    blk = wa_ref.shape[1]
    cols = slice(head * blk, (head + 1) * blk)

    xr = xr_ref[:, cols]
    tail = tail_ref[:, cols]
    row8 = lax.broadcasted_iota(jnp.int32, (SUBLANES, blk), 0)
    xc = xr * cw_ref[CONV_WIDTH - 1:CONV_WIDTH, cols] + cb_ref[:, cols]
    for s in range(1, CONV_WIDTH):
        xs = pltpu.roll(xr, s, axis=0)
        top = jnp.where(row8 < s, pltpu.roll(tail, s, axis=0), xs[:SUBLANES])
        xs = jnp.concatenate([top, xs[SUBLANES:]], axis=0)
        xc = xc + xs * cw_ref[CONV_WIDTH - 1 - s:CONV_WIDTH - s, cols]
    tail_ref[:, cols] = xr[tt - SUBLANES:tt, :]

    xcb = xc.astype(BF16)
    gate_r = _sigmoid(_dot(xcb, wa_ref[head]) + ba_ref[:, cols])
    gate_i = _sigmoid(_dot(xcb, wx_ref[head]) + bx_ref[:, cols])
    log_a = LRU_C * gate_r * (-_softplus(-lam_ref[:, cols]))
    a = jnp.exp(log_a)
    rest = 1.0 - a * a
    mult = jnp.where(rest > 0.0, rest * lax.rsqrt(rest), 0.0)
    top = jnp.where((row8 == 0) & is_first, 1.0, mult[:SUBLANES])
    mult = jnp.concatenate([top, mult[SUBLANES:]], axis=0)
    b = xc * gate_i * mult

    n_groups = tt // SUBLANES
    a3 = a.reshape(n_groups, SUBLANES, blk)
    b3 = b.reshape(n_groups, SUBLANES, blk)
    sub = lax.broadcasted_iota(jnp.int32, a3.shape, 1)
    shift = 1
    while shift < SUBLANES:
        valid = sub >= shift
        b3 = b3 + a3 * jnp.where(valid, pltpu.roll(b3, shift, axis=1), 0.0)
        a3 = a3 * jnp.where(valid, pltpu.roll(a3, shift, axis=1), 1.0)
        shift *= 2
    carry = h_ref[0:1, cols]
    groups = []
    for i in range(n_groups):
        h_i = b3[i] + a3[i] * carry
        groups.append(h_i)
        carry = h_i[SUBLANES - 1:SUBLANES, :]
    h_ref[:, cols] = jnp.broadcast_to(carry, (SUBLANES, blk))
    return _gelu_tanh(gate_ref[:, cols]) * jnp.concatenate(groups, axis=0)


def _rglru(proj, x, conv_w, conv_b, w_a, w_x, b_a, b_x, lam, w_out, *, batch, seq, tt):
    width = lam.shape[1]
    d = x.shape[1]
    nc = seq // tt
    cur = lambda col: pl.BlockSpec((tt, width), lambda b, c: (b * nc + jnp.minimum(c, nc - 1), col))
    prev = pl.BlockSpec((tt, d), lambda b, c: (b * nc + jnp.maximum(c - 1, 0), 0))
    par = lambda arr: pl.BlockSpec(arr.shape, lambda b, c: (0,) * arr.ndim, pipeline_mode=pl.Buffered(1))
    return pl.pallas_call(
        _rglru_kernel,
        grid=(batch, nc + 1),
        in_specs=[cur(0), cur(1), prev, par(conv_w), par(conv_b), par(w_a), par(w_x),
                  par(b_a), par(b_x), par(lam), par(w_out)],
        out_specs=prev,
        out_shape=jax.ShapeDtypeStruct((batch * seq, d), F32),
        scratch_shapes=[pltpu.VMEM((SUBLANES, width), F32), pltpu.VMEM((SUBLANES, width), F32),
                        pltpu.VMEM((2, tt, width), BF16)],
        compiler_params=_compiler_params(("parallel", "arbitrary"), 56),
        name="rglru",
    )(proj, proj, x, conv_w, conv_b, w_a, w_x, b_a, b_x, lam, w_out)


def _pad_cols(w, n):
    return jnp.pad(w, ((0, 0), (0, n - w.shape[1])))


def _pad_rows(w, n):
    return jnp.pad(w, ((0, n - w.shape[0]), (0, 0)))


def kernel(x, norm_mix_g, norm_mlp_g, norm_f_g, mlp_w1, mlp_w2, hy_w_in, hy_w_out, rw_mu_rkv, rw_mu_wag, rw_w0, rw_w1, rw_w2, rw_a0, rw_a1, rw_a2, rw_g1, rw_g2, rw_k_k, rw_k_a, rw_r_k, rw_ln_w, rw_ln_b, s5_lam_re, s5_lam_im, s5_log_dt, s5_b_re, s5_b_im, s5_c_re, s5_c_im, s5_d, s5_glu_w, s5_glu_b, rg_w_in, rg_w_out, rg_conv_w, rg_conv_b, rg_w_a, rg_b_a, rg_w_x, rg_b_x, rg_lam):
    batch, seq, d = x.shape
    m = batch * seq
    rw = rw_w2.shape[-1]
    row = lambda v: v.reshape(1, -1)
    bf = lambda w: w.astype(BF16)
    lane = 128
    xf = x.reshape(m, d)

    lw, a, gate, proj = _rwkv_lora(
        xf, row(norm_mix_g[0]), rw_mu_wag[0],
        bf(_pad_cols(rw_w1[0], lane)), bf(_pad_cols(rw_a1[0], lane)), bf(rw_g1[0]),
        bf(_pad_rows(rw_w2[0], lane)), bf(_pad_rows(rw_a2[0], lane)), bf(rw_g2[0]),
        row(rw_w0[0]), row(rw_a0[0]), bf(hy_w_in[0]), seq=seq, tm=256)
    ya = _rwkv_core(proj, lw, a, gate, rw_mu_rkv[0], row(rw_k_k[0]), row(rw_k_a[0]),
                    row(rw_r_k[0]), row(rw_ln_w[0]), row(rw_ln_b[0]), batch=batch, seq=seq)

    mats = _s5_matrices(s5_lam_re[0], s5_lam_im[0], s5_log_dt[0], s5_b_re[0], s5_b_im[0],
                        s5_c_re[0], s5_c_im[0], seq=seq)
    ys = _s5_core(proj, *mats, batch=batch, seq=seq, width=s5_d.shape[-1])

    x1 = _mix_out(xf, ya, ys, proj, row(s5_d[0]), bf(s5_glu_w[0]), row(s5_glu_b[0]),
                  bf(hy_w_out[0][:rw]), bf(hy_w_out[0][rw:]), tm=512)
    x2, hn1 = _mlp(x1, row(norm_mlp_g[0]), bf(mlp_w1[0]), bf(mlp_w2[0]), row(norm_mix_g[1]),
                   norm_mode="next", tm=512, tf=1024)

    proj1 = _matmul(hn1, bf(rg_w_in[0]), tm=1024, tn=1024)
    x3 = _rglru(proj1, x2, rg_conv_w[0], row(rg_conv_b[0]), bf(rg_w_a[0]), bf(rg_w_x[0]),
                row(rg_b_a[0]), row(rg_b_x[0]), row(rg_lam[0]), bf(rg_w_out[0]),
                batch=batch, seq=seq, tt=256)
    (out,) = _mlp(x3, row(norm_mlp_g[1]), bf(mlp_w1[1]), bf(mlp_w2[1]), row(norm_f_g),
                  norm_mode="final", tm=512, tf=1024)
    return out.reshape(batch, seq, d)
```

```python
import functools
import math

import jax
import jax.numpy as jnp
from jax import lax
from jax.experimental import pallas as pl
from jax.experimental.pallas import tpu as pltpu

F32 = jnp.float32
BF16 = jnp.bfloat16

NORM_EPS = 1e-6
RWKV_HEAD = 64
RWKV_GN_EPS = 64e-5
RWKV_CHUNK = 64
RWKV_SEQS = 2
HEAD_PAIR = 2 * RWKV_HEAD
S5_GROUP = 16
S5_STATE = 64
S5_CHUNK = 16
S5_SLAB = 256
S5_PART_GROUPS = 4
LRU_C = 8.0
CONV_WIDTH = 4
SUBLANES = 8
LANES = 128
V7X_VMEM_BYTES = 64 * 1024 * 1024


def _compiler_params(semantics, vmem_mib):
    assert vmem_mib * 1024 * 1024 < V7X_VMEM_BYTES
    return pltpu.CompilerParams(dimension_semantics=semantics,
                                vmem_limit_bytes=vmem_mib * 1024 * 1024)


def _rms(x, g):
    return x * lax.rsqrt(jnp.mean(x * x, axis=-1, keepdims=True) + NORM_EPS) * g


def _gelu_tanh(x):
    c = math.sqrt(2.0 / math.pi)
    return 0.5 * x * (1.0 + jnp.tanh(c * (x + 0.044715 * (x * x * x))))


def _sigmoid(x):
    return 1.0 / (1.0 + jnp.exp(-x))


def _softplus(x):
    return jnp.maximum(x, 0.0) + jnp.log1p(jnp.exp(-jnp.abs(x)))


def _dot(a, b):
    return jnp.dot(a, b, preferred_element_type=F32)


def _dot_nt(a, b):
    return lax.dot_general(a, b, (((1,), (1,)), ((), ())), preferred_element_type=F32)


def _dot_tn(a, b):
    return lax.dot_general(a, b, (((0,), (0,)), ((), ())), preferred_element_type=F32)


def _shift_rows(x, prev_row):
    row = lax.broadcasted_iota(jnp.int32, x.shape, 0)
    return jnp.where(row == 0, prev_row, pltpu.roll(x, 1, axis=0))


def _matmul_kernel(h_ref, w_ref, o_ref):
    o_ref[...] = _dot(h_ref[...], w_ref[...])


def _matmul(h, w, *, tm, tn):
    m, d = h.shape
    n = w.shape[1]
    return pl.pallas_call(
        _matmul_kernel,
        grid=(m // tm, n // tn),
        in_specs=[pl.BlockSpec((tm, d), lambda i, j: (i, 0)),
                  pl.BlockSpec((d, tn), lambda i, j: (0, j))],
        out_specs=pl.BlockSpec((tm, tn), lambda i, j: (i, j)),
        out_shape=jax.ShapeDtypeStruct((m, n), F32),
        compiler_params=_compiler_params(("parallel", "arbitrary"), 40),
        name="in_proj",
    )(h, w)


def _lora_kernel(x_ref, xp_ref, g_ref, mu_ref, w1_ref, a1_ref, g1_ref, w2_ref, a2_ref, g2_ref,
                 w0_ref, a0_ref, win_ref, lw_ref, a_ref, gate_ref, proj_ref, *, blocks_per_seq):
    g = g_ref[...]
    h = _rms(x_ref[...], g)
    proj_ref[...] = _dot(h.astype(BF16), win_ref[...])
    hp = _rms(xp_ref[...], g)
    first = (pl.program_id(0) % blocks_per_seq) == 0
    prev_row = jnp.where(first, 0.0, hp[SUBLANES - 1:SUBLANES, :])
    dh = _shift_rows(h, prev_row) - h
    xw = (h + dh * mu_ref[0:1, :]).astype(BF16)
    xa = (h + dh * mu_ref[1:2, :]).astype(BF16)
    xg = (h + dh * mu_ref[2:3, :]).astype(BF16)
    tw = jnp.tanh(_dot(xw, w1_ref[...])).astype(BF16)
    w_pre = w0_ref[...] + _dot(tw, w2_ref[...])
    w_log = -_softplus(-w_pre) - 0.5
    lw_ref[...] = -jnp.exp(w_log)
    ta = _dot(xa, a1_ref[...]).astype(BF16)
    a_ref[...] = _sigmoid(a0_ref[...] + _dot(ta, a2_ref[...]))
    tg = _sigmoid(_dot(xg, g1_ref[...])).astype(BF16)
    gate_ref[...] = _dot(tg, g2_ref[...])


def _rwkv_lora(x, g, mu_wag, w1, a1, g1, w2, a2, g2, w0, a0, w_in, *, seq, tm):
    m, d = x.shape
    width = w2.shape[1]
    n_proj = w_in.shape[1]
    blocks_per_seq = seq // tm
    full = lambda arr: pl.BlockSpec(arr.shape, lambda i: (0, 0), pipeline_mode=pl.Buffered(1))
    out = jax.ShapeDtypeStruct((m, width), F32)
    return pl.pallas_call(
        functools.partial(_lora_kernel, blocks_per_seq=blocks_per_seq),
        grid=(m // tm,),
        in_specs=[pl.BlockSpec((tm, d), lambda i: (i, 0)),
                  pl.BlockSpec((SUBLANES, d), lambda i: (jnp.maximum(i * (tm // SUBLANES) - 1, 0), 0)),
                  full(g), full(mu_wag), full(w1), full(a1), full(g1), full(w2), full(a2), full(g2),
                  full(w0), full(a0), full(w_in)],
        out_specs=[pl.BlockSpec((tm, width), lambda i: (i, 0))] * 3 + [pl.BlockSpec((tm, n_proj), lambda i: (i, 0))],
        out_shape=[out, out, out, jax.ShapeDtypeStruct((m, n_proj), F32)],
        compiler_params=_compiler_params(("parallel",), 56),
        name="rwkv_lora",
    )(x, x, g, mu_wag, w1, a1, g1, w2, a2, g2, w0, a0, w_in)


def _split3(x):
    hi = x.astype(BF16)
    r1 = x - hi.astype(F32)
    mid = r1.astype(BF16)
    lo = (r1 - mid.astype(F32)).astype(BF16)
    return hi, mid, lo


def _split2(x):
    hi = x.astype(BF16)
    lo = (x - hi.astype(F32)).astype(BF16)
    return hi, lo


def _rwkv_kernel(pr_ref, pk_ref, pv_ref, lw_ref, a_ref, gate_ref, mu_ref, kk_ref, ka_ref, rk_ref,
                 lnw_ref, lnb_ref, o_ref, prev_ref, state_ref):
    n_seq, t, width = o_ref.shape
    rows = n_seq * t
    n_pairs = width // HEAD_PAIR
    seqs = range(n_seq)
    pairs = range(n_pairs)
    chains = [(s, p) for s in seqs for p in pairs]
    slab = lambda x, p: x[:, p * HEAD_PAIR:(p + 1) * HEAD_PAIR]
    seq_rows = lambda x, s: x[s * t:(s + 1) * t]
    lanes = lambda xs: jnp.concatenate(xs, axis=1)
    flat = lambda ref: ref[...].reshape(rows, width)

    @pl.when(pl.program_id(1) == 0)
    def _():
        prev_ref[...] = jnp.zeros_like(prev_ref)
        state_ref[...] = jnp.zeros_like(state_ref)

    row = lax.broadcasted_iota(jnp.int32, (rows, width), 0)

    def per_seq(vals):
        out = vals[-1]
        for s in reversed(range(n_seq - 1)):
            out = jnp.where(row < (s + 1) * t, vals[s], out)
        return jnp.broadcast_to(out, (rows, width))

    def shift_mix(p_ref, idx):
        p = flat(p_ref)
        ps = pltpu.roll(p, 1, axis=0)
        for s in seqs:
            ps = jnp.where(row == s * t, prev_ref[idx, s, SUBLANES - 1:SUBLANES, :], ps)
            prev_ref[idx, s] = p[(s + 1) * t - SUBLANES:(s + 1) * t, :]
        return p + (ps - p) * mu_ref[idx:idx + 1, :]

    r_all = shift_mix(pr_ref, 0)
    k_all = shift_mix(pk_ref, 1)
    v_all = shift_mix(pv_ref, 2)
    a_all = flat(a_ref)
    lw_all = flat(lw_ref)

    ri = lax.broadcasted_iota(jnp.int32, (rows, rows), 0)
    ci = lax.broadcasted_iota(jnp.int32, (rows, rows), 1)
    tril = jnp.where((ci <= ri) & (ci // t == ri // t), 1.0, 0.0).astype(BF16)
    cum_all = _dot(jnp.concatenate([tril, tril, tril], axis=1),
                   jnp.concatenate(_split3(lw_all), axis=0))

    si = lax.broadcasted_iota(jnp.int32, (2 * HEAD_PAIR, HEAD_PAIR), 0) % HEAD_PAIR // RWKV_HEAD
    sj = lax.broadcasted_iota(jnp.int32, (2 * HEAD_PAIR, HEAD_PAIR), 1) // RWKV_HEAD
    seg2 = jnp.where(si == sj, 1.0, 0.0).astype(BF16)

    def head_sum_all(x):
        xh, xl = _split2(x)
        stacked = jnp.concatenate([lanes([slab(xh, p), slab(xl, p)]) for p in pairs], axis=0)
        sums = _dot(stacked, seg2)
        return lanes([sums[p * rows:(p + 1) * rows] for p in pairs])

    head0 = (lax.broadcasted_iota(jnp.int32, (t, width), 1) % HEAD_PAIR) < RWKV_HEAD

    def expand(x, s):
        xb = seq_rows(x, s).astype(BF16)
        zero = jnp.zeros_like(xb)
        return jnp.concatenate([jnp.where(head0, xb, zero), jnp.where(head0, zero, xb)], axis=0)

    bi = lax.broadcasted_iota(jnp.int32, (2 * t, 2 * t), 0) % t
    bj = lax.broadcasted_iota(jnp.int32, (2 * t, 2 * t), 1) % t
    strict = bj < bi
    incl = bj <= bi
    eye = jnp.where(lax.broadcasted_iota(jnp.int32, (2 * t, 2 * t), 0)
                    == lax.broadcasted_iota(jnp.int32, (2 * t, 2 * t), 1), 1.0, 0.0)

    kk = k_all * kk_ref[...]
    kk = kk / jnp.maximum(jnp.sqrt(head_sum_all(kk * kk)), 1e-12)
    k_all = k_all * (1.0 + (a_all - 1.0) * ka_ref[...])
    b_all = kk * a_all
    cum_last = [cum_all[(s + 1) * t - 1:(s + 1) * t, :] for s in seqs]
    cum_t = per_seq(cum_last)
    p_inv = jnp.exp(-cum_all)
    d_rest = jnp.exp(cum_t - cum_all)
    a_t = kk * jnp.exp(cum_all - lw_all)
    r_t = r_all * jnp.exp(cum_all)
    k_t, b_t = k_all * p_inv, b_all * p_inv
    k_end, b_end = k_all * d_rest, -(b_all * d_rest)
    ar = [jnp.concatenate([expand(a_t, s), expand(r_t, s)], axis=0) for s in seqs]
    kb = [jnp.concatenate([expand(k_t, s), expand(b_t, s)], axis=0) for s in seqs]
    kbt = [jnp.concatenate([expand(k_end, s), expand(b_end, s)], axis=0) for s in seqs]
    v_e = [expand(v_all, s) for s in seqs]
    decay_t = [jnp.exp(c) for c in cum_last]

    states = [state_ref[s, p] for s, p in chains]
    scores = [_dot_nt(slab(ar[s], p), slab(kb[s], p)) for s, p in chains]
    ar_h = [_dot_nt(slab(ar[s], p), st.astype(BF16)) for (s, p), st in zip(chains, states)]
    n_ab = [jnp.where(strict, sc[:2 * t, 2 * t:], 0.0) for sc in scores]
    a_ak = [jnp.where(strict, sc[:2 * t, :2 * t], 0.0).astype(BF16) for sc in scores]
    a_r = [jnp.concatenate([jnp.where(incl, sc[2 * t:, :2 * t], 0.0),
                            jnp.where(incl, -sc[2 * t:, 2 * t:], 0.0)], axis=1).astype(BF16)
           for sc in scores]
    rhs = [(ar_h[i][:2 * t] + _dot(a_ak[i], slab(v_e[s], p))).astype(BF16)
           for i, (s, p) in enumerate(chains)]

    inv = [eye - n for n in n_ab]
    n_pow = [n.astype(BF16) for n in n_ab]
    n_pow = [_dot(n, n).astype(BF16) for n in n_pow]
    n_levels = int(math.log2(t)) - 1
    for level in range(n_levels):
        nxt = [_dot(n, n).astype(BF16) for n in n_pow] if level + 1 < n_levels else None
        inv = [x + _dot(x.astype(BF16), n) for x, n in zip(inv, n_pow)]
        n_pow = nxt

    u_e = [_dot(x.astype(BF16), rh).astype(BF16) for x, rh in zip(inv, rhs)]
    vu = [jnp.concatenate([slab(v_e[s], p), u_e[i]], axis=0) for i, (s, p) in enumerate(chains)]
    o_e = [ar_h[i][2 * t:] + _dot(a_r[i], vu[i]) for i in range(len(chains))]
    for i, (s, p) in enumerate(chains):
        state_ref[s, p] = states[i] * slab(decay_t[s], p) + _dot_tn(vu[i], slab(kbt[s], p))
    o = jnp.concatenate(
        [lanes([o_e[s * n_pairs + p][:t] + o_e[s * n_pairs + p][t:] for p in pairs]) for s in seqs],
        axis=0)

    inv_n = 1.0 / RWKV_HEAD
    mean = head_sum_all(o) * inv_n
    d = o - mean
    var = head_sum_all(d * d) * inv_n
    y = d * lax.rsqrt(var + RWKV_GN_EPS) * lnw_ref[...] + lnb_ref[...]
    y = y + head_sum_all(r_all * k_all * rk_ref[...]) * v_all
    o_ref[...] = (y * flat(gate_ref)).reshape(n_seq, t, width)


def _rwkv_core(proj, lw, a, gate, mu_rkv, k_k, k_a, r_k, ln_w, ln_b, *, batch, seq):
    width = lw.shape[1]
    t, n_seq = RWKV_CHUNK, RWKV_SEQS
    assert batch % n_seq == 0 and seq % t == 0
    nc = seq // t
    as_seqs = lambda arr: arr.reshape(batch, seq, arr.shape[1])
    blk = lambda col: pl.BlockSpec((n_seq, t, width), lambda b, c: (b, c, col))
    par = lambda arr: pl.BlockSpec(arr.shape, lambda b, c: (0, 0))
    proj3 = as_seqs(proj)
    out = pl.pallas_call(
        _rwkv_kernel,
        grid=(batch // n_seq, nc),
        in_specs=[blk(0), blk(1), blk(2), blk(0), blk(0), blk(0),
                  par(mu_rkv), par(k_k), par(k_a), par(r_k), par(ln_w), par(ln_b)],
        out_specs=blk(0),
        out_shape=jax.ShapeDtypeStruct((batch, seq, width), F32),
        scratch_shapes=[pltpu.VMEM((3, n_seq, SUBLANES, width), F32),
                        pltpu.VMEM((n_seq, width // HEAD_PAIR, HEAD_PAIR, HEAD_PAIR), F32)],
        compiler_params=_compiler_params(("parallel", "arbitrary"), 32),
        name="rwkv_core",
    )(proj3, proj3, proj3, as_seqs(lw), as_seqs(a), as_seqs(gate), mu_rkv, k_k, k_a, r_k, ln_w, ln_b)
    return out.reshape(batch * seq, width)


def _s5_kernel(u_ref, fir_ref, bmat_ref, cmat_ref, pw_ref, y_ref,
               nat_ref, mid_ref, ud_ref, bu_ref, sr_ref, yd_ref):
    seq = u_ref.shape[0]
    t = S5_CHUNK
    nc = seq // t
    quarter = seq // 4
    n_lane_halves = u_ref.shape[1] // LANES
    n_state = bmat_ref.shape[2] // 2

    for h in range(n_lane_halves):
        lanes = slice(h * LANES, (h + 1) * LANES)
        nat_ref[h] = u_ref[:, lanes]
        for q in range(4):
            mid_ref[h, q * quarter:(q + 1) * quarter, :] = nat_ref[h, pl.ds(q, quarter, stride=4), :]
        for q in range(4):
            for r in range(4):
                pos = 4 * r + q
                ud_ref[pos * nc:(pos + 1) * nc, lanes] = (
                    mid_ref[h, pl.ds(q * quarter + r, nc, stride=4), :].astype(BF16))

    yd_ref[...] = _dot(ud_ref[...], fir_ref[0])
    for tau in range(1, t):
        yd_ref[tau * nc:, :] += _dot(ud_ref[0:(t - tau) * nc, :], fir_ref[tau])

    n_parts = bmat_ref.shape[0]
    for part in range(n_parts):
        bu_ref[part] = _dot(ud_ref[...], bmat_ref[part])
    row = lax.broadcasted_iota(jnp.int32, (nc, LANES), 0)
    cmul = lambda x, y: (x[0] * y[0] - x[1] * y[1], x[0] * y[1] + x[1] * y[0])
    for part in range(n_parts):
        for lc in range(n_state // LANES):
            re = slice(lc * LANES, (lc + 1) * LANES)
            im = slice(n_state + lc * LANES, n_state + (lc + 1) * LANES)
            pw = lambda k: (pw_ref[part, k:k + 1, re], pw_ref[part, k:k + 1, im])
            lam = pw(0)
            z = (bu_ref[part, 0:nc, re], bu_ref[part, 0:nc, im])
            for pos in range(1, t):
                rows = slice(pos * nc, (pos + 1) * nc)
                lz = cmul(lam, z)
                z = (lz[0] + bu_ref[part, rows, re], lz[1] + bu_ref[part, rows, im])
            shift, level = 1, 0
            while shift < nc:
                valid = row >= shift
                prev = (jnp.where(valid, pltpu.roll(z[0], shift, axis=0), 0.0),
                        jnp.where(valid, pltpu.roll(z[1], shift, axis=0), 0.0))
                step = cmul(pw(t + level), prev)
                z = (z[0] + step[0], z[1] + step[1])
                shift *= 2
                level += 1
            first = row >= 1
            s_in = (jnp.where(first, pltpu.roll(z[0], 1, axis=0), 0.0),
                    jnp.where(first, pltpu.roll(z[1], 1, axis=0), 0.0))
            for pos in range(t):
                rows = slice(pos * nc, (pos + 1) * nc)
                rot = cmul(pw(pos), s_in)
                sr_ref[part, rows, re] = rot[0].astype(BF16)
                sr_ref[part, rows, im] = rot[1].astype(BF16)
        yd_ref[...] += _dot(sr_ref[part], cmat_ref[part])

    for h in range(n_lane_halves):
        lanes = slice(h * LANES, (h + 1) * LANES)
        for q in range(4):
            for r in range(4):
                pos = 4 * r + q
                mid_ref[h, pl.ds(q * quarter + r, nc, stride=4), :] = yd_ref[pos * nc:(pos + 1) * nc, lanes]
        for q in range(4):
            nat_ref[h, pl.ds(q, quarter, stride=4), :] = mid_ref[h, q * quarter:(q + 1) * quarter, :]
        y_ref[:, lanes] = nat_ref[h]


def _s5_core(proj, fir, bmat, cmat, pw, *, batch, seq, width):
    n_slabs = width // S5_SLAB
    u_col0 = (proj.shape[1] - width) // S5_SLAB
    wblk = lambda arr: pl.BlockSpec((None,) + arr.shape[1:], lambda b, s: (s,) + (0,) * (arr.ndim - 1))
    n_parts, n_state2 = bmat.shape[1], bmat.shape[-1]
    return pl.pallas_call(
        _s5_kernel,
        grid=(batch, n_slabs),
        in_specs=[pl.BlockSpec((seq, S5_SLAB), lambda b, s: (b, u_col0 + s)),
                  wblk(fir), wblk(bmat), wblk(cmat), wblk(pw)],
        out_specs=pl.BlockSpec((seq, S5_SLAB), lambda b, s: (b, s)),
        out_shape=jax.ShapeDtypeStruct((batch * seq, width), F32),
        scratch_shapes=[pltpu.VMEM((S5_SLAB // LANES, seq, LANES), F32),
                        pltpu.VMEM((S5_SLAB // LANES, seq, LANES), F32),
                        pltpu.VMEM((seq, S5_SLAB), BF16),
                        pltpu.VMEM((n_parts, seq, n_state2), F32),
                        pltpu.VMEM((n_parts, seq, n_state2), BF16),
                        pltpu.VMEM((seq, S5_SLAB), F32)],
        compiler_params=_compiler_params(("parallel", "arbitrary"), 56),
        name="s5_core",
    )(proj, fir, bmat, cmat, pw)


def _s5_matrices(lam_re, lam_im, log_dt, b_re, b_im, c_re, c_im, *, seq):
    t = S5_CHUNK
    groups = lam_re.shape[0]
    gs = S5_SLAB // S5_GROUP
    gp = S5_PART_GROUPS
    n_parts = gs // gp
    n_slabs = groups // gs
    dt = jnp.exp(log_dt)[:, None]
    mag = jnp.exp(lam_re * dt)
    lb_re, lb_im = mag * jnp.cos(lam_im * dt), mag * jnp.sin(lam_im * dt)
    den = lam_re * lam_re + lam_im * lam_im
    z_re = ((lb_re - 1.0) * lam_re + lb_im * lam_im) / den
    z_im = (lb_im * lam_re - (lb_re - 1.0) * lam_im) / den
    bb_re = z_re[..., None] * b_re - z_im[..., None] * b_im
    bb_im = z_re[..., None] * b_im + z_im[..., None] * b_re

    def mul(carry, _):
        pr, pi = carry
        return (pr * lb_re - pi * lb_im, pr * lb_im + pi * lb_re), (pr, pi)

    _, (pw_re, pw_im) = lax.scan(mul, (jnp.ones_like(lb_re), jnp.zeros_like(lb_re)), None, length=t + 1)

    cl_re = pw_re[:t, :, None, :] * c_re[None] - pw_im[:t, :, None, :] * c_im[None]
    cl_im = pw_re[:t, :, None, :] * c_im[None] + pw_im[:t, :, None, :] * c_re[None]
    taps = jnp.einsum('tghp,gpi->tghi', cl_re, bb_re) - jnp.einsum('tghp,gpi->tghi', cl_im, bb_im)
    fir = jnp.einsum('tsgoi,gk->stgiko', taps.reshape(t, n_slabs, gs, S5_GROUP, S5_GROUP),
                     jnp.eye(gs, dtype=F32)).reshape(n_slabs, t, S5_SLAB, S5_SLAB)

    eye_g = jnp.eye(gp, dtype=F32)
    eye_p = jnp.eye(n_parts, dtype=F32)

    def b_slab(bb):
        blk = jnp.einsum('abgph,gk->abghkp', bb.reshape(n_slabs, n_parts, gp, S5_STATE, S5_GROUP), eye_g)
        blk = blk.reshape(n_slabs, n_parts, gp * S5_GROUP, gp * S5_STATE)
        return (eye_p[None, :, :, None, None] * blk[:, :, None]).reshape(n_slabs, n_parts, S5_SLAB, gp * S5_STATE)

    def c_slab(cc):
        blk = jnp.einsum('abghp,gk->abgpkh', cc.reshape(n_slabs, n_parts, gp, S5_GROUP, S5_STATE), eye_g)
        blk = blk.reshape(n_slabs, n_parts, gp * S5_STATE, gp * S5_GROUP)
        return (eye_p[None, :, None, :, None] * blk[:, :, :, None]).reshape(n_slabs, n_parts, gp * S5_STATE, S5_SLAB)

    bmat = jnp.concatenate([b_slab(bb_re), b_slab(bb_im)], axis=-1)
    cmat = jnp.concatenate([c_slab(c_re), -c_slab(c_im)], axis=2)

    squares = [(pw_re[t], pw_im[t])]
    for _ in range((seq // t - 1).bit_length() - 1):
        sr, si = squares[-1]
        squares.append((sr * sr - si * si, 2.0 * sr * si))
    tab_re = jnp.concatenate([pw_re[1:], jnp.stack([s[0] for s in squares])], axis=0)
    tab_im = jnp.concatenate([pw_im[1:], jnp.stack([s[1] for s in squares])], axis=0)

    def state_lanes(x):
        return jnp.transpose(x.reshape(x.shape[0], n_slabs, n_parts, gp * S5_STATE), (1, 2, 0, 3))

    pw = jnp.concatenate([state_lanes(tab_re), state_lanes(tab_im)], axis=-1)
    return fir.astype(BF16), bmat.astype(BF16), cmat.astype(BF16), pw


def _mix_out_kernel(x_ref, ya_ref, ys_ref, u_ref, d_ref, gw_ref, gb_ref, wa_ref, wb_ref, o_ref):
    yb = _gelu_tanh(ys_ref[...] + d_ref[...] * u_ref[...])
    yb = yb * _sigmoid(_dot(yb.astype(BF16), gw_ref[...]) + gb_ref[...])
    o_ref[...] = (x_ref[...] + _dot(ya_ref[...].astype(BF16), wa_ref[...])
                  + _dot(yb.astype(BF16), wb_ref[...]))


def _mix_out(x, ya, ys, proj, d_skip, glu_w, glu_b, w_out_a, w_out_b, *, tm):
    m, d = x.shape
    width = ya.shape[1]
    u_col = proj.shape[1] // width - 1
    full = lambda arr: pl.BlockSpec(arr.shape, lambda i: (0, 0), pipeline_mode=pl.Buffered(1))
    row = lambda w, col=0: pl.BlockSpec((tm, w), lambda i: (i, col))
    return pl.pallas_call(
        _mix_out_kernel,
        grid=(m // tm,),
        in_specs=[row(d), row(width), row(width), row(width, u_col), full(d_skip), full(glu_w),
                  full(glu_b), full(w_out_a), full(w_out_b)],
        out_specs=row(d),
        out_shape=jax.ShapeDtypeStruct((m, d), F32),
        compiler_params=_compiler_params(("parallel",), 56),
        name="mix_out",
    )(x, ya, ys, proj, d_skip, glu_w, glu_b, w_out_a, w_out_b)


def _mlp_kernel(x_ref, g_ref, w1_ref, w2_ref, gn_ref, o_ref, *rest, norm_mode):
    if norm_mode == "next":
        hn_ref, h_ref = rest
    else:
        (h_ref,) = rest
    f = pl.program_id(1)

    @pl.when(f == 0)
    def _():
        x = x_ref[...]
        h_ref[...] = _rms(x, g_ref[...]).astype(BF16)
        o_ref[...] = x

    hid = jnp.maximum(_dot(h_ref[...], w1_ref[...]), 0.0)
    o_ref[...] += _dot((hid * hid).astype(BF16), w2_ref[...])

    @pl.when(f == pl.num_programs(1) - 1)
    def _():
        normed = _rms(o_ref[...], gn_ref[...])
        if norm_mode == "next":
            hn_ref[...] = normed.astype(BF16)
        else:
            o_ref[...] = normed


def _mlp(x, g, w1, w2, gn, *, norm_mode, tm, tf):
    assert norm_mode in ("final", "next")
    m, d = x.shape
    dff = w1.shape[1]
    row = pl.BlockSpec((tm, d), lambda i, f: (i, 0))
    out_specs, out_shape = [row], [jax.ShapeDtypeStruct((m, d), F32)]
    if norm_mode == "next":
        out_specs.append(row)
        out_shape.append(jax.ShapeDtypeStruct((m, d), BF16))
    return pl.pallas_call(
        functools.partial(_mlp_kernel, norm_mode=norm_mode),
        grid=(m // tm, dff // tf),
        in_specs=[row,
                  pl.BlockSpec((1, d), lambda i, f: (0, 0)),
                  pl.BlockSpec((d, tf), lambda i, f: (0, f)),
                  pl.BlockSpec((tf, d), lambda i, f: (f, 0)),
                  pl.BlockSpec((1, d), lambda i, f: (0, 0))],
        out_specs=out_specs,
        out_shape=out_shape,
        scratch_shapes=[pltpu.VMEM((tm, d), BF16)],
        compiler_params=_compiler_params(("parallel", "arbitrary"), 48),
        name="mlp",
    )(x, g, w1, w2, gn)


def _rglru_kernel(gate_ref, xr_ref, x_ref, cw_ref, cb_ref, wa_ref, wx_ref, ba_ref, bx_ref, lam_ref,
                  wout_ref, o_ref, tail_ref, h_ref, y_ref):
    c = pl.program_id(1)
    n_blocks = pl.num_programs(1) - 1
    n_heads, blk = wa_ref.shape[0], wa_ref.shape[1]
    old, new = (c + 1) % 2, c % 2

    @pl.when(c == 0)
    def _():
        tail_ref[...] = jnp.zeros_like(tail_ref)
        h_ref[...] = jnp.zeros_like(h_ref)
        y_ref[...] = jnp.zeros_like(y_ref)

    def project(h):
        cols = slice(h * blk, (h + 1) * blk)
        o_ref[:, cols] = x_ref[:, cols] + _dot(y_ref[old], wout_ref[:, cols])

    @pl.when(c < n_blocks)
    def _():
        for h in range(n_heads):
            project(h)
            cols = slice(h * blk, (h + 1) * blk)
            y_ref[new, :, cols] = _rglru_head(
                gate_ref, xr_ref, cw_ref, cb_ref, wa_ref, wx_ref, ba_ref, bx_ref, lam_ref,
                tail_ref, h_ref, h, c == 0).astype(BF16)

    @pl.when(c == n_blocks)
    def _():
        for h in range(n_heads):
            project(h)


def _rglru_head(gate_ref, xr_ref, cw_ref, cb_ref, wa_ref, wx_ref, ba_ref, bx_ref, lam_ref,
                tail_ref, h_ref, head, is_first):
    tt = xr_ref.shape[0]
    blk = wa_ref.shape[1]
    cols = slice(head * blk, (head + 1) * blk)

    xr = xr_ref[:, cols]
    tail = tail_ref[:, cols]
    row8 = lax.broadcasted_iota(jnp.int32, (SUBLANES, blk), 0)
    xc = xr * cw_ref[CONV_WIDTH - 1:CONV_WIDTH, cols] + cb_ref[:, cols]
    for s in range(1, CONV_WIDTH):
        xs = pltpu.roll(xr, s, axis=0)
        top = jnp.where(row8 < s, pltpu.roll(tail, s, axis=0), xs[:SUBLANES])
        xs = jnp.concatenate([top, xs[SUBLANES:]], axis=0)
        xc = xc + xs * cw_ref[CONV_WIDTH - 1 - s:CONV_WIDTH - s, cols]
    tail_ref[:, cols] = xr[tt - SUBLANES:tt, :]

    xcb = xc.astype(BF16)
    gate_r = _sigmoid(_dot(xcb, wa_ref[head]) + ba_ref[:, cols])
    gate_i = _sigmoid(_dot(xcb, wx_ref[head]) + bx_ref[:, cols])
    log_a = LRU_C * gate_r * (-_softplus(-lam_ref[:, cols]))
    a = jnp.exp(log_a)
    rest = 1.0 - a * a
    mult = jnp.where(rest > 0.0, rest * lax.rsqrt(rest), 0.0)
    top = jnp.where((row8 == 0) & is_first, 1.0, mult[:SUBLANES])
    mult = jnp.concatenate([top, mult[SUBLANES:]], axis=0)
    b = xc * gate_i * mult

    n_groups = tt // SUBLANES
    a3 = a.reshape(n_groups, SUBLANES, blk)
    b3 = b.reshape(n_groups, SUBLANES, blk)
    sub = lax.broadcasted_iota(jnp.int32, a3.shape, 1)
    shift = 1
    while shift < SUBLANES:
        valid = sub >= shift
        b3 = b3 + a3 * jnp.where(valid, pltpu.roll(b3, shift, axis=1), 0.0)
        a3 = a3 * jnp.where(valid, pltpu.roll(a3, shift, axis=1), 1.0)
        shift *= 2
    carry = h_ref[0:1, cols]
    groups = []
    for i in range(n_groups):
        h_i = b3[i] + a3[i] * carry
        groups.append(h_i)
        carry = h_i[SUBLANES - 1:SUBLANES, :]
    h_ref[:, cols] = jnp.broadcast_to(carry, (SUBLANES, blk))
    return _gelu_tanh(gate_ref[:, cols]) * jnp.concatenate(groups, axis=0)


def _rglru(proj, x, conv_w, conv_b, w_a, w_x, b_a, b_x, lam, w_out, *, batch, seq, tt):
    width = lam.shape[1]
    d = x.shape[1]
    nc = seq // tt
    cur = lambda col: pl.BlockSpec((tt, width), lambda b, c: (b * nc + jnp.minimum(c, nc - 1), col))
    prev = pl.BlockSpec((tt, d), lambda b, c: (b * nc + jnp.maximum(c - 1, 0), 0))
    par = lambda arr: pl.BlockSpec(arr.shape, lambda b, c: (0,) * arr.ndim, pipeline_mode=pl.Buffered(1))
    return pl.pallas_call(
        _rglru_kernel,
        grid=(batch, nc + 1),
        in_specs=[cur(0), cur(1), prev, par(conv_w), par(conv_b), par(w_a), par(w_x),
                  par(b_a), par(b_x), par(lam), par(w_out)],
        out_specs=prev,
        out_shape=jax.ShapeDtypeStruct((batch * seq, d), F32),
        scratch_shapes=[pltpu.VMEM((SUBLANES, width), F32), pltpu.VMEM((SUBLANES, width), F32),
                        pltpu.VMEM((2, tt, width), BF16)],
        compiler_params=_compiler_params(("parallel", "arbitrary"), 56),
        name="rglru",
    )(proj, proj, x, conv_w, conv_b, w_a, w_x, b_a, b_x, lam, w_out)


def _pad_cols(w, n):
    return jnp.pad(w, ((0, 0), (0, n - w.shape[1])))


def _pad_rows(w, n):
    return jnp.pad(w, ((0, n - w.shape[0]), (0, 0)))


def kernel(x, norm_mix_g, norm_mlp_g, norm_f_g, mlp_w1, mlp_w2, hy_w_in, hy_w_out, rw_mu_rkv, rw_mu_wag, rw_w0, rw_w1, rw_w2, rw_a0, rw_a1, rw_a2, rw_g1, rw_g2, rw_k_k, rw_k_a, rw_r_k, rw_ln_w, rw_ln_b, s5_lam_re, s5_lam_im, s5_log_dt, s5_b_re, s5_b_im, s5_c_re, s5_c_im, s5_d, s5_glu_w, s5_glu_b, rg_w_in, rg_w_out, rg_conv_w, rg_conv_b, rg_w_a, rg_b_a, rg_w_x, rg_b_x, rg_lam):
    batch, seq, d = x.shape
    m = batch * seq
    rw = rw_w2.shape[-1]
    row = lambda v: v.reshape(1, -1)
    bf = lambda w: w.astype(BF16)
    lane = 128
    xf = x.reshape(m, d)

    lw, a, gate, proj = _rwkv_lora(
        xf, row(norm_mix_g[0]), rw_mu_wag[0],
        bf(_pad_cols(rw_w1[0], lane)), bf(_pad_cols(rw_a1[0], lane)), bf(rw_g1[0]),
        bf(_pad_rows(rw_w2[0], lane)), bf(_pad_rows(rw_a2[0], lane)), bf(rw_g2[0]),
        row(rw_w0[0]), row(rw_a0[0]), bf(hy_w_in[0]), seq=seq, tm=256)
    ya = _rwkv_core(proj, lw, a, gate, rw_mu_rkv[0], row(rw_k_k[0]), row(rw_k_a[0]),
                    row(rw_r_k[0]), row(rw_ln_w[0]), row(rw_ln_b[0]), batch=batch, seq=seq)

    mats = _s5_matrices(s5_lam_re[0], s5_lam_im[0], s5_log_dt[0], s5_b_re[0], s5_b_im[0],
                        s5_c_re[0], s5_c_im[0], seq=seq)
    ys = _s5_core(proj, *mats, batch=batch, seq=seq, width=s5_d.shape[-1])

    x1 = _mix_out(xf, ya, ys, proj, row(s5_d[0]), bf(s5_glu_w[0]), row(s5_glu_b[0]),
                  bf(hy_w_out[0][:rw]), bf(hy_w_out[0][rw:]), tm=512)
    x2, hn1 = _mlp(x1, row(norm_mlp_g[0]), bf(mlp_w1[0]), bf(mlp_w2[0]), row(norm_mix_g[1]),
                   norm_mode="next", tm=512, tf=1024)

    proj1 = _matmul(hn1, bf(rg_w_in[0]), tm=1024, tn=1024)
    x3 = _rglru(proj1, x2, rg_conv_w[0], row(rg_conv_b[0]), bf(rg_w_a[0]), bf(rg_w_x[0]),
                row(rg_b_a[0]), row(rg_b_x[0]), row(rg_lam[0]), bf(rg_w_out[0]),
                batch=batch, seq=seq, tt=256)
    (out,) = _mlp(x3, row(norm_mlp_g[1]), bf(mlp_w1[1]), bf(mlp_w2[1]), row(norm_f_g),
                  norm_mode="final", tm=512, tf=1024)
    return out.reshape(batch, seq, d)
```

```python
import functools
import math

import jax
import jax.numpy as jnp
from jax import lax
from jax.experimental import pallas as pl
from jax.experimental.pallas import tpu as pltpu

F32 = jnp.float32
BF16 = jnp.bfloat16

NORM_EPS = 1e-6
RWKV_HEAD = 64
RWKV_GN_EPS = 64e-5
RWKV_CHUNK = 64
RWKV_SEQS = 2
HEAD_PAIR = 2 * RWKV_HEAD
S5_GROUP = 16
S5_STATE = 64
S5_CHUNK = 16
S5_SLAB = 256
S5_PART_GROUPS = 2
LRU_C = 8.0
CONV_WIDTH = 4
SUBLANES = 8
LANES = 128
V7X_VMEM_BYTES = 64 * 1024 * 1024


def _compiler_params(semantics, vmem_mib):
    assert vmem_mib * 1024 * 1024 < V7X_VMEM_BYTES
    return pltpu.CompilerParams(dimension_semantics=semantics,
                                vmem_limit_bytes=vmem_mib * 1024 * 1024)


def _rms(x, g):
    return x * lax.rsqrt(jnp.mean(x * x, axis=-1, keepdims=True) + NORM_EPS) * g


def _gelu_tanh(x):
    c = math.sqrt(2.0 / math.pi)
    return 0.5 * x * (1.0 + jnp.tanh(c * (x + 0.044715 * (x * x * x))))


def _sigmoid(x):
    return 1.0 / (1.0 + jnp.exp(-x))


def _softplus(x):
    return jnp.maximum(x, 0.0) + jnp.log1p(jnp.exp(-jnp.abs(x)))


def _dot(a, b):
    return jnp.dot(a, b, preferred_element_type=F32)


def _dot_nt(a, b):
    return lax.dot_general(a, b, (((1,), (1,)), ((), ())), preferred_element_type=F32)


def _dot_tn(a, b):
    return lax.dot_general(a, b, (((0,), (0,)), ((), ())), preferred_element_type=F32)


def _shift_rows(x, prev_row):
    row = lax.broadcasted_iota(jnp.int32, x.shape, 0)
    return jnp.where(row == 0, prev_row, pltpu.roll(x, 1, axis=0))


def _matmul_kernel(h_ref, w_ref, o_ref):
    o_ref[...] = _dot(h_ref[...], w_ref[...])


def _matmul(h, w, *, tm, tn):
    m, d = h.shape
    n = w.shape[1]
    return pl.pallas_call(
        _matmul_kernel,
        grid=(m // tm, n // tn),
        in_specs=[pl.BlockSpec((tm, d), lambda i, j: (i, 0)),
                  pl.BlockSpec((d, tn), lambda i, j: (0, j))],
        out_specs=pl.BlockSpec((tm, tn), lambda i, j: (i, j)),
        out_shape=jax.ShapeDtypeStruct((m, n), F32),
        compiler_params=_compiler_params(("parallel", "arbitrary"), 40),
        name="in_proj",
    )(h, w)


def _lora_kernel(x_ref, xp_ref, g_ref, mu_ref, w1_ref, a1_ref, g1_ref, w2_ref, a2_ref, g2_ref,
                 w0_ref, a0_ref, win_ref, lw_ref, a_ref, gate_ref, proj_ref, *, blocks_per_seq):
    g = g_ref[...]
    h = _rms(x_ref[...], g)
    proj_ref[...] = _dot(h.astype(BF16), win_ref[...])
    hp = _rms(xp_ref[...], g)
    first = (pl.program_id(0) % blocks_per_seq) == 0
    prev_row = jnp.where(first, 0.0, hp[SUBLANES - 1:SUBLANES, :])
    dh = _shift_rows(h, prev_row) - h
    xw = (h + dh * mu_ref[0:1, :]).astype(BF16)
    xa = (h + dh * mu_ref[1:2, :]).astype(BF16)
    xg = (h + dh * mu_ref[2:3, :]).astype(BF16)
    tw = jnp.tanh(_dot(xw, w1_ref[...])).astype(BF16)
    w_pre = w0_ref[...] + _dot(tw, w2_ref[...])
    w_log = -_softplus(-w_pre) - 0.5
    lw_ref[...] = -jnp.exp(w_log)
    ta = _dot(xa, a1_ref[...]).astype(BF16)
    a_ref[...] = _sigmoid(a0_ref[...] + _dot(ta, a2_ref[...]))
    tg = _sigmoid(_dot(xg, g1_ref[...])).astype(BF16)
    gate_ref[...] = _dot(tg, g2_ref[...])


def _rwkv_lora(x, g, mu_wag, w1, a1, g1, w2, a2, g2, w0, a0, w_in, *, seq, tm):
    m, d = x.shape
    width = w2.shape[1]
    n_proj = w_in.shape[1]
    blocks_per_seq = seq // tm
    full = lambda arr: pl.BlockSpec(arr.shape, lambda i: (0, 0), pipeline_mode=pl.Buffered(1))
    out = jax.ShapeDtypeStruct((m, width), F32)
    return pl.pallas_call(
        functools.partial(_lora_kernel, blocks_per_seq=blocks_per_seq),
        grid=(m // tm,),
        in_specs=[pl.BlockSpec((tm, d), lambda i: (i, 0)),
                  pl.BlockSpec((SUBLANES, d), lambda i: (jnp.maximum(i * (tm // SUBLANES) - 1, 0), 0)),
                  full(g), full(mu_wag), full(w1), full(a1), full(g1), full(w2), full(a2), full(g2),
                  full(w0), full(a0), full(w_in)],
        out_specs=[pl.BlockSpec((tm, width), lambda i: (i, 0))] * 3 + [pl.BlockSpec((tm, n_proj), lambda i: (i, 0))],
        out_shape=[out, out, out, jax.ShapeDtypeStruct((m, n_proj), F32)],
        compiler_params=_compiler_params(("parallel",), 56),
        name="rwkv_lora",
    )(x, x, g, mu_wag, w1, a1, g1, w2, a2, g2, w0, a0, w_in)


def _split3(x):
    hi = x.astype(BF16)
    r1 = x - hi.astype(F32)
    mid = r1.astype(BF16)
    lo = (r1 - mid.astype(F32)).astype(BF16)
    return hi, mid, lo


def _split2(x):
    hi = x.astype(BF16)
    lo = (x - hi.astype(F32)).astype(BF16)
    return hi, lo


def _rwkv_kernel(pr_ref, pk_ref, pv_ref, lw_ref, a_ref, gate_ref, mu_ref, kk_ref, ka_ref, rk_ref,
                 lnw_ref, lnb_ref, o_ref, prev_ref, state_ref):
    n_seq, t, width = o_ref.shape
    rows = n_seq * t
    n_pairs = width // HEAD_PAIR
    seqs = range(n_seq)
    pairs = range(n_pairs)
    chains = [(s, p) for s in seqs for p in pairs]
    slab = lambda x, p: x[:, p * HEAD_PAIR:(p + 1) * HEAD_PAIR]
    seq_rows = lambda x, s: x[s * t:(s + 1) * t]
    lanes = lambda xs: jnp.concatenate(xs, axis=1)
    flat = lambda ref: ref[...].reshape(rows, width)

    @pl.when(pl.program_id(1) == 0)
    def _():
        prev_ref[...] = jnp.zeros_like(prev_ref)
        state_ref[...] = jnp.zeros_like(state_ref)

    row = lax.broadcasted_iota(jnp.int32, (rows, width), 0)

    def per_seq(vals):
        out = vals[-1]
        for s in reversed(range(n_seq - 1)):
            out = jnp.where(row < (s + 1) * t, vals[s], out)
        return jnp.broadcast_to(out, (rows, width))

    def shift_mix(p_ref, idx):
        p = flat(p_ref)
        ps = pltpu.roll(p, 1, axis=0)
        for s in seqs:
            ps = jnp.where(row == s * t, prev_ref[idx, s, SUBLANES - 1:SUBLANES, :], ps)
            prev_ref[idx, s] = p[(s + 1) * t - SUBLANES:(s + 1) * t, :]
        return p + (ps - p) * mu_ref[idx:idx + 1, :]

    r_all = shift_mix(pr_ref, 0)
    k_all = shift_mix(pk_ref, 1)
    v_all = shift_mix(pv_ref, 2)
    a_all = flat(a_ref)
    lw_all = flat(lw_ref)

    ri = lax.broadcasted_iota(jnp.int32, (rows, rows), 0)
    ci = lax.broadcasted_iota(jnp.int32, (rows, rows), 1)
    tril = jnp.where((ci <= ri) & (ci // t == ri // t), 1.0, 0.0).astype(BF16)
    cum_all = _dot(jnp.concatenate([tril, tril, tril], axis=1),
                   jnp.concatenate(_split3(lw_all), axis=0))

    si = lax.broadcasted_iota(jnp.int32, (2 * HEAD_PAIR, HEAD_PAIR), 0) % HEAD_PAIR // RWKV_HEAD
    sj = lax.broadcasted_iota(jnp.int32, (2 * HEAD_PAIR, HEAD_PAIR), 1) // RWKV_HEAD
    seg2 = jnp.where(si == sj, 1.0, 0.0).astype(BF16)

    def head_sum_all(x):
        xh, xl = _split2(x)
        stacked = jnp.concatenate([lanes([slab(xh, p), slab(xl, p)]) for p in pairs], axis=0)
        sums = _dot(stacked, seg2)
        return lanes([sums[p * rows:(p + 1) * rows] for p in pairs])

    head0 = (lax.broadcasted_iota(jnp.int32, (t, width), 1) % HEAD_PAIR) < RWKV_HEAD

    def expand(x, s):
        xb = seq_rows(x, s).astype(BF16)
        zero = jnp.zeros_like(xb)
        return jnp.concatenate([jnp.where(head0, xb, zero), jnp.where(head0, zero, xb)], axis=0)

    bi = lax.broadcasted_iota(jnp.int32, (2 * t, 2 * t), 0) % t
    bj = lax.broadcasted_iota(jnp.int32, (2 * t, 2 * t), 1) % t
    strict = bj < bi
    incl = bj <= bi
    eye = jnp.where(lax.broadcasted_iota(jnp.int32, (2 * t, 2 * t), 0)
                    == lax.broadcasted_iota(jnp.int32, (2 * t, 2 * t), 1), 1.0, 0.0)

    kk = k_all * kk_ref[...]
    kk = kk / jnp.maximum(jnp.sqrt(head_sum_all(kk * kk)), 1e-12)
    k_all = k_all * (1.0 + (a_all - 1.0) * ka_ref[...])
    b_all = kk * a_all
    cum_last = [cum_all[(s + 1) * t - 1:(s + 1) * t, :] for s in seqs]
    cum_t = per_seq(cum_last)
    p_inv = jnp.exp(-cum_all)
    d_rest = jnp.exp(cum_t - cum_all)
    a_t = kk * jnp.exp(cum_all - lw_all)
    r_t = r_all * jnp.exp(cum_all)
    k_t, b_t = k_all * p_inv, b_all * p_inv
    k_end, b_end = k_all * d_rest, -(b_all * d_rest)
    ar = [jnp.concatenate([expand(a_t, s), expand(r_t, s)], axis=0) for s in seqs]
    kb = [jnp.concatenate([expand(k_t, s), expand(b_t, s)], axis=0) for s in seqs]
    kbt = [jnp.concatenate([expand(k_end, s), expand(b_end, s)], axis=0) for s in seqs]
    v_e = [expand(v_all, s) for s in seqs]
    decay_t = [jnp.exp(c) for c in cum_last]

    states = [state_ref[s, p] for s, p in chains]
    scores = [_dot_nt(slab(ar[s], p), slab(kb[s], p)) for s, p in chains]
    ar_h = [_dot_nt(slab(ar[s], p), st.astype(BF16)) for (s, p), st in zip(chains, states)]
    n_ab = [jnp.where(strict, sc[:2 * t, 2 * t:], 0.0) for sc in scores]
    a_ak = [jnp.where(strict, sc[:2 * t, :2 * t], 0.0).astype(BF16) for sc in scores]
    a_r = [jnp.concatenate([jnp.where(incl, sc[2 * t:, :2 * t], 0.0),
                            jnp.where(incl, -sc[2 * t:, 2 * t:], 0.0)], axis=1).astype(BF16)
           for sc in scores]
    rhs = [(ar_h[i][:2 * t] + _dot(a_ak[i], slab(v_e[s], p))).astype(BF16)
           for i, (s, p) in enumerate(chains)]

    inv = [eye - n for n in n_ab]
    n_pow = [n.astype(BF16) for n in n_ab]
    n_pow = [_dot(n, n).astype(BF16) for n in n_pow]
    n_levels = int(math.log2(t)) - 1
    for level in range(n_levels):
        nxt = [_dot(n, n).astype(BF16) for n in n_pow] if level + 1 < n_levels else None
        inv = [x + _dot(x.astype(BF16), n) for x, n in zip(inv, n_pow)]
        n_pow = nxt

    u_e = [_dot(x.astype(BF16), rh).astype(BF16) for x, rh in zip(inv, rhs)]
    vu = [jnp.concatenate([slab(v_e[s], p), u_e[i]], axis=0) for i, (s, p) in enumerate(chains)]
    o_e = [ar_h[i][2 * t:] + _dot(a_r[i], vu[i]) for i in range(len(chains))]
    for i, (s, p) in enumerate(chains):
        state_ref[s, p] = states[i] * slab(decay_t[s], p) + _dot_tn(vu[i], slab(kbt[s], p))
    o = jnp.concatenate(
        [lanes([o_e[s * n_pairs + p][:t] + o_e[s * n_pairs + p][t:] for p in pairs]) for s in seqs],
        axis=0)

    inv_n = 1.0 / RWKV_HEAD
    mean = head_sum_all(o) * inv_n
    d = o - mean
    var = head_sum_all(d * d) * inv_n
    y = d * lax.rsqrt(var + RWKV_GN_EPS) * lnw_ref[...] + lnb_ref[...]
    y = y + head_sum_all(r_all * k_all * rk_ref[...]) * v_all
    o_ref[...] = (y * flat(gate_ref)).reshape(n_seq, t, width)


def _rwkv_core(proj, lw, a, gate, mu_rkv, k_k, k_a, r_k, ln_w, ln_b, *, batch, seq):
    width = lw.shape[1]
    t, n_seq = RWKV_CHUNK, RWKV_SEQS
    assert batch % n_seq == 0 and seq % t == 0
    nc = seq // t
    as_seqs = lambda arr: arr.reshape(batch, seq, arr.shape[1])
    blk = lambda col: pl.BlockSpec((n_seq, t, width), lambda b, c: (b, c, col))
    par = lambda arr: pl.BlockSpec(arr.shape, lambda b, c: (0, 0))
    proj3 = as_seqs(proj)
    out = pl.pallas_call(
        _rwkv_kernel,
        grid=(batch // n_seq, nc),
        in_specs=[blk(0), blk(1), blk(2), blk(0), blk(0), blk(0),
                  par(mu_rkv), par(k_k), par(k_a), par(r_k), par(ln_w), par(ln_b)],
        out_specs=blk(0),
        out_shape=jax.ShapeDtypeStruct((batch, seq, width), F32),
        scratch_shapes=[pltpu.VMEM((3, n_seq, SUBLANES, width), F32),
                        pltpu.VMEM((n_seq, width // HEAD_PAIR, HEAD_PAIR, HEAD_PAIR), F32)],
        compiler_params=_compiler_params(("parallel", "arbitrary"), 32),
        name="rwkv_core",
    )(proj3, proj3, proj3, as_seqs(lw), as_seqs(a), as_seqs(gate), mu_rkv, k_k, k_a, r_k, ln_w, ln_b)
    return out.reshape(batch * seq, width)


def _s5_kernel(u_ref, fir_ref, bmat_ref, cmat_ref, pw_ref, y_ref,
               nat_ref, mid_ref, ud_ref, bu_ref, sr_ref, yd_ref):
    seq = u_ref.shape[0]
    t = S5_CHUNK
    nc = seq // t
    quarter = seq // 4
    n_lane_halves = u_ref.shape[1] // LANES
    n_state = bmat_ref.shape[2] // 2

    for h in range(n_lane_halves):
        lanes = slice(h * LANES, (h + 1) * LANES)
        nat_ref[h] = u_ref[:, lanes]
        for q in range(4):
            mid_ref[h, q * quarter:(q + 1) * quarter, :] = nat_ref[h, pl.ds(q, quarter, stride=4), :]
        for q in range(4):
            for r in range(4):
                pos = 4 * r + q
                ud_ref[pos * nc:(pos + 1) * nc, lanes] = (
                    mid_ref[h, pl.ds(q * quarter + r, nc, stride=4), :].astype(BF16))

    yd_ref[...] = _dot(ud_ref[...], fir_ref[0])
    for tau in range(1, t):
        yd_ref[tau * nc:, :] += _dot(ud_ref[0:(t - tau) * nc, :], fir_ref[tau])

    n_parts = bmat_ref.shape[0]
    for part in range(n_parts):
        bu_ref[part] = _dot(ud_ref[...], bmat_ref[part])
    row = lax.broadcasted_iota(jnp.int32, (nc, LANES), 0)
    cmul = lambda x, y: (x[0] * y[0] - x[1] * y[1], x[0] * y[1] + x[1] * y[0])
    for part in range(n_parts):
        for lc in range(n_state // LANES):
            re = slice(lc * LANES, (lc + 1) * LANES)
            im = slice(n_state + lc * LANES, n_state + (lc + 1) * LANES)
            pw = lambda k: (pw_ref[part, k:k + 1, re], pw_ref[part, k:k + 1, im])
            lam = pw(0)
            z = (bu_ref[part, 0:nc, re], bu_ref[part, 0:nc, im])
            for pos in range(1, t):
                rows = slice(pos * nc, (pos + 1) * nc)
                lz = cmul(lam, z)
                z = (lz[0] + bu_ref[part, rows, re], lz[1] + bu_ref[part, rows, im])
            shift, level = 1, 0
            while shift < nc:
                valid = row >= shift
                prev = (jnp.where(valid, pltpu.roll(z[0], shift, axis=0), 0.0),
                        jnp.where(valid, pltpu.roll(z[1], shift, axis=0), 0.0))
                step = cmul(pw(t + level), prev)
                z = (z[0] + step[0], z[1] + step[1])
                shift *= 2
                level += 1
            first = row >= 1
            s_in = (jnp.where(first, pltpu.roll(z[0], 1, axis=0), 0.0),
                    jnp.where(first, pltpu.roll(z[1], 1, axis=0), 0.0))
            for pos in range(t):
                rows = slice(pos * nc, (pos + 1) * nc)
                rot = cmul(pw(pos), s_in)
                sr_ref[part, rows, re] = rot[0].astype(BF16)
                sr_ref[part, rows, im] = rot[1].astype(BF16)
        yd_ref[...] += _dot(sr_ref[part], cmat_ref[part])

    for h in range(n_lane_halves):
        lanes = slice(h * LANES, (h + 1) * LANES)
        for q in range(4):
            for r in range(4):
                pos = 4 * r + q
                mid_ref[h, pl.ds(q * quarter + r, nc, stride=4), :] = yd_ref[pos * nc:(pos + 1) * nc, lanes]
        for q in range(4):
            nat_ref[h, pl.ds(q, quarter, stride=4), :] = mid_ref[h, q * quarter:(q + 1) * quarter, :]
        y_ref[:, lanes] = nat_ref[h]


def _s5_core(proj, fir, bmat, cmat, pw, *, batch, seq, width):
    n_slabs = width // S5_SLAB
    u_col0 = (proj.shape[1] - width) // S5_SLAB
    wblk = lambda arr: pl.BlockSpec((None,) + arr.shape[1:], lambda b, s: (s,) + (0,) * (arr.ndim - 1))
    n_parts, n_state2 = bmat.shape[1], bmat.shape[-1]
    return pl.pallas_call(
        _s5_kernel,
        grid=(batch, n_slabs),
        in_specs=[pl.BlockSpec((seq, S5_SLAB), lambda b, s: (b, u_col0 + s)),
                  wblk(fir), wblk(bmat), wblk(cmat), wblk(pw)],
        out_specs=pl.BlockSpec((seq, S5_SLAB), lambda b, s: (b, s)),
        out_shape=jax.ShapeDtypeStruct((batch * seq, width), F32),
        scratch_shapes=[pltpu.VMEM((S5_SLAB // LANES, seq, LANES), F32),
                        pltpu.VMEM((S5_SLAB // LANES, seq, LANES), F32),
                        pltpu.VMEM((seq, S5_SLAB), BF16),
                        pltpu.VMEM((n_parts, seq, n_state2), F32),
                        pltpu.VMEM((n_parts, seq, n_state2), BF16),
                        pltpu.VMEM((seq, S5_SLAB), F32)],
        compiler_params=_compiler_params(("parallel", "arbitrary"), 56),
        name="s5_core",
    )(proj, fir, bmat, cmat, pw)


def _s5_matrices(lam_re, lam_im, log_dt, b_re, b_im, c_re, c_im, *, seq):
    t = S5_CHUNK
    groups = lam_re.shape[0]
    gs = S5_SLAB // S5_GROUP
    gp = S5_PART_GROUPS
    n_parts = gs // gp
    n_slabs = groups // gs
    dt = jnp.exp(log_dt)[:, None]
    mag = jnp.exp(lam_re * dt)
    lb_re, lb_im = mag * jnp.cos(lam_im * dt), mag * jnp.sin(lam_im * dt)
    den = lam_re * lam_re + lam_im * lam_im
    z_re = ((lb_re - 1.0) * lam_re + lb_im * lam_im) / den
    z_im = (lb_im * lam_re - (lb_re - 1.0) * lam_im) / den
    bb_re = z_re[..., None] * b_re - z_im[..., None] * b_im
    bb_im = z_re[..., None] * b_im + z_im[..., None] * b_re

    def mul(carry, _):
        pr, pi = carry
        return (pr * lb_re - pi * lb_im, pr * lb_im + pi * lb_re), (pr, pi)

    _, (pw_re, pw_im) = lax.scan(mul, (jnp.ones_like(lb_re), jnp.zeros_like(lb_re)), None, length=t + 1)

    cl_re = pw_re[:t, :, None, :] * c_re[None] - pw_im[:t, :, None, :] * c_im[None]
    cl_im = pw_re[:t, :, None, :] * c_im[None] + pw_im[:t, :, None, :] * c_re[None]
    taps = jnp.einsum('tghp,gpi->tghi', cl_re, bb_re) - jnp.einsum('tghp,gpi->tghi', cl_im, bb_im)
    fir = jnp.einsum('tsgoi,gk->stgiko', taps.reshape(t, n_slabs, gs, S5_GROUP, S5_GROUP),
                     jnp.eye(gs, dtype=F32)).reshape(n_slabs, t, S5_SLAB, S5_SLAB)

    eye_g = jnp.eye(gp, dtype=F32)
    eye_p = jnp.eye(n_parts, dtype=F32)

    def b_slab(bb):
        blk = jnp.einsum('abgph,gk->abghkp', bb.reshape(n_slabs, n_parts, gp, S5_STATE, S5_GROUP), eye_g)
        blk = blk.reshape(n_slabs, n_parts, gp * S5_GROUP, gp * S5_STATE)
        return (eye_p[None, :, :, None, None] * blk[:, :, None]).reshape(n_slabs, n_parts, S5_SLAB, gp * S5_STATE)

    def c_slab(cc):
        blk = jnp.einsum('abghp,gk->abgpkh', cc.reshape(n_slabs, n_parts, gp, S5_GROUP, S5_STATE), eye_g)
        blk = blk.reshape(n_slabs, n_parts, gp * S5_STATE, gp * S5_GROUP)
        return (eye_p[None, :, None, :, None] * blk[:, :, :, None]).reshape(n_slabs, n_parts, gp * S5_STATE, S5_SLAB)

    bmat = jnp.concatenate([b_slab(bb_re), b_slab(bb_im)], axis=-1)
    cmat = jnp.concatenate([c_slab(c_re), -c_slab(c_im)], axis=2)

    squares = [(pw_re[t], pw_im[t])]
    for _ in range((seq // t - 1).bit_length() - 1):
        sr, si = squares[-1]
        squares.append((sr * sr - si * si, 2.0 * sr * si))
    tab_re = jnp.concatenate([pw_re[1:], jnp.stack([s[0] for s in squares])], axis=0)
    tab_im = jnp.concatenate([pw_im[1:], jnp.stack([s[1] for s in squares])], axis=0)

    def state_lanes(x):
        return jnp.transpose(x.reshape(x.shape[0], n_slabs, n_parts, gp * S5_STATE), (1, 2, 0, 3))

    pw = jnp.concatenate([state_lanes(tab_re), state_lanes(tab_im)], axis=-1)
    return fir.astype(BF16), bmat.astype(BF16), cmat.astype(BF16), pw


def _mix_out_kernel(x_ref, ya_ref, ys_ref, u_ref, d_ref, gw_ref, gb_ref, wa_ref, wb_ref, o_ref):
    yb = _gelu_tanh(ys_ref[...] + d_ref[...] * u_ref[...])
    yb = yb * _sigmoid(_dot(yb.astype(BF16), gw_ref[...]) + gb_ref[...])
    o_ref[...] = (x_ref[...] + _dot(ya_ref[...].astype(BF16), wa_ref[...])
                  + _dot(yb.astype(BF16), wb_ref[...]))


def _mix_out(x, ya, ys, proj, d_skip, glu_w, glu_b, w_out_a, w_out_b, *, tm):
    m, d = x.shape
    width = ya.shape[1]
    u_col = proj.shape[1] // width - 1
    full = lambda arr: pl.BlockSpec(arr.shape, lambda i: (0, 0), pipeline_mode=pl.Buffered(1))
    row = lambda w, col=0: pl.BlockSpec((tm, w), lambda i: (i, col))
    return pl.pallas_call(
        _mix_out_kernel,
        grid=(m // tm,),
        in_specs=[row(d), row(width), row(width), row(width, u_col), full(d_skip), full(glu_w),
                  full(glu_b), full(w_out_a), full(w_out_b)],
        out_specs=row(d),
        out_shape=jax.ShapeDtypeStruct((m, d), F32),
        compiler_params=_compiler_params(("parallel",), 56),
        name="mix_out",
    )(x, ya, ys, proj, d_skip, glu_w, glu_b, w_out_a, w_out_b)


def _mlp_kernel(x_ref, g_ref, w1_ref, w2_ref, gn_ref, o_ref, *rest, norm_mode):
    if norm_mode == "next":
        hn_ref, h_ref = rest
    else:
        (h_ref,) = rest
    f = pl.program_id(1)

    @pl.when(f == 0)
    def _():
        x = x_ref[...]
        h_ref[...] = _rms(x, g_ref[...]).astype(BF16)
        o_ref[...] = x

    hid = jnp.maximum(_dot(h_ref[...], w1_ref[...]), 0.0)
    o_ref[...] += _dot((hid * hid).astype(BF16), w2_ref[...])

    @pl.when(f == pl.num_programs(1) - 1)
    def _():
        normed = _rms(o_ref[...], gn_ref[...])
        if norm_mode == "next":
            hn_ref[...] = normed.astype(BF16)
        else:
            o_ref[...] = normed


def _mlp(x, g, w1, w2, gn, *, norm_mode, tm, tf):
    assert norm_mode in ("final", "next")
    m, d = x.shape
    dff = w1.shape[1]
    row = pl.BlockSpec((tm, d), lambda i, f: (i, 0))
    out_specs, out_shape = [row], [jax.ShapeDtypeStruct((m, d), F32)]
    if norm_mode == "next":
        out_specs.append(row)
        out_shape.append(jax.ShapeDtypeStruct((m, d), BF16))
    return pl.pallas_call(
        functools.partial(_mlp_kernel, norm_mode=norm_mode),
        grid=(m // tm, dff // tf),
        in_specs=[row,
                  pl.BlockSpec((1, d), lambda i, f: (0, 0)),
                  pl.BlockSpec((d, tf), lambda i, f: (0, f)),
                  pl.BlockSpec((tf, d), lambda i, f: (f, 0)),
                  pl.BlockSpec((1, d), lambda i, f: (0, 0))],
        out_specs=out_specs,
        out_shape=out_shape,
        scratch_shapes=[pltpu.VMEM((tm, d), BF16)],
        compiler_params=_compiler_params(("parallel", "arbitrary"), 48),
        name="mlp",
    )(x, g, w1, w2, gn)


def _rglru_kernel(gate_ref, xr_ref, x_ref, cw_ref, cb_ref, wa_ref, wx_ref, ba_ref, bx_ref, lam_ref,
                  wout_ref, o_ref, tail_ref, h_ref, y_ref):
    c = pl.program_id(1)
    n_blocks = pl.num_programs(1) - 1
    n_heads, blk = wa_ref.shape[0], wa_ref.shape[1]
    old, new = (c + 1) % 2, c % 2

    @pl.when(c == 0)
    def _():
        tail_ref[...] = jnp.zeros_like(tail_ref)
        h_ref[...] = jnp.zeros_like(h_ref)
        y_ref[...] = jnp.zeros_like(y_ref)

    def project(h):
        cols = slice(h * blk, (h + 1) * blk)
        o_ref[:, cols] = x_ref[:, cols] + _dot(y_ref[old], wout_ref[:, cols])

    @pl.when(c < n_blocks)
    def _():
        for h in range(n_heads):
            project(h)
            cols = slice(h * blk, (h + 1) * blk)
            y_ref[new, :, cols] = _rglru_head(
                gate_ref, xr_ref, cw_ref, cb_ref, wa_ref, wx_ref, ba_ref, bx_ref, lam_ref,
                tail_ref, h_ref, h, c == 0).astype(BF16)

    @pl.when(c == n_blocks)
    def _():
        for h in range(n_heads):
            project(h)


def _rglru_head(gate_ref, xr_ref, cw_ref, cb_ref, wa_ref, wx_ref, ba_ref, bx_ref, lam_ref,
                tail_ref, h_ref, head, is_first):
    tt = xr_ref.shape[0]
    blk = wa_ref.shape[1]
    cols = slice(head * blk, (head + 1) * blk)

    xr = xr_ref[:, cols]
    tail = tail_ref[:, cols]
    row8 = lax.broadcasted_iota(jnp.int32, (SUBLANES, blk), 0)
    xc = xr * cw_ref[CONV_WIDTH - 1:CONV_WIDTH, cols] + cb_ref[:, cols]
    for s in range(1, CONV_WIDTH):
        xs = pltpu.roll(xr, s, axis=0)
        top = jnp.where(row8 < s, pltpu.roll(tail, s, axis=0), xs[:SUBLANES])
        xs = jnp.concatenate([top, xs[SUBLANES:]], axis=0)
        xc = xc + xs * cw_ref[CONV_WIDTH - 1 - s:CONV_WIDTH - s, cols]
    tail_ref[:, cols] = xr[tt - SUBLANES:tt, :]

    xcb = xc.astype(BF16)
    gate_r = _sigmoid(_dot(xcb, wa_ref[head]) + ba_ref[:, cols])
    gate_i = _sigmoid(_dot(xcb, wx_ref[head]) + bx_ref[:, cols])
    log_a = LRU_C * gate_r * (-_softplus(-lam_ref[:, cols]))
    a = jnp.exp(log_a)
    rest = 1.0 - a * a
    mult = jnp.where(rest > 0.0, rest * lax.rsqrt(rest), 0.0)
    top = jnp.where((row8 == 0) & is_first, 1.0, mult[:SUBLANES])
    mult = jnp.concatenate([top, mult[SUBLANES:]], axis=0)
    b = xc * gate_i * mult

    n_groups = tt // SUBLANES
    a3 = a.reshape(n_groups, SUBLANES, blk)
    b3 = b.reshape(n_groups, SUBLANES, blk)
    sub = lax.broadcasted_iota(jnp.int32, a3.shape, 1)
    shift = 1
    while shift < SUBLANES:
        valid = sub >= shift
        b3 = b3 + a3 * jnp.where(valid, pltpu.roll(b3, shift, axis=1), 0.0)
        a3 = a3 * jnp.where(valid, pltpu.roll(a3, shift, axis=1), 1.0)
        shift *= 2
    carry = h_ref[0:1, cols]
    groups = []
    for i in range(n_groups):
        h_i = b3[i] + a3[i] * carry
        groups.append(h_i)
        carry = h_i[SUBLANES - 1:SUBLANES, :]
    h_ref[:, cols] = jnp.broadcast_to(carry, (SUBLANES, blk))
    return _gelu_tanh(gate_ref[:, cols]) * jnp.concatenate(groups, axis=0)


def _rglru(proj, x, conv_w, conv_b, w_a, w_x, b_a, b_x, lam, w_out, *, batch, seq, tt):
    width = lam.shape[1]
    d = x.shape[1]
    nc = seq // tt
    cur = lambda col: pl.BlockSpec((tt, width), lambda b, c: (b * nc + jnp.minimum(c, nc - 1), col))
    prev = pl.BlockSpec((tt, d), lambda b, c: (b * nc + jnp.maximum(c - 1, 0), 0))
    par = lambda arr: pl.BlockSpec(arr.shape, lambda b, c: (0,) * arr.ndim, pipeline_mode=pl.Buffered(1))
    return pl.pallas_call(
        _rglru_kernel,
        grid=(batch, nc + 1),
        in_specs=[cur(0), cur(1), prev, par(conv_w), par(conv_b), par(w_a), par(w_x),
                  par(b_a), par(b_x), par(lam), par(w_out)],
        out_specs=prev,
        out_shape=jax.ShapeDtypeStruct((batch * seq, d), F32),
        scratch_shapes=[pltpu.VMEM((SUBLANES, width), F32), pltpu.VMEM((SUBLANES, width), F32),
                        pltpu.VMEM((2, tt, width), BF16)],
        compiler_params=_compiler_params(("parallel", "arbitrary"), 56),
        name="rglru",
    )(proj, proj, x, conv_w, conv_b, w_a, w_x, b_a, b_x, lam, w_out)


def _pad_cols(w, n):
    return jnp.pad(w, ((0, 0), (0, n - w.shape[1])))


def _pad_rows(w, n):
    return jnp.pad(w, ((0, n - w.shape[0]), (0, 0)))


def kernel(x, norm_mix_g, norm_mlp_g, norm_f_g, mlp_w1, mlp_w2, hy_w_in, hy_w_out, rw_mu_rkv, rw_mu_wag, rw_w0, rw_w1, rw_w2, rw_a0, rw_a1, rw_a2, rw_g1, rw_g2, rw_k_k, rw_k_a, rw_r_k, rw_ln_w, rw_ln_b, s5_lam_re, s5_lam_im, s5_log_dt, s5_b_re, s5_b_im, s5_c_re, s5_c_im, s5_d, s5_glu_w, s5_glu_b, rg_w_in, rg_w_out, rg_conv_w, rg_conv_b, rg_w_a, rg_b_a, rg_w_x, rg_b_x, rg_lam):
    batch, seq, d = x.shape
    m = batch * seq
    rw = rw_w2.shape[-1]
    row = lambda v: v.reshape(1, -1)
    bf = lambda w: w.astype(BF16)
    lane = 128
    xf = x.reshape(m, d)

    lw, a, gate, proj = _rwkv_lora(
        xf, row(norm_mix_g[0]), rw_mu_wag[0],
        bf(_pad_cols(rw_w1[0], lane)), bf(_pad_cols(rw_a1[0], lane)), bf(rw_g1[0]),
        bf(_pad_rows(rw_w2[0], lane)), bf(_pad_rows(rw_a2[0], lane)), bf(rw_g2[0]),
        row(rw_w0[0]), row(rw_a0[0]), bf(hy_w_in[0]), seq=seq, tm=256)
    ya = _rwkv_core(proj, lw, a, gate, rw_mu_rkv[0], row(rw_k_k[0]), row(rw_k_a[0]),
                    row(rw_r_k[0]), row(rw_ln_w[0]), row(rw_ln_b[0]), batch=batch, seq=seq)

    mats = _s5_matrices(s5_lam_re[0], s5_lam_im[0], s5_log_dt[0], s5_b_re[0], s5_b_im[0],
                        s5_c_re[0], s5_c_im[0], seq=seq)
    ys = _s5_core(proj, *mats, batch=batch, seq=seq, width=s5_d.shape[-1])

    x1 = _mix_out(xf, ya, ys, proj, row(s5_d[0]), bf(s5_glu_w[0]), row(s5_glu_b[0]),
                  bf(hy_w_out[0][:rw]), bf(hy_w_out[0][rw:]), tm=512)
    x2, hn1 = _mlp(x1, row(norm_mlp_g[0]), bf(mlp_w1[0]), bf(mlp_w2[0]), row(norm_mix_g[1]),
                   norm_mode="next", tm=512, tf=1024)

    proj1 = _matmul(hn1, bf(rg_w_in[0]), tm=1024, tn=1024)
    x3 = _rglru(proj1, x2, rg_conv_w[0], row(rg_conv_b[0]), bf(rg_w_a[0]), bf(rg_w_x[0]),
                row(rg_b_a[0]), row(rg_b_x[0]), row(rg_lam[0]), bf(rg_w_out[0]),
                batch=batch, seq=seq, tt=256)
    (out,) = _mlp(x3, row(norm_mlp_g[1]), bf(mlp_w1[1]), bf(mlp_w2[1]), row(norm_f_g),
                  norm_mode="final", tm=512, tf=1024)
    return out.reshape(batch, seq, d)
```

```python
import functools
import math

import jax
import jax.numpy as jnp
from jax import lax
from jax.experimental import pallas as pl
from jax.experimental.pallas import tpu as pltpu

F32 = jnp.float32
BF16 = jnp.bfloat16

NORM_EPS = 1e-6
RWKV_HEAD = 64
RWKV_GN_EPS = 64e-5
RWKV_CHUNK = 64
RWKV_SEQS = 2
HEAD_PAIR = 2 * RWKV_HEAD
S5_GROUP = 16
S5_STATE = 64
S5_CHUNK = 16
S5_SLAB = 256
S5_PART_GROUPS = 2
LRU_C = 8.0
CONV_WIDTH = 4
SUBLANES = 8
LANES = 128
V7X_VMEM_BYTES = 64 * 1024 * 1024


def _compiler_params(semantics, vmem_mib):
    assert vmem_mib * 1024 * 1024 < V7X_VMEM_BYTES
    return pltpu.CompilerParams(dimension_semantics=semantics,
                                vmem_limit_bytes=vmem_mib * 1024 * 1024)


def _rms(x, g):
    return x * lax.rsqrt(jnp.mean(x * x, axis=-1, keepdims=True) + NORM_EPS) * g


def _gelu_tanh(x):
    c = math.sqrt(2.0 / math.pi)
    return 0.5 * x * (1.0 + jnp.tanh(c * (x + 0.044715 * (x * x * x))))


def _sigmoid(x):
    return 1.0 / (1.0 + jnp.exp(-x))


def _softplus(x):
    return jnp.maximum(x, 0.0) + jnp.log1p(jnp.exp(-jnp.abs(x)))


def _dot(a, b):
    return jnp.dot(a, b, preferred_element_type=F32)


def _dot_nt(a, b):
    return lax.dot_general(a, b, (((1,), (1,)), ((), ())), preferred_element_type=F32)


def _dot_tn(a, b):
    return lax.dot_general(a, b, (((0,), (0,)), ((), ())), preferred_element_type=F32)


def _shift_rows(x, prev_row):
    row = lax.broadcasted_iota(jnp.int32, x.shape, 0)
    return jnp.where(row == 0, prev_row, pltpu.roll(x, 1, axis=0))


def _rg_in_proj_kernel(h_ref, w_ref, o_ref, *, n_gate):
    j = pl.program_id(1)
    sub = 2 * LANES

    @pl.when(j < n_gate)
    def _():
        h = h_ref[...]
        for n in range(o_ref.shape[1] // sub):
            cols = slice(n * sub, (n + 1) * sub)
            o_ref[:, cols] = _gelu_tanh(_dot(h, w_ref[:, cols]))

    @pl.when(j >= n_gate)
    def _():
        o_ref[...] = _dot(h_ref[...], w_ref[...])


def _rg_in_proj(h, w, *, tm, tn):
    m, d = h.shape
    n = w.shape[1]
    return pl.pallas_call(
        functools.partial(_rg_in_proj_kernel, n_gate=(n // 2) // tn),
        grid=(m // tm, n // tn),
        in_specs=[pl.BlockSpec((tm, d), lambda i, j: (i, 0)),
                  pl.BlockSpec((d, tn), lambda i, j: (0, j))],
        out_specs=pl.BlockSpec((tm, tn), lambda i, j: (i, j)),
        out_shape=jax.ShapeDtypeStruct((m, n), F32),
        compiler_params=_compiler_params(("parallel", "arbitrary"), 40),
        name="in_proj",
    )(h, w)


def _lora_kernel(x_ref, xp_ref, g_ref, mu_ref, w1_ref, a1_ref, g1_ref, w2_ref, a2_ref, g2_ref,
                 w0_ref, a0_ref, win_ref, lw_ref, a_ref, gate_ref, proj_ref, *, blocks_per_seq):
    g = g_ref[...]
    h = _rms(x_ref[...], g)
    proj_ref[...] = _dot(h.astype(BF16), win_ref[...])
    hp = _rms(xp_ref[...], g)
    first = (pl.program_id(0) % blocks_per_seq) == 0
    prev_row = jnp.where(first, 0.0, hp[SUBLANES - 1:SUBLANES, :])
    dh = _shift_rows(h, prev_row) - h
    xw = (h + dh * mu_ref[0:1, :]).astype(BF16)
    xa = (h + dh * mu_ref[1:2, :]).astype(BF16)
    xg = (h + dh * mu_ref[2:3, :]).astype(BF16)
    tw = jnp.tanh(_dot(xw, w1_ref[...])).astype(BF16)
    w_pre = w0_ref[...] + _dot(tw, w2_ref[...])
    w_log = -_softplus(-w_pre) - 0.5
    lw_ref[...] = -jnp.exp(w_log)
    ta = _dot(xa, a1_ref[...]).astype(BF16)
    a_ref[...] = _sigmoid(a0_ref[...] + _dot(ta, a2_ref[...]))
    tg = _sigmoid(_dot(xg, g1_ref[...])).astype(BF16)
    gate_ref[...] = _dot(tg, g2_ref[...])


def _rwkv_lora(x, g, mu_wag, w1, a1, g1, w2, a2, g2, w0, a0, w_in, *, seq, tm):
    m, d = x.shape
    width = w2.shape[1]
    n_proj = w_in.shape[1]
    blocks_per_seq = seq // tm
    full = lambda arr: pl.BlockSpec(arr.shape, lambda i: (0, 0), pipeline_mode=pl.Buffered(1))
    out = jax.ShapeDtypeStruct((m, width), F32)
    return pl.pallas_call(
        functools.partial(_lora_kernel, blocks_per_seq=blocks_per_seq),
        grid=(m // tm,),
        in_specs=[pl.BlockSpec((tm, d), lambda i: (i, 0)),
                  pl.BlockSpec((SUBLANES, d), lambda i: (jnp.maximum(i * (tm // SUBLANES) - 1, 0), 0)),
                  full(g), full(mu_wag), full(w1), full(a1), full(g1), full(w2), full(a2), full(g2),
                  full(w0), full(a0), full(w_in)],
        out_specs=[pl.BlockSpec((tm, width), lambda i: (i, 0))] * 3 + [pl.BlockSpec((tm, n_proj), lambda i: (i, 0))],
        out_shape=[out, out, out, jax.ShapeDtypeStruct((m, n_proj), F32)],
        compiler_params=_compiler_params(("parallel",), 56),
        name="rwkv_lora",
    )(x, x, g, mu_wag, w1, a1, g1, w2, a2, g2, w0, a0, w_in)


def _split3(x):
    hi = x.astype(BF16)
    r1 = x - hi.astype(F32)
    mid = r1.astype(BF16)
    lo = (r1 - mid.astype(F32)).astype(BF16)
    return hi, mid, lo


def _split2(x):
    hi = x.astype(BF16)
    lo = (x - hi.astype(F32)).astype(BF16)
    return hi, lo


def _rwkv_kernel(pr_ref, pk_ref, pv_ref, lw_ref, a_ref, gate_ref, mu_ref, kk_ref, ka_ref, rk_ref,
                 lnw_ref, lnb_ref, o_ref, prev_ref, state_ref):
    n_seq, t, width = o_ref.shape
    rows = n_seq * t
    n_pairs = width // HEAD_PAIR
    seqs = range(n_seq)
    pairs = range(n_pairs)
    chains = [(s, p) for s in seqs for p in pairs]
    slab = lambda x, p: x[:, p * HEAD_PAIR:(p + 1) * HEAD_PAIR]
    seq_rows = lambda x, s: x[s * t:(s + 1) * t]
    lanes = lambda xs: jnp.concatenate(xs, axis=1)
    flat = lambda ref: ref[...].reshape(rows, width)

    @pl.when(pl.program_id(1) == 0)
    def _():
        prev_ref[...] = jnp.zeros_like(prev_ref)
        state_ref[...] = jnp.zeros_like(state_ref)

    row = lax.broadcasted_iota(jnp.int32, (rows, width), 0)

    def per_seq(vals):
        out = vals[-1]
        for s in reversed(range(n_seq - 1)):
            out = jnp.where(row < (s + 1) * t, vals[s], out)
        return jnp.broadcast_to(out, (rows, width))

    def shift_mix(p_ref, idx):
        p = flat(p_ref)
        ps = pltpu.roll(p, 1, axis=0)
        for s in seqs:
            ps = jnp.where(row == s * t, prev_ref[idx, s, SUBLANES - 1:SUBLANES, :], ps)
            prev_ref[idx, s] = p[(s + 1) * t - SUBLANES:(s + 1) * t, :]
        return p + (ps - p) * mu_ref[idx:idx + 1, :]

    r_all = shift_mix(pr_ref, 0)
    k_all = shift_mix(pk_ref, 1)
    v_all = shift_mix(pv_ref, 2)
    a_all = flat(a_ref)
    lw_all = flat(lw_ref)

    ri = lax.broadcasted_iota(jnp.int32, (rows, rows), 0)
    ci = lax.broadcasted_iota(jnp.int32, (rows, rows), 1)
    tril = jnp.where((ci <= ri) & (ci // t == ri // t), 1.0, 0.0).astype(BF16)
    cum_all = _dot(jnp.concatenate([tril, tril, tril], axis=1),
                   jnp.concatenate(_split3(lw_all), axis=0))

    si = lax.broadcasted_iota(jnp.int32, (2 * HEAD_PAIR, HEAD_PAIR), 0) % HEAD_PAIR // RWKV_HEAD
    sj = lax.broadcasted_iota(jnp.int32, (2 * HEAD_PAIR, HEAD_PAIR), 1) // RWKV_HEAD
    seg2 = jnp.where(si == sj, 1.0, 0.0).astype(BF16)

    def head_sum_all(x):
        xh, xl = _split2(x)
        stacked = jnp.concatenate([lanes([slab(xh, p), slab(xl, p)]) for p in pairs], axis=0)
        sums = _dot(stacked, seg2)
        return lanes([sums[p * rows:(p + 1) * rows] for p in pairs])

    head0 = (lax.broadcasted_iota(jnp.int32, (t, width), 1) % HEAD_PAIR) < RWKV_HEAD

    def expand(x, s):
        xb = seq_rows(x, s).astype(BF16)
        zero = jnp.zeros_like(xb)
        return jnp.concatenate([jnp.where(head0, xb, zero), jnp.where(head0, zero, xb)], axis=0)

    bi = lax.broadcasted_iota(jnp.int32, (2 * t, 2 * t), 0) % t
    bj = lax.broadcasted_iota(jnp.int32, (2 * t, 2 * t), 1) % t
    strict = bj < bi
    incl = bj <= bi
    eye = jnp.where(lax.broadcasted_iota(jnp.int32, (2 * t, 2 * t), 0)
                    == lax.broadcasted_iota(jnp.int32, (2 * t, 2 * t), 1), 1.0, 0.0)

    kk = k_all * kk_ref[...]
    kk = kk / jnp.maximum(jnp.sqrt(head_sum_all(kk * kk)), 1e-12)
    k_all = k_all * (1.0 + (a_all - 1.0) * ka_ref[...])
    b_all = kk * a_all
    cum_last = [cum_all[(s + 1) * t - 1:(s + 1) * t, :] for s in seqs]
    cum_t = per_seq(cum_last)
    p_inv = jnp.exp(-cum_all)
    d_rest = jnp.exp(cum_t - cum_all)
    a_t = kk * jnp.exp(cum_all - lw_all)
    r_t = r_all * jnp.exp(cum_all)
    k_t, b_t = k_all * p_inv, b_all * p_inv
    k_end, b_end = k_all * d_rest, -(b_all * d_rest)
    ar = [jnp.concatenate([expand(a_t, s), expand(r_t, s)], axis=0) for s in seqs]
    kb = [jnp.concatenate([expand(k_t, s), expand(b_t, s)], axis=0) for s in seqs]
    kbt = [jnp.concatenate([expand(k_end, s), expand(b_end, s)], axis=0) for s in seqs]
    v_e = [expand(v_all, s) for s in seqs]
    decay_t = [jnp.exp(c) for c in cum_last]

    states = [state_ref[s, p] for s, p in chains]
    scores = [_dot_nt(slab(ar[s], p), slab(kb[s], p)) for s, p in chains]
    ar_h = [_dot_nt(slab(ar[s], p), st.astype(BF16)) for (s, p), st in zip(chains, states)]
    n_ab = [jnp.where(strict, sc[:2 * t, 2 * t:], 0.0) for sc in scores]
    a_ak = [jnp.where(strict, sc[:2 * t, :2 * t], 0.0).astype(BF16) for sc in scores]
    a_r = [jnp.concatenate([jnp.where(incl, sc[2 * t:, :2 * t], 0.0),
                            jnp.where(incl, -sc[2 * t:, 2 * t:], 0.0)], axis=1).astype(BF16)
           for sc in scores]
    rhs = [(ar_h[i][:2 * t] + _dot(a_ak[i], slab(v_e[s], p))).astype(BF16)
           for i, (s, p) in enumerate(chains)]

    inv = [eye - n for n in n_ab]
    n_pow = [n.astype(BF16) for n in n_ab]
    n_pow = [_dot(n, n).astype(BF16) for n in n_pow]
    n_levels = int(math.log2(t)) - 1
    for level in range(n_levels):
        nxt = [_dot(n, n).astype(BF16) for n in n_pow] if level + 1 < n_levels else None
        inv = [x + _dot(x.astype(BF16), n) for x, n in zip(inv, n_pow)]
        n_pow = nxt

    u_e = [_dot(x.astype(BF16), rh).astype(BF16) for x, rh in zip(inv, rhs)]
    vu = [jnp.concatenate([slab(v_e[s], p), u_e[i]], axis=0) for i, (s, p) in enumerate(chains)]
    o_e = [ar_h[i][2 * t:] + _dot(a_r[i], vu[i]) for i in range(len(chains))]
    for i, (s, p) in enumerate(chains):
        state_ref[s, p] = states[i] * slab(decay_t[s], p) + _dot_tn(vu[i], slab(kbt[s], p))
    o = jnp.concatenate(
        [lanes([o_e[s * n_pairs + p][:t] + o_e[s * n_pairs + p][t:] for p in pairs]) for s in seqs],
        axis=0)

    inv_n = 1.0 / RWKV_HEAD
    mean = head_sum_all(o) * inv_n
    d = o - mean
    var = head_sum_all(d * d) * inv_n
    y = d * lax.rsqrt(var + RWKV_GN_EPS) * lnw_ref[...] + lnb_ref[...]
    y = y + head_sum_all(r_all * k_all * rk_ref[...]) * v_all
    o_ref[...] = (y * flat(gate_ref)).reshape(n_seq, t, width)


def _rwkv_core(proj, lw, a, gate, mu_rkv, k_k, k_a, r_k, ln_w, ln_b, *, batch, seq):
    width = lw.shape[1]
    t, n_seq = RWKV_CHUNK, RWKV_SEQS
    assert batch % n_seq == 0 and seq % t == 0
    nc = seq // t
    as_seqs = lambda arr: arr.reshape(batch, seq, arr.shape[1])
    blk = lambda col: pl.BlockSpec((n_seq, t, width), lambda b, c: (b, c, col))
    par = lambda arr: pl.BlockSpec(arr.shape, lambda b, c: (0, 0))
    proj3 = as_seqs(proj)
    out = pl.pallas_call(
        _rwkv_kernel,
        grid=(batch // n_seq, nc),
        in_specs=[blk(0), blk(1), blk(2), blk(0), blk(0), blk(0),
                  par(mu_rkv), par(k_k), par(k_a), par(r_k), par(ln_w), par(ln_b)],
        out_specs=blk(0),
        out_shape=jax.ShapeDtypeStruct((batch, seq, width), F32),
        scratch_shapes=[pltpu.VMEM((3, n_seq, SUBLANES, width), F32),
                        pltpu.VMEM((n_seq, width // HEAD_PAIR, HEAD_PAIR, HEAD_PAIR), F32)],
        compiler_params=_compiler_params(("parallel", "arbitrary"), 32),
        name="rwkv_core",
    )(proj3, proj3, proj3, as_seqs(lw), as_seqs(a), as_seqs(gate), mu_rkv, k_k, k_a, r_k, ln_w, ln_b)
    return out.reshape(batch * seq, width)


def _s5_kernel(u_ref, fir_ref, bmat_ref, cmat_ref, pw_ref, y_ref,
               nat_ref, mid_ref, ud_ref, bu_ref, sr_ref, yd_ref):
    seq = u_ref.shape[0]
    t = S5_CHUNK
    nc = seq // t
    quarter = seq // 4
    n_lane_halves = u_ref.shape[1] // LANES
    n_state = bmat_ref.shape[2] // 2

    for h in range(n_lane_halves):
        lanes = slice(h * LANES, (h + 1) * LANES)
        nat_ref[h] = u_ref[:, lanes]
        for q in range(4):
            mid_ref[h, q * quarter:(q + 1) * quarter, :] = nat_ref[h, pl.ds(q, quarter, stride=4), :]
        for q in range(4):
            for r in range(4):
                pos = 4 * r + q
                ud_ref[pos * nc:(pos + 1) * nc, lanes] = (
                    mid_ref[h, pl.ds(q * quarter + r, nc, stride=4), :].astype(BF16))

    yd_ref[...] = _dot(ud_ref[...], fir_ref[0])
    for tau in range(1, t):
        yd_ref[tau * nc:, :] += _dot(ud_ref[0:(t - tau) * nc, :], fir_ref[tau])

    n_parts = bmat_ref.shape[0]
    for part in range(n_parts):
        bu_ref[part] = _dot(ud_ref[...], bmat_ref[part])
    row = lax.broadcasted_iota(jnp.int32, (nc, LANES), 0)
    cmul = lambda x, y: (x[0] * y[0] - x[1] * y[1], x[0] * y[1] + x[1] * y[0])
    for part in range(n_parts):
        for lc in range(n_state // LANES):
            re = slice(lc * LANES, (lc + 1) * LANES)
            im = slice(n_state + lc * LANES, n_state + (lc + 1) * LANES)
            pw = lambda k: (pw_ref[part, k:k + 1, re], pw_ref[part, k:k + 1, im])
            lam = pw(0)
            z = (bu_ref[part, 0:nc, re], bu_ref[part, 0:nc, im])
            for pos in range(1, t):
                rows = slice(pos * nc, (pos + 1) * nc)
                lz = cmul(lam, z)
                z = (lz[0] + bu_ref[part, rows, re], lz[1] + bu_ref[part, rows, im])
            shift, level = 1, 0
            while shift < nc:
                valid = row >= shift
                prev = (jnp.where(valid, pltpu.roll(z[0], shift, axis=0), 0.0),
                        jnp.where(valid, pltpu.roll(z[1], shift, axis=0), 0.0))
                step = cmul(pw(t + level), prev)
                z = (z[0] + step[0], z[1] + step[1])
                shift *= 2
                level += 1
            first = row >= 1
            s_in = (jnp.where(first, pltpu.roll(z[0], 1, axis=0), 0.0),
                    jnp.where(first, pltpu.roll(z[1], 1, axis=0), 0.0))
            for pos in range(t):
                rows = slice(pos * nc, (pos + 1) * nc)
                rot = cmul(pw(pos), s_in)
                sr_ref[part, rows, re] = rot[0].astype(BF16)
                sr_ref[part, rows, im] = rot[1].astype(BF16)
        yd_ref[...] += _dot(sr_ref[part], cmat_ref[part])

    for h in range(n_lane_halves):
        lanes = slice(h * LANES, (h + 1) * LANES)
        for q in range(4):
            for r in range(4):
                pos = 4 * r + q
                mid_ref[h, pl.ds(q * quarter + r, nc, stride=4), :] = yd_ref[pos * nc:(pos + 1) * nc, lanes]
        for q in range(4):
            nat_ref[h, pl.ds(q, quarter, stride=4), :] = mid_ref[h, q * quarter:(q + 1) * quarter, :]
        y_ref[:, lanes] = nat_ref[h]


def _s5_core(proj, fir, bmat, cmat, pw, *, batch, seq, width):
    n_slabs = width // S5_SLAB
    u_col0 = (proj.shape[1] - width) // S5_SLAB
    wblk = lambda arr: pl.BlockSpec((None,) + arr.shape[1:], lambda b, s: (s,) + (0,) * (arr.ndim - 1))
    n_parts, n_state2 = bmat.shape[1], bmat.shape[-1]
    return pl.pallas_call(
        _s5_kernel,
        grid=(batch, n_slabs),
        in_specs=[pl.BlockSpec((seq, S5_SLAB), lambda b, s: (b, u_col0 + s)),
                  wblk(fir), wblk(bmat), wblk(cmat), wblk(pw)],
        out_specs=pl.BlockSpec((seq, S5_SLAB), lambda b, s: (b, s)),
        out_shape=jax.ShapeDtypeStruct((batch * seq, width), F32),
        scratch_shapes=[pltpu.VMEM((S5_SLAB // LANES, seq, LANES), F32),
                        pltpu.VMEM((S5_SLAB // LANES, seq, LANES), F32),
                        pltpu.VMEM((seq, S5_SLAB), BF16),
                        pltpu.VMEM((n_parts, seq, n_state2), F32),
                        pltpu.VMEM((n_parts, seq, n_state2), BF16),
                        pltpu.VMEM((seq, S5_SLAB), F32)],
        compiler_params=_compiler_params(("parallel", "arbitrary"), 56),
        name="s5_core",
    )(proj, fir, bmat, cmat, pw)


def _s5_matrices(lam_re, lam_im, log_dt, b_re, b_im, c_re, c_im, *, seq):
    t = S5_CHUNK
    groups = lam_re.shape[0]
    gs = S5_SLAB // S5_GROUP
    gp = S5_PART_GROUPS
    n_parts = gs // gp
    n_slabs = groups // gs
    dt = jnp.exp(log_dt)[:, None]
    mag = jnp.exp(lam_re * dt)
    lb_re, lb_im = mag * jnp.cos(lam_im * dt), mag * jnp.sin(lam_im * dt)
    den = lam_re * lam_re + lam_im * lam_im
    z_re = ((lb_re - 1.0) * lam_re + lb_im * lam_im) / den
    z_im = (lb_im * lam_re - (lb_re - 1.0) * lam_im) / den
    bb_re = z_re[..., None] * b_re - z_im[..., None] * b_im
    bb_im = z_re[..., None] * b_im + z_im[..., None] * b_re

    def mul(carry, _):
        pr, pi = carry
        return (pr * lb_re - pi * lb_im, pr * lb_im + pi * lb_re), (pr, pi)

    _, (pw_re, pw_im) = lax.scan(mul, (jnp.ones_like(lb_re), jnp.zeros_like(lb_re)), None, length=t + 1)

    cl_re = pw_re[:t, :, None, :] * c_re[None] - pw_im[:t, :, None, :] * c_im[None]
    cl_im = pw_re[:t, :, None, :] * c_im[None] + pw_im[:t, :, None, :] * c_re[None]
    taps = jnp.einsum('tghp,gpi->tghi', cl_re, bb_re) - jnp.einsum('tghp,gpi->tghi', cl_im, bb_im)
    fir = jnp.einsum('tsgoi,gk->stgiko', taps.reshape(t, n_slabs, gs, S5_GROUP, S5_GROUP),
                     jnp.eye(gs, dtype=F32)).reshape(n_slabs, t, S5_SLAB, S5_SLAB)

    eye_g = jnp.eye(gp, dtype=F32)
    eye_p = jnp.eye(n_parts, dtype=F32)

    def b_slab(bb):
        blk = jnp.einsum('abgph,gk->abghkp', bb.reshape(n_slabs, n_parts, gp, S5_STATE, S5_GROUP), eye_g)
        blk = blk.reshape(n_slabs, n_parts, gp * S5_GROUP, gp * S5_STATE)
        return (eye_p[None, :, :, None, None] * blk[:, :, None]).reshape(n_slabs, n_parts, S5_SLAB, gp * S5_STATE)

    def c_slab(cc):
        blk = jnp.einsum('abghp,gk->abgpkh', cc.reshape(n_slabs, n_parts, gp, S5_GROUP, S5_STATE), eye_g)
        blk = blk.reshape(n_slabs, n_parts, gp * S5_STATE, gp * S5_GROUP)
        return (eye_p[None, :, None, :, None] * blk[:, :, :, None]).reshape(n_slabs, n_parts, gp * S5_STATE, S5_SLAB)

    bmat = jnp.concatenate([b_slab(bb_re), b_slab(bb_im)], axis=-1)
    cmat = jnp.concatenate([c_slab(c_re), -c_slab(c_im)], axis=2)

    squares = [(pw_re[t], pw_im[t])]
    for _ in range((seq // t - 1).bit_length() - 1):
        sr, si = squares[-1]
        squares.append((sr * sr - si * si, 2.0 * sr * si))
    tab_re = jnp.concatenate([pw_re[1:], jnp.stack([s[0] for s in squares])], axis=0)
    tab_im = jnp.concatenate([pw_im[1:], jnp.stack([s[1] for s in squares])], axis=0)

    def state_lanes(x):
        return jnp.transpose(x.reshape(x.shape[0], n_slabs, n_parts, gp * S5_STATE), (1, 2, 0, 3))

    pw = jnp.concatenate([state_lanes(tab_re), state_lanes(tab_im)], axis=-1)
    return fir.astype(BF16), bmat.astype(BF16), cmat.astype(BF16), pw


def _mix_out_kernel(x_ref, ya_ref, ys_ref, u_ref, d_ref, gw_ref, gb_ref, wa_ref, wb_ref, o_ref):
    yb = _gelu_tanh(ys_ref[...] + d_ref[...] * u_ref[...])
    yb = yb * _sigmoid(_dot(yb.astype(BF16), gw_ref[...]) + gb_ref[...])
    o_ref[...] = (x_ref[...] + _dot(ya_ref[...].astype(BF16), wa_ref[...])
                  + _dot(yb.astype(BF16), wb_ref[...]))


def _mix_out(x, ya, ys, proj, d_skip, glu_w, glu_b, w_out_a, w_out_b, *, tm):
    m, d = x.shape
    width = ya.shape[1]
    u_col = proj.shape[1] // width - 1
    full = lambda arr: pl.BlockSpec(arr.shape, lambda i: (0, 0), pipeline_mode=pl.Buffered(1))
    row = lambda w, col=0: pl.BlockSpec((tm, w), lambda i: (i, col))
    return pl.pallas_call(
        _mix_out_kernel,
        grid=(m // tm,),
        in_specs=[row(d), row(width), row(width), row(width, u_col), full(d_skip), full(glu_w),
                  full(glu_b), full(w_out_a), full(w_out_b)],
        out_specs=row(d),
        out_shape=jax.ShapeDtypeStruct((m, d), F32),
        compiler_params=_compiler_params(("parallel",), 56),
        name="mix_out",
    )(x, ya, ys, proj, d_skip, glu_w, glu_b, w_out_a, w_out_b)


def _mlp_kernel(x_ref, g_ref, w1_ref, w2_ref, gn_ref, o_ref, *rest, norm_mode):
    if norm_mode == "next":
        hn_ref, h_ref = rest
    else:
        (h_ref,) = rest
    f = pl.program_id(1)

    @pl.when(f == 0)
    def _():
        x = x_ref[...]
        h_ref[...] = _rms(x, g_ref[...]).astype(BF16)
        o_ref[...] = x

    hid = jnp.maximum(_dot(h_ref[...], w1_ref[...]), 0.0)
    o_ref[...] += _dot((hid * hid).astype(BF16), w2_ref[...])

    @pl.when(f == pl.num_programs(1) - 1)
    def _():
        normed = _rms(o_ref[...], gn_ref[...])
        if norm_mode == "next":
            hn_ref[...] = normed.astype(BF16)
        else:
            o_ref[...] = normed


def _mlp(x, g, w1, w2, gn, *, norm_mode, tm, tf):
    assert norm_mode in ("final", "next")
    m, d = x.shape
    dff = w1.shape[1]
    row = pl.BlockSpec((tm, d), lambda i, f: (i, 0))
    out_specs, out_shape = [row], [jax.ShapeDtypeStruct((m, d), F32)]
    if norm_mode == "next":
        out_specs.append(row)
        out_shape.append(jax.ShapeDtypeStruct((m, d), BF16))
    return pl.pallas_call(
        functools.partial(_mlp_kernel, norm_mode=norm_mode),
        grid=(m // tm, dff // tf),
        in_specs=[row,
                  pl.BlockSpec((1, d), lambda i, f: (0, 0)),
                  pl.BlockSpec((d, tf), lambda i, f: (0, f)),
                  pl.BlockSpec((tf, d), lambda i, f: (f, 0)),
                  pl.BlockSpec((1, d), lambda i, f: (0, 0))],
        out_specs=out_specs,
        out_shape=out_shape,
        scratch_shapes=[pltpu.VMEM((tm, d), BF16)],
        compiler_params=_compiler_params(("parallel", "arbitrary"), 48),
        name="mlp",
    )(x, g, w1, w2, gn)


def _rglru_kernel(gate_ref, xr_ref, x_ref, cw_ref, cb_ref, wa_ref, wx_ref, ba_ref, bx_ref, lam_ref,
                  wout_ref, o_ref, tail_ref, h_ref, y_ref):
    c = pl.program_id(1)
    n_blocks = pl.num_programs(1) - 1
    n_heads, blk = wa_ref.shape[0], wa_ref.shape[1]
    old, new = (c + 1) % 2, c % 2

    @pl.when(c == 0)
    def _():
        tail_ref[...] = jnp.zeros_like(tail_ref)
        h_ref[...] = jnp.zeros_like(h_ref)
        y_ref[...] = jnp.zeros_like(y_ref)

    def project(h):
        cols = slice(h * blk, (h + 1) * blk)
        o_ref[:, cols] = x_ref[:, cols] + _dot(y_ref[old], wout_ref[:, cols])

    @pl.when(c < n_blocks)
    def _():
        for h in range(n_heads):
            project(h)
            cols = slice(h * blk, (h + 1) * blk)
            y_ref[new, :, cols] = _rglru_head(
                gate_ref, xr_ref, cw_ref, cb_ref, wa_ref, wx_ref, ba_ref, bx_ref, lam_ref,
                tail_ref, h_ref, h, c == 0).astype(BF16)

    @pl.when(c == n_blocks)
    def _():
        for h in range(n_heads):
            project(h)


def _rglru_head(gate_ref, xr_ref, cw_ref, cb_ref, wa_ref, wx_ref, ba_ref, bx_ref, lam_ref,
                tail_ref, h_ref, head, is_first):
    tt = xr_ref.shape[0]
    blk = wa_ref.shape[1]
    cols = slice(head * blk, (head + 1) * blk)

    xr = xr_ref[:, cols]
    tail = tail_ref[:, cols]
    row8 = lax.broadcasted_iota(jnp.int32, (SUBLANES, blk), 0)
    xc = xr * cw_ref[CONV_WIDTH - 1:CONV_WIDTH, cols] + cb_ref[:, cols]
    for s in range(1, CONV_WIDTH):
        xs = pltpu.roll(xr, s, axis=0)
        top = jnp.where(row8 < s, pltpu.roll(tail, s, axis=0), xs[:SUBLANES])
        xs = jnp.concatenate([top, xs[SUBLANES:]], axis=0)
        xc = xc + xs * cw_ref[CONV_WIDTH - 1 - s:CONV_WIDTH - s, cols]
    tail_ref[:, cols] = xr[tt - SUBLANES:tt, :]

    xcb = xc.astype(BF16)
    gate_r = _sigmoid(_dot(xcb, wa_ref[head]) + ba_ref[:, cols])
    gate_i = _sigmoid(_dot(xcb, wx_ref[head]) + bx_ref[:, cols])
    log_a = LRU_C * gate_r * (-_softplus(-lam_ref[:, cols]))
    a = jnp.exp(log_a)
    rest = 1.0 - a * a
    mult = jnp.where(rest > 0.0, rest * lax.rsqrt(rest), 0.0)
    top = jnp.where((row8 == 0) & is_first, 1.0, mult[:SUBLANES])
    mult = jnp.concatenate([top, mult[SUBLANES:]], axis=0)
    b = xc * gate_i * mult

    n_groups = tt // SUBLANES
    a3 = a.reshape(n_groups, SUBLANES, blk)
    b3 = b.reshape(n_groups, SUBLANES, blk)
    sub = lax.broadcasted_iota(jnp.int32, a3.shape, 1)
    shift = 1
    while shift < SUBLANES:
        valid = sub >= shift
        b3 = b3 + a3 * jnp.where(valid, pltpu.roll(b3, shift, axis=1), 0.0)
        a3 = a3 * jnp.where(valid, pltpu.roll(a3, shift, axis=1), 1.0)
        shift *= 2
    carry = h_ref[0:1, cols]
    groups = []
    for i in range(n_groups):
        h_i = b3[i] + a3[i] * carry
        groups.append(h_i)
        carry = h_i[SUBLANES - 1:SUBLANES, :]
    h_ref[:, cols] = jnp.broadcast_to(carry, (SUBLANES, blk))
    return gate_ref[:, cols] * jnp.concatenate(groups, axis=0)


def _rglru(proj, x, conv_w, conv_b, w_a, w_x, b_a, b_x, lam, w_out, *, batch, seq, tt):
    width = lam.shape[1]
    d = x.shape[1]
    nc = seq // tt
    cur = lambda col: pl.BlockSpec((tt, width), lambda b, c: (b * nc + jnp.minimum(c, nc - 1), col))
    prev = pl.BlockSpec((tt, d), lambda b, c: (b * nc + jnp.maximum(c - 1, 0), 0))
    par = lambda arr: pl.BlockSpec(arr.shape, lambda b, c: (0,) * arr.ndim, pipeline_mode=pl.Buffered(1))
    return pl.pallas_call(
        _rglru_kernel,
        grid=(batch, nc + 1),
        in_specs=[cur(0), cur(1), prev, par(conv_w), par(conv_b), par(w_a), par(w_x),
                  par(b_a), par(b_x), par(lam), par(w_out)],
        out_specs=prev,
        out_shape=jax.ShapeDtypeStruct((batch * seq, d), F32),
        scratch_shapes=[pltpu.VMEM((SUBLANES, width), F32), pltpu.VMEM((SUBLANES, width), F32),
                        pltpu.VMEM((2, tt, width), BF16)],
        compiler_params=_compiler_params(("parallel", "arbitrary"), 56),
        name="rglru",
    )(proj, proj, x, conv_w, conv_b, w_a, w_x, b_a, b_x, lam, w_out)


def _pad_cols(w, n):
    return jnp.pad(w, ((0, 0), (0, n - w.shape[1])))


def _pad_rows(w, n):
    return jnp.pad(w, ((0, n - w.shape[0]), (0, 0)))


def kernel(x, norm_mix_g, norm_mlp_g, norm_f_g, mlp_w1, mlp_w2, hy_w_in, hy_w_out, rw_mu_rkv, rw_mu_wag, rw_w0, rw_w1, rw_w2, rw_a0, rw_a1, rw_a2, rw_g1, rw_g2, rw_k_k, rw_k_a, rw_r_k, rw_ln_w, rw_ln_b, s5_lam_re, s5_lam_im, s5_log_dt, s5_b_re, s5_b_im, s5_c_re, s5_c_im, s5_d, s5_glu_w, s5_glu_b, rg_w_in, rg_w_out, rg_conv_w, rg_conv_b, rg_w_a, rg_b_a, rg_w_x, rg_b_x, rg_lam):
    batch, seq, d = x.shape
    m = batch * seq
    rw = rw_w2.shape[-1]
    row = lambda v: v.reshape(1, -1)
    bf = lambda w: w.astype(BF16)
    lane = 128
    xf = x.reshape(m, d)

    lw, a, gate, proj = _rwkv_lora(
        xf, row(norm_mix_g[0]), rw_mu_wag[0],
        bf(_pad_cols(rw_w1[0], lane)), bf(_pad_cols(rw_a1[0], lane)), bf(rw_g1[0]),
        bf(_pad_rows(rw_w2[0], lane)), bf(_pad_rows(rw_a2[0], lane)), bf(rw_g2[0]),
        row(rw_w0[0]), row(rw_a0[0]), bf(hy_w_in[0]), seq=seq, tm=256)
    ya = _rwkv_core(proj, lw, a, gate, rw_mu_rkv[0], row(rw_k_k[0]), row(rw_k_a[0]),
                    row(rw_r_k[0]), row(rw_ln_w[0]), row(rw_ln_b[0]), batch=batch, seq=seq)

    mats = _s5_matrices(s5_lam_re[0], s5_lam_im[0], s5_log_dt[0], s5_b_re[0], s5_b_im[0],
                        s5_c_re[0], s5_c_im[0], seq=seq)
    ys = _s5_core(proj, *mats, batch=batch, seq=seq, width=s5_d.shape[-1])

    x1 = _mix_out(xf, ya, ys, proj, row(s5_d[0]), bf(s5_glu_w[0]), row(s5_glu_b[0]),
                  bf(hy_w_out[0][:rw]), bf(hy_w_out[0][rw:]), tm=512)
    x2, hn1 = _mlp(x1, row(norm_mlp_g[0]), bf(mlp_w1[0]), bf(mlp_w2[0]), row(norm_mix_g[1]),
                   norm_mode="next", tm=512, tf=1024)

    proj1 = _rg_in_proj(hn1, bf(rg_w_in[0]), tm=1024, tn=1024)
    x3 = _rglru(proj1, x2, rg_conv_w[0], row(rg_conv_b[0]), bf(rg_w_a[0]), bf(rg_w_x[0]),
                row(rg_b_a[0]), row(rg_b_x[0]), row(rg_lam[0]), bf(rg_w_out[0]),
                batch=batch, seq=seq, tt=256)
    (out,) = _mlp(x3, row(norm_mlp_g[1]), bf(mlp_w1[1]), bf(mlp_w2[1]), row(norm_f_g),
                  norm_mode="final", tm=512, tf=1024)
    return out.reshape(batch, seq, d)
```

```python
import functools
import math

import jax
import jax.numpy as jnp
from jax import lax
from jax.experimental import pallas as pl
from jax.experimental.pallas import tpu as pltpu

F32 = jnp.float32
BF16 = jnp.bfloat16

NORM_EPS = 1e-6
RWKV_HEAD = 64
RWKV_GN_EPS = 64e-5
RWKV_CHUNK = 64
RWKV_SEQS = 2
HEAD_PAIR = 2 * RWKV_HEAD
S5_GROUP = 16
S5_STATE = 64
S5_CHUNK = 16
S5_SLAB = 256
S5_PART_GROUPS = 2
LRU_C = 8.0
CONV_WIDTH = 4
SUBLANES = 8
LANES = 128
V7X_VMEM_BYTES = 64 * 1024 * 1024


def _compiler_params(semantics, vmem_mib):
    assert vmem_mib * 1024 * 1024 < V7X_VMEM_BYTES
    return pltpu.CompilerParams(dimension_semantics=semantics,
                                vmem_limit_bytes=vmem_mib * 1024 * 1024)


def _rms(x, g):
    return x * lax.rsqrt(jnp.mean(x * x, axis=-1, keepdims=True) + NORM_EPS) * g


def _gelu_tanh(x):
    c = math.sqrt(2.0 / math.pi)
    return 0.5 * x * (1.0 + jnp.tanh(c * (x + 0.044715 * (x * x * x))))


def _sigmoid(x):
    return 1.0 / (1.0 + jnp.exp(-x))


def _softplus(x):
    return jnp.maximum(x, 0.0) + jnp.log1p(jnp.exp(-jnp.abs(x)))


def _dot(a, b):
    return jnp.dot(a, b, preferred_element_type=F32)


def _dot_nt(a, b):
    return lax.dot_general(a, b, (((1,), (1,)), ((), ())), preferred_element_type=F32)


def _dot_tn(a, b):
    return lax.dot_general(a, b, (((0,), (0,)), ((), ())), preferred_element_type=F32)


def _shift_rows(x, prev_row):
    row = lax.broadcasted_iota(jnp.int32, x.shape, 0)
    return jnp.where(row == 0, prev_row, pltpu.roll(x, 1, axis=0))


def _rg_in_proj_kernel(h_ref, w_ref, o_ref, *, n_gate):
    j = pl.program_id(1)
    sub = 2 * LANES

    @pl.when(j < n_gate)
    def _():
        h = h_ref[...]
        for n in range(o_ref.shape[1] // sub):
            cols = slice(n * sub, (n + 1) * sub)
            o_ref[:, cols] = _gelu_tanh(_dot(h, w_ref[:, cols]))

    @pl.when(j >= n_gate)
    def _():
        o_ref[...] = _dot(h_ref[...], w_ref[...])


def _rg_in_proj(h, w, *, tm, tn):
    m, d = h.shape
    n = w.shape[1]
    return pl.pallas_call(
        functools.partial(_rg_in_proj_kernel, n_gate=(n // 2) // tn),
        grid=(m // tm, n // tn),
        in_specs=[pl.BlockSpec((tm, d), lambda i, j: (i, 0)),
                  pl.BlockSpec((d, tn), lambda i, j: (0, j))],
        out_specs=pl.BlockSpec((tm, tn), lambda i, j: (i, j)),
        out_shape=jax.ShapeDtypeStruct((m, n), F32),
        compiler_params=_compiler_params(("parallel", "arbitrary"), 40),
        name="in_proj",
    )(h, w)


def _lora_kernel(x_ref, xp_ref, g_ref, mu_ref, w1_ref, a1_ref, g1_ref, w2_ref, a2_ref, g2_ref,
                 w0_ref, a0_ref, win_ref, lw_ref, a_ref, gate_ref, proj_ref, *, blocks_per_seq):
    g = g_ref[...]
    h = _rms(x_ref[...], g)
    proj_ref[...] = _dot(h.astype(BF16), win_ref[...])
    hp = _rms(xp_ref[...], g)
    first = (pl.program_id(0) % blocks_per_seq) == 0
    prev_row = jnp.where(first, 0.0, hp[SUBLANES - 1:SUBLANES, :])
    dh = _shift_rows(h, prev_row) - h
    xw = (h + dh * mu_ref[0:1, :]).astype(BF16)
    xa = (h + dh * mu_ref[1:2, :]).astype(BF16)
    xg = (h + dh * mu_ref[2:3, :]).astype(BF16)
    tw = jnp.tanh(_dot(xw, w1_ref[...])).astype(BF16)
    w_pre = w0_ref[...] + _dot(tw, w2_ref[...])
    w_log = -_softplus(-w_pre) - 0.5
    lw_ref[...] = -jnp.exp(w_log)
    ta = _dot(xa, a1_ref[...]).astype(BF16)
    a_ref[...] = _sigmoid(a0_ref[...] + _dot(ta, a2_ref[...]))
    tg = _sigmoid(_dot(xg, g1_ref[...])).astype(BF16)
    gate_ref[...] = _dot(tg, g2_ref[...])


def _rwkv_lora(x, g, mu_wag, w1, a1, g1, w2, a2, g2, w0, a0, w_in, *, seq, tm):
    m, d = x.shape
    width = w2.shape[1]
    n_proj = w_in.shape[1]
    blocks_per_seq = seq // tm
    full = lambda arr: pl.BlockSpec(arr.shape, lambda i: (0, 0), pipeline_mode=pl.Buffered(1))
    out = jax.ShapeDtypeStruct((m, width), F32)
    return pl.pallas_call(
        functools.partial(_lora_kernel, blocks_per_seq=blocks_per_seq),
        grid=(m // tm,),
        in_specs=[pl.BlockSpec((tm, d), lambda i: (i, 0)),
                  pl.BlockSpec((SUBLANES, d), lambda i: (jnp.maximum(i * (tm // SUBLANES) - 1, 0), 0)),
                  full(g), full(mu_wag), full(w1), full(a1), full(g1), full(w2), full(a2), full(g2),
                  full(w0), full(a0), full(w_in)],
        out_specs=[pl.BlockSpec((tm, width), lambda i: (i, 0))] * 3 + [pl.BlockSpec((tm, n_proj), lambda i: (i, 0))],
        out_shape=[out, out, out, jax.ShapeDtypeStruct((m, n_proj), F32)],
        compiler_params=_compiler_params(("parallel",), 56),
        name="rwkv_lora",
    )(x, x, g, mu_wag, w1, a1, g1, w2, a2, g2, w0, a0, w_in)


def _split3(x):
    hi = x.astype(BF16)
    r1 = x - hi.astype(F32)
    mid = r1.astype(BF16)
    lo = (r1 - mid.astype(F32)).astype(BF16)
    return hi, mid, lo


def _split2(x):
    hi = x.astype(BF16)
    lo = (x - hi.astype(F32)).astype(BF16)
    return hi, lo


def _rwkv_kernel(pr_ref, pk_ref, pv_ref, lw_ref, a_ref, gate_ref, mu_ref, kk_ref, ka_ref, rk_ref,
                 lnw_ref, lnb_ref, o_ref, prev_ref, state_ref):
    n_seq, t, width = o_ref.shape
    rows = n_seq * t
    n_pairs = width // HEAD_PAIR
    seqs = range(n_seq)
    pairs = range(n_pairs)
    chains = [(s, p) for s in seqs for p in pairs]
    slab = lambda x, p: x[:, p * HEAD_PAIR:(p + 1) * HEAD_PAIR]
    seq_rows = lambda x, s: x[s * t:(s + 1) * t]
    lanes = lambda xs: jnp.concatenate(xs, axis=1)
    flat = lambda ref: ref[...].reshape(rows, width)

    @pl.when(pl.program_id(1) == 0)
    def _():
        prev_ref[...] = jnp.zeros_like(prev_ref)
        state_ref[...] = jnp.zeros_like(state_ref)

    row = lax.broadcasted_iota(jnp.int32, (rows, width), 0)

    def per_seq(vals):
        out = vals[-1]
        for s in reversed(range(n_seq - 1)):
            out = jnp.where(row < (s + 1) * t, vals[s], out)
        return jnp.broadcast_to(out, (rows, width))

    def shift_mix(p_ref, idx):
        p = flat(p_ref)
        ps = pltpu.roll(p, 1, axis=0)
        for s in seqs:
            ps = jnp.where(row == s * t, prev_ref[idx, s, SUBLANES - 1:SUBLANES, :], ps)
            prev_ref[idx, s] = p[(s + 1) * t - SUBLANES:(s + 1) * t, :]
        return p + (ps - p) * mu_ref[idx:idx + 1, :]

    r_all = shift_mix(pr_ref, 0)
    k_all = shift_mix(pk_ref, 1)
    v_all = shift_mix(pv_ref, 2)
    a_all = flat(a_ref)
    lw_all = flat(lw_ref)

    ri = lax.broadcasted_iota(jnp.int32, (rows, rows), 0)
    ci = lax.broadcasted_iota(jnp.int32, (rows, rows), 1)
    tril = jnp.where((ci <= ri) & (ci // t == ri // t), 1.0, 0.0).astype(BF16)
    cum_all = _dot(jnp.concatenate([tril, tril, tril], axis=1),
                   jnp.concatenate(_split3(lw_all), axis=0))

    si = lax.broadcasted_iota(jnp.int32, (2 * HEAD_PAIR, HEAD_PAIR), 0) % HEAD_PAIR // RWKV_HEAD
    sj = lax.broadcasted_iota(jnp.int32, (2 * HEAD_PAIR, HEAD_PAIR), 1) // RWKV_HEAD
    seg2 = jnp.where(si == sj, 1.0, 0.0).astype(BF16)

    def head_sum_all(x):
        xh, xl = _split2(x)
        stacked = jnp.concatenate([lanes([slab(xh, p), slab(xl, p)]) for p in pairs], axis=0)
        sums = _dot(stacked, seg2)
        return lanes([sums[p * rows:(p + 1) * rows] for p in pairs])

    head0 = (lax.broadcasted_iota(jnp.int32, (t, width), 1) % HEAD_PAIR) < RWKV_HEAD

    def expand(x, s):
        xb = seq_rows(x, s).astype(BF16)
        zero = jnp.zeros_like(xb)
        return jnp.concatenate([jnp.where(head0, xb, zero), jnp.where(head0, zero, xb)], axis=0)

    bi = lax.broadcasted_iota(jnp.int32, (2 * t, 2 * t), 0) % t
    bj = lax.broadcasted_iota(jnp.int32, (2 * t, 2 * t), 1) % t
    strict = bj < bi
    incl = bj <= bi
    eye = jnp.where(lax.broadcasted_iota(jnp.int32, (2 * t, 2 * t), 0)
                    == lax.broadcasted_iota(jnp.int32, (2 * t, 2 * t), 1), 1.0, 0.0)

    kk = k_all * kk_ref[...]
    kk = kk / jnp.maximum(jnp.sqrt(head_sum_all(kk * kk)), 1e-12)
    k_all = k_all * (1.0 + (a_all - 1.0) * ka_ref[...])
    b_all = kk * a_all
    cum_last = [cum_all[(s + 1) * t - 1:(s + 1) * t, :] for s in seqs]
    cum_t = per_seq(cum_last)
    p_inv = jnp.exp(-cum_all)
    d_rest = jnp.exp(cum_t - cum_all)
    a_t = kk * jnp.exp(cum_all - lw_all)
    r_t = r_all * jnp.exp(cum_all)
    k_t, b_t = k_all * p_inv, b_all * p_inv
    k_end, b_end = k_all * d_rest, -(b_all * d_rest)
    ar = [jnp.concatenate([expand(a_t, s), expand(r_t, s)], axis=0) for s in seqs]
    kb = [jnp.concatenate([expand(k_t, s), expand(b_t, s)], axis=0) for s in seqs]
    kbt = [jnp.concatenate([expand(k_end, s), expand(b_end, s)], axis=0) for s in seqs]
    v_e = [expand(v_all, s) for s in seqs]
    decay_t = [jnp.exp(c) for c in cum_last]

    states = [state_ref[s, p] for s, p in chains]
    scores = [_dot_nt(slab(ar[s], p), slab(kb[s], p)) for s, p in chains]
    ar_h = [_dot_nt(slab(ar[s], p), st.astype(BF16)) for (s, p), st in zip(chains, states)]
    n_ab = [jnp.where(strict, sc[:2 * t, 2 * t:], 0.0) for sc in scores]
    a_ak = [jnp.where(strict, sc[:2 * t, :2 * t], 0.0).astype(BF16) for sc in scores]
    a_r = [jnp.concatenate([jnp.where(incl, sc[2 * t:, :2 * t], 0.0),
                            jnp.where(incl, -sc[2 * t:, 2 * t:], 0.0)], axis=1).astype(BF16)
           for sc in scores]
    rhs = [(ar_h[i][:2 * t] + _dot(a_ak[i], slab(v_e[s], p))).astype(BF16)
           for i, (s, p) in enumerate(chains)]

    inv = [eye - n for n in n_ab]
    n_pow = [n.astype(BF16) for n in n_ab]
    n_pow = [_dot(n, n).astype(BF16) for n in n_pow]
    n_levels = int(math.log2(t)) - 1
    for level in range(n_levels):
        nxt = [_dot(n, n).astype(BF16) for n in n_pow] if level + 1 < n_levels else None
        inv = [x + _dot(x.astype(BF16), n) for x, n in zip(inv, n_pow)]
        n_pow = nxt

    u_e = [_dot(x.astype(BF16), rh).astype(BF16) for x, rh in zip(inv, rhs)]
    vu = [jnp.concatenate([slab(v_e[s], p), u_e[i]], axis=0) for i, (s, p) in enumerate(chains)]
    o_e = [ar_h[i][2 * t:] + _dot(a_r[i], vu[i]) for i in range(len(chains))]
    for i, (s, p) in enumerate(chains):
        state_ref[s, p] = states[i] * slab(decay_t[s], p) + _dot_tn(vu[i], slab(kbt[s], p))
    o = jnp.concatenate(
        [lanes([o_e[s * n_pairs + p][:t] + o_e[s * n_pairs + p][t:] for p in pairs]) for s in seqs],
        axis=0)

    inv_n = 1.0 / RWKV_HEAD
    mean = head_sum_all(o) * inv_n
    d = o - mean
    var = head_sum_all(d * d) * inv_n
    y = d * lax.rsqrt(var + RWKV_GN_EPS) * lnw_ref[...] + lnb_ref[...]
    y = y + head_sum_all(r_all * k_all * rk_ref[...]) * v_all
    o_ref[...] = (y * flat(gate_ref)).reshape(n_seq, t, width)


def _rwkv_core(proj, lw, a, gate, mu_rkv, k_k, k_a, r_k, ln_w, ln_b, *, batch, seq):
    width = lw.shape[1]
    t, n_seq = RWKV_CHUNK, RWKV_SEQS
    assert batch % n_seq == 0 and seq % t == 0
    nc = seq // t
    as_seqs = lambda arr: arr.reshape(batch, seq, arr.shape[1])
    blk = lambda col: pl.BlockSpec((n_seq, t, width), lambda b, c: (b, c, col))
    par = lambda arr: pl.BlockSpec(arr.shape, lambda b, c: (0, 0))
    proj3 = as_seqs(proj)
    out = pl.pallas_call(
        _rwkv_kernel,
        grid=(batch // n_seq, nc),
        in_specs=[blk(0), blk(1), blk(2), blk(0), blk(0), blk(0),
                  par(mu_rkv), par(k_k), par(k_a), par(r_k), par(ln_w), par(ln_b)],
        out_specs=blk(0),
        out_shape=jax.ShapeDtypeStruct((batch, seq, width), F32),
        scratch_shapes=[pltpu.VMEM((3, n_seq, SUBLANES, width), F32),
                        pltpu.VMEM((n_seq, width // HEAD_PAIR, HEAD_PAIR, HEAD_PAIR), F32)],
        compiler_params=_compiler_params(("parallel", "arbitrary"), 32),
        name="rwkv_core",
    )(proj3, proj3, proj3, as_seqs(lw), as_seqs(a), as_seqs(gate), mu_rkv, k_k, k_a, r_k, ln_w, ln_b)
    return out.reshape(batch * seq, width)


def _s5_kernel(u_ref, fir_ref, bmat_ref, cmat_ref, pw_ref, y_ref,
               nat_ref, mid_ref, ud_ref, bu_ref, sr_ref, yd_ref):
    seq = u_ref.shape[0]
    t = S5_CHUNK
    nc = seq // t
    quarter = seq // 4
    n_lane_halves = u_ref.shape[1] // LANES
    n_state = bmat_ref.shape[2] // 2

    for h in range(n_lane_halves):
        lanes = slice(h * LANES, (h + 1) * LANES)
        nat_ref[h] = u_ref[:, lanes]
        for q in range(4):
            mid_ref[h, q * quarter:(q + 1) * quarter, :] = nat_ref[h, pl.ds(q, quarter, stride=4), :]
        for q in range(4):
            for r in range(4):
                pos = 4 * r + q
                ud_ref[pos * nc:(pos + 1) * nc, lanes] = (
                    mid_ref[h, pl.ds(q * quarter + r, nc, stride=4), :].astype(BF16))

    yd_ref[...] = _dot(ud_ref[...], fir_ref[0])
    for tau in range(1, t):
        yd_ref[tau * nc:, :] += _dot(ud_ref[0:(t - tau) * nc, :], fir_ref[tau])

    n_parts = bmat_ref.shape[0]
    for part in range(n_parts):
        bu_ref[part] = _dot(ud_ref[...], bmat_ref[part])
    row = lax.broadcasted_iota(jnp.int32, (nc, LANES), 0)
    cmul = lambda x, y: (x[0] * y[0] - x[1] * y[1], x[0] * y[1] + x[1] * y[0])
    for part in range(n_parts):
        for lc in range(n_state // LANES):
            re = slice(lc * LANES, (lc + 1) * LANES)
            im = slice(n_state + lc * LANES, n_state + (lc + 1) * LANES)
            pw = lambda k: (pw_ref[part, k:k + 1, re], pw_ref[part, k:k + 1, im])
            lam = pw(0)
            z = (bu_ref[part, 0:nc, re], bu_ref[part, 0:nc, im])
            for pos in range(1, t):
                rows = slice(pos * nc, (pos + 1) * nc)
                lz = cmul(lam, z)
                z = (lz[0] + bu_ref[part, rows, re], lz[1] + bu_ref[part, rows, im])
            shift, level = 1, 0
            while shift < nc:
                valid = row >= shift
                prev = (jnp.where(valid, pltpu.roll(z[0], shift, axis=0), 0.0),
                        jnp.where(valid, pltpu.roll(z[1], shift, axis=0), 0.0))
                step = cmul(pw(t + level), prev)
                z = (z[0] + step[0], z[1] + step[1])
                shift *= 2
                level += 1
            first = row >= 1
            s_in = (jnp.where(first, pltpu.roll(z[0], 1, axis=0), 0.0),
                    jnp.where(first, pltpu.roll(z[1], 1, axis=0), 0.0))
            for pos in range(t):
                rows = slice(pos * nc, (pos + 1) * nc)
                rot = cmul(pw(pos), s_in)
                sr_ref[part, rows, re] = rot[0].astype(BF16)
                sr_ref[part, rows, im] = rot[1].astype(BF16)
        yd_ref[...] += _dot(sr_ref[part], cmat_ref[part])

    for h in range(n_lane_halves):
        lanes = slice(h * LANES, (h + 1) * LANES)
        for q in range(4):
            for r in range(4):
                pos = 4 * r + q
                mid_ref[h, pl.ds(q * quarter + r, nc, stride=4), :] = yd_ref[pos * nc:(pos + 1) * nc, lanes]
        for q in range(4):
            nat_ref[h, pl.ds(q, quarter, stride=4), :] = mid_ref[h, q * quarter:(q + 1) * quarter, :]
        y_ref[:, lanes] = nat_ref[h]


def _s5_core(proj, fir, bmat, cmat, pw, *, batch, seq, width):
    n_slabs = width // S5_SLAB
    u_col0 = (proj.shape[1] - width) // S5_SLAB
    wblk = lambda arr: pl.BlockSpec((None,) + arr.shape[1:], lambda b, s: (s,) + (0,) * (arr.ndim - 1))
    n_parts, n_state2 = bmat.shape[1], bmat.shape[-1]
    return pl.pallas_call(
        _s5_kernel,
        grid=(batch, n_slabs),
        in_specs=[pl.BlockSpec((seq, S5_SLAB), lambda b, s: (b, u_col0 + s)),
                  wblk(fir), wblk(bmat), wblk(cmat), wblk(pw)],
        out_specs=pl.BlockSpec((seq, S5_SLAB), lambda b, s: (b, s)),
        out_shape=jax.ShapeDtypeStruct((batch * seq, width), F32),
        scratch_shapes=[pltpu.VMEM((S5_SLAB // LANES, seq, LANES), F32),
                        pltpu.VMEM((S5_SLAB // LANES, seq, LANES), F32),
                        pltpu.VMEM((seq, S5_SLAB), BF16),
                        pltpu.VMEM((n_parts, seq, n_state2), F32),
                        pltpu.VMEM((n_parts, seq, n_state2), BF16),
                        pltpu.VMEM((seq, S5_SLAB), F32)],
        compiler_params=_compiler_params(("parallel", "arbitrary"), 56),
        name="s5_core",
    )(proj, fir, bmat, cmat, pw)


def _s5_matrices(lam_re, lam_im, log_dt, b_re, b_im, c_re, c_im, *, seq):
    t = S5_CHUNK
    groups = lam_re.shape[0]
    gs = S5_SLAB // S5_GROUP
    gp = S5_PART_GROUPS
    n_parts = gs // gp
    n_slabs = groups // gs
    dt = jnp.exp(log_dt)[:, None]
    mag = jnp.exp(lam_re * dt)
    lb_re, lb_im = mag * jnp.cos(lam_im * dt), mag * jnp.sin(lam_im * dt)
    den = lam_re * lam_re + lam_im * lam_im
    z_re = ((lb_re - 1.0) * lam_re + lb_im * lam_im) / den
    z_im = (lb_im * lam_re - (lb_re - 1.0) * lam_im) / den
    bb_re = z_re[..., None] * b_re - z_im[..., None] * b_im
    bb_im = z_re[..., None] * b_im + z_im[..., None] * b_re

    def mul(carry, _):
        pr, pi = carry
        return (pr * lb_re - pi * lb_im, pr * lb_im + pi * lb_re), (pr, pi)

    _, (pw_re, pw_im) = lax.scan(mul, (jnp.ones_like(lb_re), jnp.zeros_like(lb_re)), None, length=t + 1)

    cl_re = pw_re[:t, :, None, :] * c_re[None] - pw_im[:t, :, None, :] * c_im[None]
    cl_im = pw_re[:t, :, None, :] * c_im[None] + pw_im[:t, :, None, :] * c_re[None]
    taps = jnp.einsum('tghp,gpi->tghi', cl_re, bb_re) - jnp.einsum('tghp,gpi->tghi', cl_im, bb_im)
    taps = jnp.transpose(taps.reshape(t, n_slabs, gs, S5_GROUP, S5_GROUP), (1, 0, 2, 4, 3))
    same_group = jnp.eye(gs, dtype=BF16)[None, None, :, None, :, None]
    fir = (taps.astype(BF16)[:, :, :, :, None, :] * same_group).reshape(n_slabs, t, S5_SLAB, S5_SLAB)

    eye_g = jnp.eye(gp, dtype=F32)
    eye_p = jnp.eye(n_parts, dtype=F32)

    def b_slab(bb):
        blk = jnp.einsum('abgph,gk->abghkp', bb.reshape(n_slabs, n_parts, gp, S5_STATE, S5_GROUP), eye_g)
        blk = blk.reshape(n_slabs, n_parts, gp * S5_GROUP, gp * S5_STATE)
        return (eye_p[None, :, :, None, None] * blk[:, :, None]).reshape(n_slabs, n_parts, S5_SLAB, gp * S5_STATE)

    def c_slab(cc):
        blk = jnp.einsum('abghp,gk->abgpkh', cc.reshape(n_slabs, n_parts, gp, S5_GROUP, S5_STATE), eye_g)
        blk = blk.reshape(n_slabs, n_parts, gp * S5_STATE, gp * S5_GROUP)
        return (eye_p[None, :, None, :, None] * blk[:, :, :, None]).reshape(n_slabs, n_parts, gp * S5_STATE, S5_SLAB)

    bmat = jnp.concatenate([b_slab(bb_re), b_slab(bb_im)], axis=-1)
    cmat = jnp.concatenate([c_slab(c_re), -c_slab(c_im)], axis=2)

    squares = [(pw_re[t], pw_im[t])]
    for _ in range((seq // t - 1).bit_length() - 1):
        sr, si = squares[-1]
        squares.append((sr * sr - si * si, 2.0 * sr * si))
    tab_re = jnp.concatenate([pw_re[1:], jnp.stack([s[0] for s in squares])], axis=0)
    tab_im = jnp.concatenate([pw_im[1:], jnp.stack([s[1] for s in squares])], axis=0)

    def state_lanes(x):
        return jnp.transpose(x.reshape(x.shape[0], n_slabs, n_parts, gp * S5_STATE), (1, 2, 0, 3))

    pw = jnp.concatenate([state_lanes(tab_re), state_lanes(tab_im)], axis=-1)
    return fir.astype(BF16), bmat.astype(BF16), cmat.astype(BF16), pw


def _mix_out_kernel(x_ref, ya_ref, ys_ref, u_ref, d_ref, gw_ref, gb_ref, wa_ref, wb_ref, o_ref):
    yb = _gelu_tanh(ys_ref[...] + d_ref[...] * u_ref[...])
    yb = yb * _sigmoid(_dot(yb.astype(BF16), gw_ref[...]) + gb_ref[...])
    o_ref[...] = (x_ref[...] + _dot(ya_ref[...].astype(BF16), wa_ref[...])
                  + _dot(yb.astype(BF16), wb_ref[...]))


def _mix_out(x, ya, ys, proj, d_skip, glu_w, glu_b, w_out_a, w_out_b, *, tm):
    m, d = x.shape
    width = ya.shape[1]
    u_col = proj.shape[1] // width - 1
    full = lambda arr: pl.BlockSpec(arr.shape, lambda i: (0, 0), pipeline_mode=pl.Buffered(1))
    row = lambda w, col=0: pl.BlockSpec((tm, w), lambda i: (i, col))
    return pl.pallas_call(
        _mix_out_kernel,
        grid=(m // tm,),
        in_specs=[row(d), row(width), row(width), row(width, u_col), full(d_skip), full(glu_w),
                  full(glu_b), full(w_out_a), full(w_out_b)],
        out_specs=row(d),
        out_shape=jax.ShapeDtypeStruct((m, d), F32),
        compiler_params=_compiler_params(("parallel",), 56),
        name="mix_out",
    )(x, ya, ys, proj, d_skip, glu_w, glu_b, w_out_a, w_out_b)


def _mlp_kernel(x_ref, g_ref, w1_ref, w2_ref, gn_ref, o_ref, *rest, norm_mode):
    if norm_mode == "next":
        hn_ref, h_ref = rest
    else:
        (h_ref,) = rest
    f = pl.program_id(1)

    @pl.when(f == 0)
    def _():
        x = x_ref[...]
        h_ref[...] = _rms(x, g_ref[...]).astype(BF16)
        o_ref[...] = x

    hid = jnp.maximum(_dot(h_ref[...], w1_ref[...]), 0.0)
    o_ref[...] += _dot((hid * hid).astype(BF16), w2_ref[...])

    @pl.when(f == pl.num_programs(1) - 1)
    def _():
        normed = _rms(o_ref[...], gn_ref[...])
        if norm_mode == "next":
            hn_ref[...] = normed.astype(BF16)
        else:
            o_ref[...] = normed


def _mlp(x, g, w1, w2, gn, *, norm_mode, tm, tf):
    assert norm_mode in ("final", "next")
    m, d = x.shape
    dff = w1.shape[1]
    row = pl.BlockSpec((tm, d), lambda i, f: (i, 0))
    out_specs, out_shape = [row], [jax.ShapeDtypeStruct((m, d), F32)]
    if norm_mode == "next":
        out_specs.append(row)
        out_shape.append(jax.ShapeDtypeStruct((m, d), BF16))
    return pl.pallas_call(
        functools.partial(_mlp_kernel, norm_mode=norm_mode),
        grid=(m // tm, dff // tf),
        in_specs=[row,
                  pl.BlockSpec((1, d), lambda i, f: (0, 0)),
                  pl.BlockSpec((d, tf), lambda i, f: (0, f)),
                  pl.BlockSpec((tf, d), lambda i, f: (f, 0)),
                  pl.BlockSpec((1, d), lambda i, f: (0, 0))],
        out_specs=out_specs,
        out_shape=out_shape,
        scratch_shapes=[pltpu.VMEM((tm, d), BF16)],
        compiler_params=_compiler_params(("parallel", "arbitrary"), 48),
        name="mlp",
    )(x, g, w1, w2, gn)


def _rglru_kernel(gate_ref, xr_ref, x_ref, cw_ref, cb_ref, wa_ref, wx_ref, ba_ref, bx_ref, lam_ref,
                  wout_ref, o_ref, tail_ref, h_ref, y_ref):
    c = pl.program_id(1)
    n_blocks = pl.num_programs(1) - 1
    n_heads, blk = wa_ref.shape[0], wa_ref.shape[1]
    old, new = (c + 1) % 2, c % 2

    @pl.when(c == 0)
    def _():
        tail_ref[...] = jnp.zeros_like(tail_ref)
        h_ref[...] = jnp.zeros_like(h_ref)
        y_ref[...] = jnp.zeros_like(y_ref)

    def project(h):
        cols = slice(h * blk, (h + 1) * blk)
        o_ref[:, cols] = x_ref[:, cols] + _dot(y_ref[old], wout_ref[:, cols])

    @pl.when(c < n_blocks)
    def _():
        for h in range(n_heads):
            project(h)
            cols = slice(h * blk, (h + 1) * blk)
            y_ref[new, :, cols] = _rglru_head(
                gate_ref, xr_ref, cw_ref, cb_ref, wa_ref, wx_ref, ba_ref, bx_ref, lam_ref,
                tail_ref, h_ref, h, c == 0).astype(BF16)

    @pl.when(c == n_blocks)
    def _():
        for h in range(n_heads):
            project(h)


def _rglru_head(gate_ref, xr_ref, cw_ref, cb_ref, wa_ref, wx_ref, ba_ref, bx_ref, lam_ref,
                tail_ref, h_ref, head, is_first):
    tt = xr_ref.shape[0]
    blk = wa_ref.shape[1]
    cols = slice(head * blk, (head + 1) * blk)

    xr = xr_ref[:, cols]
    tail = tail_ref[:, cols]
    row8 = lax.broadcasted_iota(jnp.int32, (SUBLANES, blk), 0)
    xc = xr * cw_ref[CONV_WIDTH - 1:CONV_WIDTH, cols] + cb_ref[:, cols]
    for s in range(1, CONV_WIDTH):
        xs = pltpu.roll(xr, s, axis=0)
        top = jnp.where(row8 < s, pltpu.roll(tail, s, axis=0), xs[:SUBLANES])
        xs = jnp.concatenate([top, xs[SUBLANES:]], axis=0)
        xc = xc + xs * cw_ref[CONV_WIDTH - 1 - s:CONV_WIDTH - s, cols]
    tail_ref[:, cols] = xr[tt - SUBLANES:tt, :]

    xcb = xc.astype(BF16)
    gate_r = _sigmoid(_dot(xcb, wa_ref[head]) + ba_ref[:, cols])
    gate_i = _sigmoid(_dot(xcb, wx_ref[head]) + bx_ref[:, cols])
    log_a = LRU_C * gate_r * (-_softplus(-lam_ref[:, cols]))
    a = jnp.exp(log_a)
    rest = 1.0 - a * a
    mult = jnp.where(rest > 0.0, rest * lax.rsqrt(rest), 0.0)
    top = jnp.where((row8 == 0) & is_first, 1.0, mult[:SUBLANES])
    mult = jnp.concatenate([top, mult[SUBLANES:]], axis=0)
    b = xc * gate_i * mult

    n_groups = tt // SUBLANES
    a3 = a.reshape(n_groups, SUBLANES, blk)
    b3 = b.reshape(n_groups, SUBLANES, blk)
    sub = lax.broadcasted_iota(jnp.int32, a3.shape, 1)
    shift = 1
    while shift < SUBLANES:
        valid = sub >= shift
        b3 = b3 + a3 * jnp.where(valid, pltpu.roll(b3, shift, axis=1), 0.0)
        a3 = a3 * jnp.where(valid, pltpu.roll(a3, shift, axis=1), 1.0)
        shift *= 2
    carry = h_ref[0:1, cols]
    groups = []
    for i in range(n_groups):
        h_i = b3[i] + a3[i] * carry
        groups.append(h_i)
        carry = h_i[SUBLANES - 1:SUBLANES, :]
    h_ref[:, cols] = jnp.broadcast_to(carry, (SUBLANES, blk))
    return gate_ref[:, cols] * jnp.concatenate(groups, axis=0)


def _rglru(proj, x, conv_w, conv_b, w_a, w_x, b_a, b_x, lam, w_out, *, batch, seq, tt):
    width = lam.shape[1]
    d = x.shape[1]
    nc = seq // tt
    cur = lambda col: pl.BlockSpec((tt, width), lambda b, c: (b * nc + jnp.minimum(c, nc - 1), col))
    prev = pl.BlockSpec((tt, d), lambda b, c: (b * nc + jnp.maximum(c - 1, 0), 0))
    par = lambda arr: pl.BlockSpec(arr.shape, lambda b, c: (0,) * arr.ndim, pipeline_mode=pl.Buffered(1))
    return pl.pallas_call(
        _rglru_kernel,
        grid=(batch, nc + 1),
        in_specs=[cur(0), cur(1), prev, par(conv_w), par(conv_b), par(w_a), par(w_x),
                  par(b_a), par(b_x), par(lam), par(w_out)],
        out_specs=prev,
        out_shape=jax.ShapeDtypeStruct((batch * seq, d), F32),
        scratch_shapes=[pltpu.VMEM((SUBLANES, width), F32), pltpu.VMEM((SUBLANES, width), F32),
                        pltpu.VMEM((2, tt, width), BF16)],
        compiler_params=_compiler_params(("parallel", "arbitrary"), 56),
        name="rglru",
    )(proj, proj, x, conv_w, conv_b, w_a, w_x, b_a, b_x, lam, w_out)


def _pad_cols(w, n):
    return jnp.pad(w, ((0, 0), (0, n - w.shape[1])))


def _pad_rows(w, n):
    return jnp.pad(w, ((0, n - w.shape[0]), (0, 0)))


def kernel(x, norm_mix_g, norm_mlp_g, norm_f_g, mlp_w1, mlp_w2, hy_w_in, hy_w_out, rw_mu_rkv, rw_mu_wag, rw_w0, rw_w1, rw_w2, rw_a0, rw_a1, rw_a2, rw_g1, rw_g2, rw_k_k, rw_k_a, rw_r_k, rw_ln_w, rw_ln_b, s5_lam_re, s5_lam_im, s5_log_dt, s5_b_re, s5_b_im, s5_c_re, s5_c_im, s5_d, s5_glu_w, s5_glu_b, rg_w_in, rg_w_out, rg_conv_w, rg_conv_b, rg_w_a, rg_b_a, rg_w_x, rg_b_x, rg_lam):
    batch, seq, d = x.shape
    m = batch * seq
    rw = rw_w2.shape[-1]
    row = lambda v: v.reshape(1, -1)
    bf = lambda w: w.astype(BF16)
    lane = 128
    xf = x.reshape(m, d)

    lw, a, gate, proj = _rwkv_lora(
        xf, row(norm_mix_g[0]), rw_mu_wag[0],
        bf(_pad_cols(rw_w1[0], lane)), bf(_pad_cols(rw_a1[0], lane)), bf(rw_g1[0]),
        bf(_pad_rows(rw_w2[0], lane)), bf(_pad_rows(rw_a2[0], lane)), bf(rw_g2[0]),
        row(rw_w0[0]), row(rw_a0[0]), bf(hy_w_in[0]), seq=seq, tm=256)
    ya = _rwkv_core(proj, lw, a, gate, rw_mu_rkv[0], row(rw_k_k[0]), row(rw_k_a[0]),
                    row(rw_r_k[0]), row(rw_ln_w[0]), row(rw_ln_b[0]), batch=batch, seq=seq)

    mats = _s5_matrices(s5_lam_re[0], s5_lam_im[0], s5_log_dt[0], s5_b_re[0], s5_b_im[0],
                        s5_c_re[0], s5_c_im[0], seq=seq)
    ys = _s5_core(proj, *mats, batch=batch, seq=seq, width=s5_d.shape[-1])

    x1 = _mix_out(xf, ya, ys, proj, row(s5_d[0]), bf(s5_glu_w[0]), row(s5_glu_b[0]),
                  bf(hy_w_out[0][:rw]), bf(hy_w_out[0][rw:]), tm=512)
    x2, hn1 = _mlp(x1, row(norm_mlp_g[0]), bf(mlp_w1[0]), bf(mlp_w2[0]), row(norm_mix_g[1]),
                   norm_mode="next", tm=512, tf=1024)

    proj1 = _rg_in_proj(hn1, bf(rg_w_in[0]), tm=1024, tn=1024)
    x3 = _rglru(proj1, x2, rg_conv_w[0], row(rg_conv_b[0]), bf(rg_w_a[0]), bf(rg_w_x[0]),
                row(rg_b_a[0]), row(rg_b_x[0]), row(rg_lam[0]), bf(rg_w_out[0]),
                batch=batch, seq=seq, tt=256)
    (out,) = _mlp(x3, row(norm_mlp_g[1]), bf(mlp_w1[1]), bf(mlp_w2[1]), row(norm_f_g),
                  norm_mode="final", tm=512, tf=1024)
    return out.reshape(batch, seq, d)
```

```python
import functools
import math

import jax
import jax.numpy as jnp
from jax import lax
from jax.experimental import pallas as pl
from jax.experimental.pallas import tpu as pltpu

F32 = jnp.float32
BF16 = jnp.bfloat16

NORM_EPS = 1e-6
RWKV_HEAD = 64
RWKV_GN_EPS = 64e-5
RWKV_CHUNK = 64
RWKV_SEQS = 2
HEAD_PAIR = 2 * RWKV_HEAD
S5_GROUP = 16
S5_STATE = 64
S5_CHUNK = 16
S5_SLAB = 256
S5_PART_GROUPS = 2
LRU_C = 8.0
CONV_WIDTH = 4
SUBLANES = 8
LANES = 128
V7X_VMEM_BYTES = 64 * 1024 * 1024


def _compiler_params(semantics, vmem_mib):
    assert vmem_mib * 1024 * 1024 < V7X_VMEM_BYTES
    return pltpu.CompilerParams(dimension_semantics=semantics,
                                vmem_limit_bytes=vmem_mib * 1024 * 1024)


def _rms(x, g):
    return x * lax.rsqrt(jnp.mean(x * x, axis=-1, keepdims=True) + NORM_EPS) * g


def _gelu_tanh(x):
    c = math.sqrt(2.0 / math.pi)
    return 0.5 * x * (1.0 + jnp.tanh(c * (x + 0.044715 * (x * x * x))))


def _sigmoid(x):
    return 1.0 / (1.0 + jnp.exp(-x))


def _softplus(x):
    return jnp.maximum(x, 0.0) + jnp.log1p(jnp.exp(-jnp.abs(x)))


def _dot(a, b):
    return jnp.dot(a, b, preferred_element_type=F32)


def _dot_nt(a, b):
    return lax.dot_general(a, b, (((1,), (1,)), ((), ())), preferred_element_type=F32)


def _dot_tn(a, b):
    return lax.dot_general(a, b, (((0,), (0,)), ((), ())), preferred_element_type=F32)


def _shift_rows(x, prev_row):
    row = lax.broadcasted_iota(jnp.int32, x.shape, 0)
    return jnp.where(row == 0, prev_row, pltpu.roll(x, 1, axis=0))


def _rg_in_proj_kernel(h_ref, w_ref, o_ref, *, n_gate):
    j = pl.program_id(1)
    sub = 2 * LANES

    @pl.when(j < n_gate)
    def _():
        h = h_ref[...]
        for n in range(o_ref.shape[1] // sub):
            cols = slice(n * sub, (n + 1) * sub)
            o_ref[:, cols] = _gelu_tanh(_dot(h, w_ref[:, cols]))

    @pl.when(j >= n_gate)
    def _():
        o_ref[...] = _dot(h_ref[...], w_ref[...])


def _rg_in_proj(h, w, *, tm, tn):
    m, d = h.shape
    n = w.shape[1]
    return pl.pallas_call(
        functools.partial(_rg_in_proj_kernel, n_gate=(n // 2) // tn),
        grid=(m // tm, n // tn),
        in_specs=[pl.BlockSpec((tm, d), lambda i, j: (i, 0)),
                  pl.BlockSpec((d, tn), lambda i, j: (0, j))],
        out_specs=pl.BlockSpec((tm, tn), lambda i, j: (i, j)),
        out_shape=jax.ShapeDtypeStruct((m, n), F32),
        compiler_params=_compiler_params(("parallel", "arbitrary"), 40),
        name="in_proj",
    )(h, w)


def _lora_kernel(x_ref, xp_ref, g_ref, mu_ref, w1_ref, a1_ref, g1_ref, w2_ref, a2_ref, g2_ref,
                 w0_ref, a0_ref, win_ref, lw_ref, a_ref, gate_ref, proj_ref, *, blocks_per_seq):
    g = g_ref[...]
    h = _rms(x_ref[...], g)
    proj_ref[...] = _dot(h.astype(BF16), win_ref[...])
    hp = _rms(xp_ref[...], g)
    first = (pl.program_id(0) % blocks_per_seq) == 0
    prev_row = jnp.where(first, 0.0, hp[SUBLANES - 1:SUBLANES, :])
    dh = _shift_rows(h, prev_row) - h
    xw = (h + dh * mu_ref[0:1, :]).astype(BF16)
    xa = (h + dh * mu_ref[1:2, :]).astype(BF16)
    xg = (h + dh * mu_ref[2:3, :]).astype(BF16)
    tw = jnp.tanh(_dot(xw, w1_ref[...])).astype(BF16)
    w_pre = w0_ref[...] + _dot(tw, w2_ref[...])
    w_log = -_softplus(-w_pre) - 0.5
    lw_ref[...] = -jnp.exp(w_log)
    ta = _dot(xa, a1_ref[...]).astype(BF16)
    a_ref[...] = _sigmoid(a0_ref[...] + _dot(ta, a2_ref[...]))
    tg = _sigmoid(_dot(xg, g1_ref[...])).astype(BF16)
    gate_ref[...] = _dot(tg, g2_ref[...])


def _rwkv_lora(x, g, mu_wag, w1, a1, g1, w2, a2, g2, w0, a0, w_in, *, seq, tm):
    m, d = x.shape
    width = w2.shape[1]
    n_proj = w_in.shape[1]
    blocks_per_seq = seq // tm
    full = lambda arr: pl.BlockSpec(arr.shape, lambda i: (0, 0), pipeline_mode=pl.Buffered(1))
    out = jax.ShapeDtypeStruct((m, width), F32)
    return pl.pallas_call(
        functools.partial(_lora_kernel, blocks_per_seq=blocks_per_seq),
        grid=(m // tm,),
        in_specs=[pl.BlockSpec((tm, d), lambda i: (i, 0)),
                  pl.BlockSpec((SUBLANES, d), lambda i: (jnp.maximum(i * (tm // SUBLANES) - 1, 0), 0)),
                  full(g), full(mu_wag), full(w1), full(a1), full(g1), full(w2), full(a2), full(g2),
                  full(w0), full(a0), full(w_in)],
        out_specs=[pl.BlockSpec((tm, width), lambda i: (i, 0))] * 3 + [pl.BlockSpec((tm, n_proj), lambda i: (i, 0))],
        out_shape=[out, out, out, jax.ShapeDtypeStruct((m, n_proj), F32)],
        compiler_params=_compiler_params(("parallel",), 56),
        name="rwkv_lora",
    )(x, x, g, mu_wag, w1, a1, g1, w2, a2, g2, w0, a0, w_in)


def _split3(x):
    hi = x.astype(BF16)
    r1 = x - hi.astype(F32)
    mid = r1.astype(BF16)
    lo = (r1 - mid.astype(F32)).astype(BF16)
    return hi, mid, lo


def _split2(x):
    hi = x.astype(BF16)
    lo = (x - hi.astype(F32)).astype(BF16)
    return hi, lo


def _rwkv_kernel(pr_ref, pk_ref, pv_ref, lw_ref, a_ref, gate_ref, mu_ref, kk_ref, ka_ref, rk_ref,
                 lnw_ref, lnb_ref, o_ref, prev_ref, state_ref):
    n_seq, t, width = o_ref.shape
    rows = n_seq * t
    n_pairs = width // HEAD_PAIR
    seqs = range(n_seq)
    pairs = range(n_pairs)
    chains = [(s, p) for s in seqs for p in pairs]
    slab = lambda x, p: x[:, p * HEAD_PAIR:(p + 1) * HEAD_PAIR]
    seq_rows = lambda x, s: x[s * t:(s + 1) * t]
    lanes = lambda xs: jnp.concatenate(xs, axis=1)
    flat = lambda ref: ref[...].reshape(rows, width)

    @pl.when(pl.program_id(1) == 0)
    def _():
        prev_ref[...] = jnp.zeros_like(prev_ref)
        state_ref[...] = jnp.zeros_like(state_ref)

    row = lax.broadcasted_iota(jnp.int32, (rows, width), 0)

    def per_seq(vals):
        out = vals[-1]
        for s in reversed(range(n_seq - 1)):
            out = jnp.where(row < (s + 1) * t, vals[s], out)
        return jnp.broadcast_to(out, (rows, width))

    def shift_mix(p_ref, idx):
        p = flat(p_ref)
        ps = pltpu.roll(p, 1, axis=0)
        for s in seqs:
            ps = jnp.where(row == s * t, prev_ref[idx, s, SUBLANES - 1:SUBLANES, :], ps)
            prev_ref[idx, s] = p[(s + 1) * t - SUBLANES:(s + 1) * t, :]
        return p + (ps - p) * mu_ref[idx:idx + 1, :]

    r_all = shift_mix(pr_ref, 0)
    k_all = shift_mix(pk_ref, 1)
    v_all = shift_mix(pv_ref, 2)
    a_all = flat(a_ref)
    lw_all = flat(lw_ref)

    ri = lax.broadcasted_iota(jnp.int32, (rows, rows), 0)
    ci = lax.broadcasted_iota(jnp.int32, (rows, rows), 1)
    tril = jnp.where((ci <= ri) & (ci // t == ri // t), 1.0, 0.0).astype(BF16)
    cum_all = _dot(jnp.concatenate([tril, tril, tril], axis=1),
                   jnp.concatenate(_split3(lw_all), axis=0))

    si = lax.broadcasted_iota(jnp.int32, (2 * HEAD_PAIR, HEAD_PAIR), 0) % HEAD_PAIR // RWKV_HEAD
    sj = lax.broadcasted_iota(jnp.int32, (2 * HEAD_PAIR, HEAD_PAIR), 1) // RWKV_HEAD
    seg2 = jnp.where(si == sj, 1.0, 0.0).astype(BF16)

    def head_sum_all(x):
        xh, xl = _split2(x)
        stacked = jnp.concatenate([lanes([slab(xh, p), slab(xl, p)]) for p in pairs], axis=0)
        sums = _dot(stacked, seg2)
        return lanes([sums[p * rows:(p + 1) * rows] for p in pairs])

    head0 = (lax.broadcasted_iota(jnp.int32, (t, width), 1) % HEAD_PAIR) < RWKV_HEAD

    def expand(x, s):
        xb = seq_rows(x, s).astype(BF16)
        zero = jnp.zeros_like(xb)
        return jnp.concatenate([jnp.where(head0, xb, zero), jnp.where(head0, zero, xb)], axis=0)

    bi = lax.broadcasted_iota(jnp.int32, (2 * t, 2 * t), 0) % t
    bj = lax.broadcasted_iota(jnp.int32, (2 * t, 2 * t), 1) % t
    strict = bj < bi
    incl = bj <= bi
    eye = jnp.where(lax.broadcasted_iota(jnp.int32, (2 * t, 2 * t), 0)
                    == lax.broadcasted_iota(jnp.int32, (2 * t, 2 * t), 1), 1.0, 0.0)

    kk = k_all * kk_ref[...]
    kk = kk / jnp.maximum(jnp.sqrt(head_sum_all(kk * kk)), 1e-12)
    k_all = k_all * (1.0 + (a_all - 1.0) * ka_ref[...])
    b_all = kk * a_all
    cum_last = [cum_all[(s + 1) * t - 1:(s + 1) * t, :] for s in seqs]
    cum_t = per_seq(cum_last)
    p_inv = jnp.exp(-cum_all)
    d_rest = jnp.exp(cum_t - cum_all)
    a_t = kk * jnp.exp(cum_all - lw_all)
    r_t = r_all * jnp.exp(cum_all)
    k_t, b_t = k_all * p_inv, b_all * p_inv
    k_end, b_end = k_all * d_rest, -(b_all * d_rest)
    ar = [jnp.concatenate([expand(a_t, s), expand(r_t, s)], axis=0) for s in seqs]
    kb = [jnp.concatenate([expand(k_t, s), expand(b_t, s)], axis=0) for s in seqs]
    kbt = [jnp.concatenate([expand(k_end, s), expand(b_end, s)], axis=0) for s in seqs]
    v_e = [expand(v_all, s) for s in seqs]
    decay_t = [jnp.exp(c) for c in cum_last]

    states = [state_ref[s, p] for s, p in chains]
    scores = [_dot_nt(slab(ar[s], p), slab(kb[s], p)) for s, p in chains]
    ar_h = [_dot_nt(slab(ar[s], p), st.astype(BF16)) for (s, p), st in zip(chains, states)]
    n_ab = [jnp.where(strict, sc[:2 * t, 2 * t:], 0.0) for sc in scores]
    a_ak = [jnp.where(strict, sc[:2 * t, :2 * t], 0.0).astype(BF16) for sc in scores]
    a_r = [jnp.concatenate([jnp.where(incl, sc[2 * t:, :2 * t], 0.0),
                            jnp.where(incl, -sc[2 * t:, 2 * t:], 0.0)], axis=1).astype(BF16)
           for sc in scores]
    rhs = [(ar_h[i][:2 * t] + _dot(a_ak[i], slab(v_e[s], p))).astype(BF16)
           for i, (s, p) in enumerate(chains)]

    inv = [eye - n for n in n_ab]
    n_pow = [n.astype(BF16) for n in n_ab]
    n_pow = [_dot(n, n).astype(BF16) for n in n_pow]
    n_levels = int(math.log2(t)) - 1
    for level in range(n_levels):
        nxt = [_dot(n, n).astype(BF16) for n in n_pow] if level + 1 < n_levels else None
        inv = [x + _dot(x.astype(BF16), n) for x, n in zip(inv, n_pow)]
        n_pow = nxt

    u_e = [_dot(x.astype(BF16), rh).astype(BF16) for x, rh in zip(inv, rhs)]
    vu = [jnp.concatenate([slab(v_e[s], p), u_e[i]], axis=0) for i, (s, p) in enumerate(chains)]
    o_e = [ar_h[i][2 * t:] + _dot(a_r[i], vu[i]) for i in range(len(chains))]
    for i, (s, p) in enumerate(chains):
        state_ref[s, p] = states[i] * slab(decay_t[s], p) + _dot_tn(vu[i], slab(kbt[s], p))
    o = jnp.concatenate(
        [lanes([o_e[s * n_pairs + p][:t] + o_e[s * n_pairs + p][t:] for p in pairs]) for s in seqs],
        axis=0)

    inv_n = 1.0 / RWKV_HEAD
    mean = head_sum_all(o) * inv_n
    d = o - mean
    var = head_sum_all(d * d) * inv_n
    y = d * lax.rsqrt(var + RWKV_GN_EPS) * lnw_ref[...] + lnb_ref[...]
    y = y + head_sum_all(r_all * k_all * rk_ref[...]) * v_all
    o_ref[...] = (y * flat(gate_ref)).reshape(n_seq, t, width)


def _rwkv_core(proj, lw, a, gate, mu_rkv, k_k, k_a, r_k, ln_w, ln_b, *, batch, seq):
    width = lw.shape[1]
    t, n_seq = RWKV_CHUNK, RWKV_SEQS
    assert batch % n_seq == 0 and seq % t == 0
    nc = seq // t
    as_seqs = lambda arr: arr.reshape(batch, seq, arr.shape[1])
    blk = lambda col: pl.BlockSpec((n_seq, t, width), lambda b, c: (b, c, col))
    par = lambda arr: pl.BlockSpec(arr.shape, lambda b, c: (0, 0))
    proj3 = as_seqs(proj)
    out = pl.pallas_call(
        _rwkv_kernel,
        grid=(batch // n_seq, nc),
        in_specs=[blk(0), blk(1), blk(2), blk(0), blk(0), blk(0),
                  par(mu_rkv), par(k_k), par(k_a), par(r_k), par(ln_w), par(ln_b)],
        out_specs=blk(0),
        out_shape=jax.ShapeDtypeStruct((batch, seq, width), F32),
        scratch_shapes=[pltpu.VMEM((3, n_seq, SUBLANES, width), F32),
                        pltpu.VMEM((n_seq, width // HEAD_PAIR, HEAD_PAIR, HEAD_PAIR), F32)],
        compiler_params=_compiler_params(("parallel", "arbitrary"), 32),
        name="rwkv_core",
    )(proj3, proj3, proj3, as_seqs(lw), as_seqs(a), as_seqs(gate), mu_rkv, k_k, k_a, r_k, ln_w, ln_b)
    return out.reshape(batch * seq, width)


def _s5_kernel(u_ref, fir_ref, bmat_ref, cmat_ref, pw_ref, y_ref,
               nat_ref, mid_ref, ud_ref, bu_ref, sr_ref, yd_ref):
    seq = u_ref.shape[0]
    t = S5_CHUNK
    nc = seq // t
    quarter = seq // 4
    n_lane_halves = u_ref.shape[1] // LANES
    n_state = bmat_ref.shape[2] // 2

    for h in range(n_lane_halves):
        lanes = slice(h * LANES, (h + 1) * LANES)
        nat_ref[h] = u_ref[:, lanes]
        for q in range(4):
            mid_ref[h, q * quarter:(q + 1) * quarter, :] = nat_ref[h, pl.ds(q, quarter, stride=4), :]
        for q in range(4):
            for r in range(4):
                pos = 4 * r + q
                ud_ref[pos * nc:(pos + 1) * nc, lanes] = (
                    mid_ref[h, pl.ds(q * quarter + r, nc, stride=4), :].astype(BF16))

    yd_ref[...] = _dot(ud_ref[...], fir_ref[0])
    for tau in range(1, t):
        yd_ref[tau * nc:, :] += _dot(ud_ref[0:(t - tau) * nc, :], fir_ref[tau])

    n_parts = bmat_ref.shape[0]
    for part in range(n_parts):
        bu_ref[part] = _dot(ud_ref[...], bmat_ref[part])
    row = lax.broadcasted_iota(jnp.int32, (nc, LANES), 0)
    cmul = lambda x, y: (x[0] * y[0] - x[1] * y[1], x[0] * y[1] + x[1] * y[0])
    for part in range(n_parts):
        for lc in range(n_state // LANES):
            re = slice(lc * LANES, (lc + 1) * LANES)
            im = slice(n_state + lc * LANES, n_state + (lc + 1) * LANES)
            pw = lambda k: (pw_ref[part, k:k + 1, re], pw_ref[part, k:k + 1, im])
            lam = pw(0)
            z = (bu_ref[part, 0:nc, re], bu_ref[part, 0:nc, im])
            for pos in range(1, t):
                rows = slice(pos * nc, (pos + 1) * nc)
                lz = cmul(lam, z)
                z = (lz[0] + bu_ref[part, rows, re], lz[1] + bu_ref[part, rows, im])
            shift, level = 1, 0
            while shift < nc:
                valid = row >= shift
                prev = (jnp.where(valid, pltpu.roll(z[0], shift, axis=0), 0.0),
                        jnp.where(valid, pltpu.roll(z[1], shift, axis=0), 0.0))
                step = cmul(pw(t + level), prev)
                z = (z[0] + step[0], z[1] + step[1])
                shift *= 2
                level += 1
            first = row >= 1
            s_in = (jnp.where(first, pltpu.roll(z[0], 1, axis=0), 0.0),
                    jnp.where(first, pltpu.roll(z[1], 1, axis=0), 0.0))
            for pos in range(t):
                rows = slice(pos * nc, (pos + 1) * nc)
                rot = cmul(pw(pos), s_in)
                sr_ref[part, rows, re] = rot[0].astype(BF16)
                sr_ref[part, rows, im] = rot[1].astype(BF16)
        yd_ref[...] += _dot(sr_ref[part], cmat_ref[part])

    for h in range(n_lane_halves):
        lanes = slice(h * LANES, (h + 1) * LANES)
        for q in range(4):
            for r in range(4):
                pos = 4 * r + q
                mid_ref[h, pl.ds(q * quarter + r, nc, stride=4), :] = yd_ref[pos * nc:(pos + 1) * nc, lanes]
        for q in range(4):
            nat_ref[h, pl.ds(q, quarter, stride=4), :] = mid_ref[h, q * quarter:(q + 1) * quarter, :]
        y_ref[:, lanes] = nat_ref[h]


def _s5_core(proj, fir, bmat, cmat, pw, *, batch, seq, width):
    n_slabs = width // S5_SLAB
    u_col0 = (proj.shape[1] - width) // S5_SLAB
    wblk = lambda arr: pl.BlockSpec((None,) + arr.shape[1:], lambda b, s: (s,) + (0,) * (arr.ndim - 1))
    n_parts, n_state2 = bmat.shape[1], bmat.shape[-1]
    return pl.pallas_call(
        _s5_kernel,
        grid=(batch, n_slabs),
        in_specs=[pl.BlockSpec((seq, S5_SLAB), lambda b, s: (b, u_col0 + s)),
                  wblk(fir), wblk(bmat), wblk(cmat), wblk(pw)],
        out_specs=pl.BlockSpec((seq, S5_SLAB), lambda b, s: (b, s)),
        out_shape=jax.ShapeDtypeStruct((batch * seq, width), F32),
        scratch_shapes=[pltpu.VMEM((S5_SLAB // LANES, seq, LANES), F32),
                        pltpu.VMEM((S5_SLAB // LANES, seq, LANES), F32),
                        pltpu.VMEM((seq, S5_SLAB), BF16),
                        pltpu.VMEM((n_parts, seq, n_state2), F32),
                        pltpu.VMEM((n_parts, seq, n_state2), BF16),
                        pltpu.VMEM((seq, S5_SLAB), F32)],
        compiler_params=_compiler_params(("parallel", "arbitrary"), 56),
        name="s5_core",
    )(proj, fir, bmat, cmat, pw)


def _s5_matrices(lam_re, lam_im, log_dt, b_re, b_im, c_re, c_im, *, seq):
    t = S5_CHUNK
    groups = lam_re.shape[0]
    gs = S5_SLAB // S5_GROUP
    gp = S5_PART_GROUPS
    n_parts = gs // gp
    n_slabs = groups // gs
    dt = jnp.exp(log_dt)[:, None]
    mag = jnp.exp(lam_re * dt)
    lb_re, lb_im = mag * jnp.cos(lam_im * dt), mag * jnp.sin(lam_im * dt)
    den = lam_re * lam_re + lam_im * lam_im
    z_re = ((lb_re - 1.0) * lam_re + lb_im * lam_im) / den
    z_im = (lb_im * lam_re - (lb_re - 1.0) * lam_im) / den
    bb_re = z_re[..., None] * b_re - z_im[..., None] * b_im
    bb_im = z_re[..., None] * b_im + z_im[..., None] * b_re

    def mul(carry, _):
        pr, pi = carry
        return (pr * lb_re - pi * lb_im, pr * lb_im + pi * lb_re), (pr, pi)

    _, (pw_re, pw_im) = lax.scan(mul, (jnp.ones_like(lb_re), jnp.zeros_like(lb_re)), None, length=t + 1)

    cl_re = pw_re[:t, :, None, :] * c_re[None] - pw_im[:t, :, None, :] * c_im[None]
    cl_im = pw_re[:t, :, None, :] * c_im[None] + pw_im[:t, :, None, :] * c_re[None]
    taps = jnp.einsum('tghp,gpi->tghi', cl_re, bb_re) - jnp.einsum('tghp,gpi->tghi', cl_im, bb_im)
    rows = jnp.transpose(taps.reshape(t, n_slabs, gs, S5_GROUP, S5_GROUP), (1, 0, 2, 4, 3))
    rows = rows.reshape(n_slabs, t, S5_SLAB, S5_GROUP)
    lane = jnp.arange(S5_SLAB)
    repeat = (lane[None, :] % S5_GROUP == jnp.arange(S5_GROUP)[:, None]).astype(F32)
    same_group = lane[:, None] // S5_GROUP == lane[None, :] // S5_GROUP
    fir = jnp.where(same_group, jnp.einsum('strh,hc->strc', rows, repeat), 0.0)

    eye_g = jnp.eye(gp, dtype=F32)
    eye_p = jnp.eye(n_parts, dtype=F32)

    def b_slab(bb):
        blk = jnp.einsum('abgph,gk->abghkp', bb.reshape(n_slabs, n_parts, gp, S5_STATE, S5_GROUP), eye_g)
        blk = blk.reshape(n_slabs, n_parts, gp * S5_GROUP, gp * S5_STATE)
        return (eye_p[None, :, :, None, None] * blk[:, :, None]).reshape(n_slabs, n_parts, S5_SLAB, gp * S5_STATE)

    def c_slab(cc):
        blk = jnp.einsum('abghp,gk->abgpkh', cc.reshape(n_slabs, n_parts, gp, S5_GROUP, S5_STATE), eye_g)
        blk = blk.reshape(n_slabs, n_parts, gp * S5_STATE, gp * S5_GROUP)
        return (eye_p[None, :, None, :, None] * blk[:, :, :, None]).reshape(n_slabs, n_parts, gp * S5_STATE, S5_SLAB)

    bmat = jnp.concatenate([b_slab(bb_re), b_slab(bb_im)], axis=-1)
    cmat = jnp.concatenate([c_slab(c_re), -c_slab(c_im)], axis=2)

    squares = [(pw_re[t], pw_im[t])]
    for _ in range((seq // t - 1).bit_length() - 1):
        sr, si = squares[-1]
        squares.append((sr * sr - si * si, 2.0 * sr * si))
    tab_re = jnp.concatenate([pw_re[1:], jnp.stack([s[0] for s in squares])], axis=0)
    tab_im = jnp.concatenate([pw_im[1:], jnp.stack([s[1] for s in squares])], axis=0)

    def state_lanes(x):
        return jnp.transpose(x.reshape(x.shape[0], n_slabs, n_parts, gp * S5_STATE), (1, 2, 0, 3))

    pw = jnp.concatenate([state_lanes(tab_re), state_lanes(tab_im)], axis=-1)
    return fir.astype(BF16), bmat.astype(BF16), cmat.astype(BF16), pw


def _mix_out_kernel(x_ref, ya_ref, ys_ref, u_ref, d_ref, gw_ref, gb_ref, wa_ref, wb_ref, o_ref):
    yb = _gelu_tanh(ys_ref[...] + d_ref[...] * u_ref[...])
    yb = yb * _sigmoid(_dot(yb.astype(BF16), gw_ref[...]) + gb_ref[...])
    o_ref[...] = (x_ref[...] + _dot(ya_ref[...].astype(BF16), wa_ref[...])
                  + _dot(yb.astype(BF16), wb_ref[...]))


def _mix_out(x, ya, ys, proj, d_skip, glu_w, glu_b, w_out_a, w_out_b, *, tm):
    m, d = x.shape
    width = ya.shape[1]
    u_col = proj.shape[1] // width - 1
    full = lambda arr: pl.BlockSpec(arr.shape, lambda i: (0, 0), pipeline_mode=pl.Buffered(1))
    row = lambda w, col=0: pl.BlockSpec((tm, w), lambda i: (i, col))
    return pl.pallas_call(
        _mix_out_kernel,
        grid=(m // tm,),
        in_specs=[row(d), row(width), row(width), row(width, u_col), full(d_skip), full(glu_w),
                  full(glu_b), full(w_out_a), full(w_out_b)],
        out_specs=row(d),
        out_shape=jax.ShapeDtypeStruct((m, d), F32),
        compiler_params=_compiler_params(("parallel",), 56),
        name="mix_out",
    )(x, ya, ys, proj, d_skip, glu_w, glu_b, w_out_a, w_out_b)


def _mlp_kernel(x_ref, g_ref, w1_ref, w2_ref, gn_ref, o_ref, *rest, norm_mode):
    if norm_mode == "next":
        hn_ref, h_ref = rest
    else:
        (h_ref,) = rest
    f = pl.program_id(1)

    @pl.when(f == 0)
    def _():
        x = x_ref[...]
        h_ref[...] = _rms(x, g_ref[...]).astype(BF16)
        o_ref[...] = x

    hid = jnp.maximum(_dot(h_ref[...], w1_ref[...]), 0.0)
    o_ref[...] += _dot((hid * hid).astype(BF16), w2_ref[...])

    @pl.when(f == pl.num_programs(1) - 1)
    def _():
        normed = _rms(o_ref[...], gn_ref[...])
        if norm_mode == "next":
            hn_ref[...] = normed.astype(BF16)
        else:
            o_ref[...] = normed


def _mlp(x, g, w1, w2, gn, *, norm_mode, tm, tf):
    assert norm_mode in ("final", "next")
    m, d = x.shape
    dff = w1.shape[1]
    row = pl.BlockSpec((tm, d), lambda i, f: (i, 0))
    out_specs, out_shape = [row], [jax.ShapeDtypeStruct((m, d), F32)]
    if norm_mode == "next":
        out_specs.append(row)
        out_shape.append(jax.ShapeDtypeStruct((m, d), BF16))
    return pl.pallas_call(
        functools.partial(_mlp_kernel, norm_mode=norm_mode),
        grid=(m // tm, dff // tf),
        in_specs=[row,
                  pl.BlockSpec((1, d), lambda i, f: (0, 0)),
                  pl.BlockSpec((d, tf), lambda i, f: (0, f)),
                  pl.BlockSpec((tf, d), lambda i, f: (f, 0)),
                  pl.BlockSpec((1, d), lambda i, f: (0, 0))],
        out_specs=out_specs,
        out_shape=out_shape,
        scratch_shapes=[pltpu.VMEM((tm, d), BF16)],
        compiler_params=_compiler_params(("parallel", "arbitrary"), 48),
        name="mlp",
    )(x, g, w1, w2, gn)


def _rglru_kernel(gate_ref, xr_ref, x_ref, cw_ref, cb_ref, wa_ref, wx_ref, ba_ref, bx_ref, lam_ref,
                  wout_ref, o_ref, tail_ref, h_ref, y_ref):
    c = pl.program_id(1)
    n_blocks = pl.num_programs(1) - 1
    n_heads, blk = wa_ref.shape[0], wa_ref.shape[1]
    old, new = (c + 1) % 2, c % 2

    @pl.when(c == 0)
    def _():
        tail_ref[...] = jnp.zeros_like(tail_ref)
        h_ref[...] = jnp.zeros_like(h_ref)
        y_ref[...] = jnp.zeros_like(y_ref)

    def project(h):
        cols = slice(h * blk, (h + 1) * blk)
        o_ref[:, cols] = x_ref[:, cols] + _dot(y_ref[old], wout_ref[:, cols])

    @pl.when(c < n_blocks)
    def _():
        for h in range(n_heads):
            project(h)
            cols = slice(h * blk, (h + 1) * blk)
            y_ref[new, :, cols] = _rglru_head(
                gate_ref, xr_ref, cw_ref, cb_ref, wa_ref, wx_ref, ba_ref, bx_ref, lam_ref,
                tail_ref, h_ref, h, c == 0).astype(BF16)

    @pl.when(c == n_blocks)
    def _():
        for h in range(n_heads):
            project(h)


def _rglru_head(gate_ref, xr_ref, cw_ref, cb_ref, wa_ref, wx_ref, ba_ref, bx_ref, lam_ref,
                tail_ref, h_ref, head, is_first):
    tt = xr_ref.shape[0]
    blk = wa_ref.shape[1]
    cols = slice(head * blk, (head + 1) * blk)

    xr = xr_ref[:, cols]
    tail = tail_ref[:, cols]
    row8 = lax.broadcasted_iota(jnp.int32, (SUBLANES, blk), 0)
    xc = xr * cw_ref[CONV_WIDTH - 1:CONV_WIDTH, cols] + cb_ref[:, cols]
    for s in range(1, CONV_WIDTH):
        xs = pltpu.roll(xr, s, axis=0)
        top = jnp.where(row8 < s, pltpu.roll(tail, s, axis=0), xs[:SUBLANES])
        xs = jnp.concatenate([top, xs[SUBLANES:]], axis=0)
        xc = xc + xs * cw_ref[CONV_WIDTH - 1 - s:CONV_WIDTH - s, cols]
    tail_ref[:, cols] = xr[tt - SUBLANES:tt, :]

    xcb = xc.astype(BF16)
    gate_r = _sigmoid(_dot(xcb, wa_ref[head]) + ba_ref[:, cols])
    gate_i = _sigmoid(_dot(xcb, wx_ref[head]) + bx_ref[:, cols])
    log_a = LRU_C * gate_r * (-_softplus(-lam_ref[:, cols]))
    a = jnp.exp(log_a)
    rest = 1.0 - a * a
    mult = jnp.where(rest > 0.0, rest * lax.rsqrt(rest), 0.0)
    top = jnp.where((row8 == 0) & is_first, 1.0, mult[:SUBLANES])
    mult = jnp.concatenate([top, mult[SUBLANES:]], axis=0)
    b = xc * gate_i * mult

    n_groups = tt // SUBLANES
    a3 = a.reshape(n_groups, SUBLANES, blk)
    b3 = b.reshape(n_groups, SUBLANES, blk)
    sub = lax.broadcasted_iota(jnp.int32, a3.shape, 1)
    shift = 1
    while shift < SUBLANES:
        valid = sub >= shift
        b3 = b3 + a3 * jnp.where(valid, pltpu.roll(b3, shift, axis=1), 0.0)
        a3 = a3 * jnp.where(valid, pltpu.roll(a3, shift, axis=1), 1.0)
        shift *= 2
    carry = h_ref[0:1, cols]
    groups = []
    for i in range(n_groups):
        h_i = b3[i] + a3[i] * carry
        groups.append(h_i)
        carry = h_i[SUBLANES - 1:SUBLANES, :]
    h_ref[:, cols] = jnp.broadcast_to(carry, (SUBLANES, blk))
    return gate_ref[:, cols] * jnp.concatenate(groups, axis=0)


def _rglru(proj, x, conv_w, conv_b, w_a, w_x, b_a, b_x, lam, w_out, *, batch, seq, tt):
    width = lam.shape[1]
    d = x.shape[1]
    nc = seq // tt
    cur = lambda col: pl.BlockSpec((tt, width), lambda b, c: (b * nc + jnp.minimum(c, nc - 1), col))
    prev = pl.BlockSpec((tt, d), lambda b, c: (b * nc + jnp.maximum(c - 1, 0), 0))
    par = lambda arr: pl.BlockSpec(arr.shape, lambda b, c: (0,) * arr.ndim, pipeline_mode=pl.Buffered(1))
    return pl.pallas_call(
        _rglru_kernel,
        grid=(batch, nc + 1),
        in_specs=[cur(0), cur(1), prev, par(conv_w), par(conv_b), par(w_a), par(w_x),
                  par(b_a), par(b_x), par(lam), par(w_out)],
        out_specs=prev,
        out_shape=jax.ShapeDtypeStruct((batch * seq, d), F32),
        scratch_shapes=[pltpu.VMEM((SUBLANES, width), F32), pltpu.VMEM((SUBLANES, width), F32),
                        pltpu.VMEM((2, tt, width), BF16)],
        compiler_params=_compiler_params(("parallel", "arbitrary"), 56),
        name="rglru",
    )(proj, proj, x, conv_w, conv_b, w_a, w_x, b_a, b_x, lam, w_out)


def _pad_cols(w, n):
    return jnp.pad(w, ((0, 0), (0, n - w.shape[1])))


def _pad_rows(w, n):
    return jnp.pad(w, ((0, n - w.shape[0]), (0, 0)))


def kernel(x, norm_mix_g, norm_mlp_g, norm_f_g, mlp_w1, mlp_w2, hy_w_in, hy_w_out, rw_mu_rkv, rw_mu_wag, rw_w0, rw_w1, rw_w2, rw_a0, rw_a1, rw_a2, rw_g1, rw_g2, rw_k_k, rw_k_a, rw_r_k, rw_ln_w, rw_ln_b, s5_lam_re, s5_lam_im, s5_log_dt, s5_b_re, s5_b_im, s5_c_re, s5_c_im, s5_d, s5_glu_w, s5_glu_b, rg_w_in, rg_w_out, rg_conv_w, rg_conv_b, rg_w_a, rg_b_a, rg_w_x, rg_b_x, rg_lam):
    batch, seq, d = x.shape
    m = batch * seq
    rw = rw_w2.shape[-1]
    row = lambda v: v.reshape(1, -1)
    bf = lambda w: w.astype(BF16)
    lane = 128
    xf = x.reshape(m, d)

    lw, a, gate, proj = _rwkv_lora(
        xf, row(norm_mix_g[0]), rw_mu_wag[0],
        bf(_pad_cols(rw_w1[0], lane)), bf(_pad_cols(rw_a1[0], lane)), bf(rw_g1[0]),
        bf(_pad_rows(rw_w2[0], lane)), bf(_pad_rows(rw_a2[0], lane)), bf(rw_g2[0]),
        row(rw_w0[0]), row(rw_a0[0]), bf(hy_w_in[0]), seq=seq, tm=256)
    ya = _rwkv_core(proj, lw, a, gate, rw_mu_rkv[0], row(rw_k_k[0]), row(rw_k_a[0]),
                    row(rw_r_k[0]), row(rw_ln_w[0]), row(rw_ln_b[0]), batch=batch, seq=seq)

    mats = _s5_matrices(s5_lam_re[0], s5_lam_im[0], s5_log_dt[0], s5_b_re[0], s5_b_im[0],
                        s5_c_re[0], s5_c_im[0], seq=seq)
    ys = _s5_core(proj, *mats, batch=batch, seq=seq, width=s5_d.shape[-1])

    x1 = _mix_out(xf, ya, ys, proj, row(s5_d[0]), bf(s5_glu_w[0]), row(s5_glu_b[0]),
                  bf(hy_w_out[0][:rw]), bf(hy_w_out[0][rw:]), tm=512)
    x2, hn1 = _mlp(x1, row(norm_mlp_g[0]), bf(mlp_w1[0]), bf(mlp_w2[0]), row(norm_mix_g[1]),
                   norm_mode="next", tm=512, tf=1024)

    proj1 = _rg_in_proj(hn1, bf(rg_w_in[0]), tm=1024, tn=1024)
    x3 = _rglru(proj1, x2, rg_conv_w[0], row(rg_conv_b[0]), bf(rg_w_a[0]), bf(rg_w_x[0]),
                row(rg_b_a[0]), row(rg_b_x[0]), row(rg_lam[0]), bf(rg_w_out[0]),
                batch=batch, seq=seq, tt=256)
    (out,) = _mlp(x3, row(norm_mlp_g[1]), bf(mlp_w1[1]), bf(mlp_w2[1]), row(norm_f_g),
                  norm_mode="final", tm=512, tf=1024)
    return out.reshape(batch, seq, d)
```
